```python
import jax, jax.numpy as jnp
from jax import lax
import numpy as np

D_MODEL = 1024
BATCH = 2
SEQ = 8192
DEPTH = 4
DEC_BATCH = 128
DEC_SEQ = 8
PAST_LEN = 8192
PAGE_SIZE = 128

N_HEADS = 8
N_KV_HEADS = 2
HEAD_DIM = 64
Q_GROUP = N_HEADS // N_KV_HEADS
ATTN_WIDTH = N_HEADS * HEAD_DIM
KV_WIDTH = N_KV_HEADS * HEAD_DIM
WINDOW = 128
CONV_WIDTH = D_MODEL // 2
CONV_K = 3
CHUNK = 128
MLP_WIDTH = D_MODEL // 2
N_SPATIAL_GROUPS = 4
SPATIAL_GROUP_DIM = MLP_WIDTH // N_SPATIAL_GROUPS
N_BRANCHES = 3
EPS = 1e-6
NEG_INF = -1e30

COL_SIZES = (ATTN_WIDTH, KV_WIDTH, KV_WIDTH, ATTN_WIDTH,
             CONV_WIDTH, CONV_WIDTH, CONV_WIDTH, CONV_WIDTH,
             MLP_WIDTH, MLP_WIDTH, MLP_WIDTH,
             N_BRANCHES * D_MODEL)
IN_COLS = sum(COL_SIZES)

kernel_name = "hybrid_swa_shortconv_chunkmlp_decode_step"


def _rmsnorm(x, w):
    xf = x.astype(jnp.float32)
    r = lax.rsqrt(jnp.mean(xf * xf, axis=-1, keepdims=True) + EPS)
    return (xf * r).astype(x.dtype) * w


def _alibi_slopes():
    h = jnp.arange(1, N_HEADS + 1, dtype=jnp.float32)
    return jnp.exp2(-8.0 * h / N_HEADS).reshape(N_KV_HEADS, Q_GROUP)


def _split_cols(h):
    parts = []
    start = 0
    for size in COL_SIZES:
        parts.append(h[..., start:start + size])
        start += size
    return parts


def _mixer_inputs(x, p):
    n, t = x.shape[0], x.shape[1]
    xn = _rmsnorm(x, p["norm_w"])
    h = jnp.einsum("btd,de->bte", xn, p["w_in"])
    q, k, v, z_a, gate_b, gate_c, h_b, z_b, u, v_c, z_c, g = _split_cols(h)
    q = _rmsnorm(q.reshape(n, t, N_KV_HEADS, Q_GROUP, HEAD_DIM), p["q_norm_w"])
    k = _rmsnorm(k.reshape(n, t, N_KV_HEADS, HEAD_DIM), p["k_norm_w"])
    v = v.reshape(n, t, N_KV_HEADS, HEAD_DIM)
    conv_in = gate_c * h_b
    v_c = _rmsnorm(v_c, p["v_norm_w"])
    gates = jax.nn.sigmoid(g + p["b_gate"]).reshape(n, t, N_BRANCHES, D_MODEL)
    return q, k, v, z_a, gate_b, conv_in, z_b, u, v_c, z_c, gates


def _sink_attend(q, k, v, dist, valid, sinks):
    s = jnp.einsum("...qkgd,...skd->...kgqs", q, k).astype(jnp.float32) * (HEAD_DIM ** -0.5)
    s = s - _alibi_slopes()[:, :, None, None] * dist.astype(jnp.float32)
    s = jnp.where(valid, s, NEG_INF)
    sink = sinks.astype(jnp.float32).reshape(N_KV_HEADS, Q_GROUP, 1, 1)
    mx = jnp.maximum(s.max(axis=-1, keepdims=True), sink)
    e = jnp.exp(s - mx)
    prob = e / (e.sum(axis=-1, keepdims=True) + jnp.exp(sink - mx))
    return jnp.einsum("...kgqs,...skd->...qkgd", prob.astype(v.dtype), v)


def _attn_prompt(q, k, v, sinks):
    b, s = q.shape[0], q.shape[1]
    nb = s // WINDOW
    qb = q.reshape(b, nb, WINDOW, N_KV_HEADS, Q_GROUP, HEAD_DIM)
    kb = k.reshape(b, nb, WINDOW, N_KV_HEADS, HEAD_DIM)
    vb = v.reshape(b, nb, WINDOW, N_KV_HEADS, HEAD_DIM)

    def with_prev(t):
        prev = jnp.concatenate([jnp.zeros_like(t[:, :1]), t[:, :-1]], axis=1)
        return jnp.concatenate([prev, t], axis=2)

    kk, vv = with_prev(kb), with_prev(vb)
    i = jnp.arange(WINDOW)[:, None]
    j = jnp.arange(2 * WINDOW)[None, :]
    dist = i + WINDOW - j
    band = (dist >= 0) & (dist < WINDOW)
    has_prev = (jnp.arange(nb)[:, None, None] > 0) | (j[None] >= WINDOW)
    valid = (band[None] & has_prev)[:, None, None]
    out = _sink_attend(qb, kk, vv, dist, valid, sinks)
    return out.reshape(b, s, ATTN_WIDTH)


def _attn_sample(q, k, v, cache_k, cache_v, sinks):
    n, t = q.shape[0], q.shape[1]
    w = cache_k.shape[1]
    kk = jnp.concatenate([cache_k, k], axis=1)
    vv = jnp.concatenate([cache_v, v], axis=1)
    i = jnp.arange(t)[:, None]
    j = jnp.arange(w + t)[None, :]
    dist = i + w - j
    valid = (dist >= 0) & (dist < WINDOW)
    out = _sink_attend(q, kk, vv, dist, valid, sinks)
    return out.reshape(n, t, ATTN_WIDTH), kk[:, -w:], vv[:, -w:]


def _causal_conv(prev, xc, w):
    t = xc.shape[1]
    xp = jnp.concatenate([prev, xc], axis=1)
    out = w[0] * xp[:, 0:t]
    for r in range(1, CONV_K):
        out = out + w[r] * xp[:, r:r + t]
    return out, xp[:, t:]


def _spatial_gate(u, v_c, w_s, b_s):
    n, t = u.shape[0], u.shape[1]
    length = min(t, CHUNK)
    nc = t // length
    mask = jnp.tril(jnp.ones((length, length), dtype=bool))
    wm = jnp.where(mask, w_s[:, :length, :length], 0.0)
    vb = v_c.reshape(n, nc, length, N_SPATIAL_GROUPS, SPATIAL_GROUP_DIM)
    sp = jnp.einsum("gts,bnsgc->bntgc", wm, vb) + b_s[:, :length].T[:, :, None]
    return u * sp.reshape(n, t, MLP_WIDTH)


def _merge(x, y_a, y_b, y_c, z_a, z_b, z_c, gates, p):
    a = jnp.einsum("bte,ed->btd", jax.nn.silu(z_a) * y_a, p["w_out_a"])
    b = jnp.einsum("bte,ed->btd", jax.nn.silu(z_b) * y_b, p["w_out_b"])
    c = jnp.einsum("bte,ed->btd", jax.nn.silu(z_c) * y_c, p["w_out_c"])
    m = gates[:, :, 0] * a + gates[:, :, 1] * b + gates[:, :, 2] * c
    return x + jnp.einsum("btd,de->bte", m, p["w_o"])


def setup_inputs(seed: int = 0) -> dict:
    key = jax.random.key(seed)
    ks = jax.random.split(key, 24)
    f32 = jnp.float32
    w_buf = min(WINDOW, PAST_LEN)

    def nrm(k, shape, scale):
        return jax.random.normal(k, shape, f32) * scale

    return {
        "x_prompt": nrm(ks[0], (BATCH, SEQ, D_MODEL), 1.0),
        "x_sample": nrm(ks[1], (DEC_BATCH, DEC_SEQ, D_MODEL), 1.0),
        "cache_k": nrm(ks[2], (DEPTH, DEC_BATCH, w_buf, N_KV_HEADS, HEAD_DIM), 1.0),
        "cache_v": nrm(ks[3], (DEPTH, DEC_BATCH, w_buf, N_KV_HEADS, HEAD_DIM), 1.0),
        "state_conv": nrm(ks[4], (DEPTH, DEC_BATCH, CONV_K - 1, CONV_WIDTH), 1.0),
        "norm_w": 1.0 + nrm(ks[5], (DEPTH, D_MODEL), 0.02),
        "w_in": nrm(ks[6], (DEPTH, D_MODEL, IN_COLS), D_MODEL ** -0.5),
        "b_gate": nrm(ks[7], (DEPTH, N_BRANCHES * D_MODEL), 0.02),
        "q_norm_w": 1.0 + nrm(ks[8], (DEPTH, HEAD_DIM), 0.02),
        "k_norm_w": 1.0 + nrm(ks[9], (DEPTH, HEAD_DIM), 0.02),
        "sinks": nrm(ks[10], (DEPTH, N_HEADS), 0.5),
        "conv_w": nrm(ks[11], (DEPTH, CONV_K, CONV_WIDTH), CONV_K ** -0.5),
        "v_norm_w": 1.0 + nrm(ks[12], (DEPTH, MLP_WIDTH), 0.02),
        "w_spatial": nrm(ks[13], (DEPTH, N_SPATIAL_GROUPS, CHUNK, CHUNK), CHUNK ** -0.5),
        "b_spatial": 1.0 + nrm(ks[14], (DEPTH, N_SPATIAL_GROUPS, CHUNK), 0.1),
        "w_out_a": nrm(ks[15], (DEPTH, ATTN_WIDTH, D_MODEL), ATTN_WIDTH ** -0.5),
        "w_out_b": nrm(ks[16], (DEPTH, CONV_WIDTH, D_MODEL), CONV_WIDTH ** -0.5),
        "w_out_c": nrm(ks[17], (DEPTH, MLP_WIDTH, D_MODEL), MLP_WIDTH ** -0.5),
        "w_o": nrm(ks[18], (DEPTH, D_MODEL, D_MODEL), D_MODEL ** -0.5),
    }


def reference(x_prompt, x_sample, cache_k, cache_v, state_conv, norm_w, w_in, b_gate,
              q_norm_w, k_norm_w, sinks, conv_w, v_norm_w, w_spatial, b_spatial,
              w_out_a, w_out_b, w_out_c, w_o):
    yp, ys = x_prompt, x_sample
    pk, pv, pc, sk, sv, sc, scv = [], [], [], [], [], [], []
    for l in range(DEPTH):
        p = {"norm_w": norm_w[l], "w_in": w_in[l], "b_gate": b_gate[l],
             "q_norm_w": q_norm_w[l], "k_norm_w": k_norm_w[l], "v_norm_w": v_norm_w[l],
             "w_out_a": w_out_a[l], "w_out_b": w_out_b[l], "w_out_c": w_out_c[l], "w_o": w_o[l]}

        q, k, v, z_a, gate_b, conv_in, z_b, u, v_c, z_c, gates = _mixer_inputs(yp, p)
        y_a = _attn_prompt(q, k, v, sinks[l])
        conv_out, conv_last = _causal_conv(
            jnp.zeros((yp.shape[0], CONV_K - 1, CONV_WIDTH), yp.dtype), conv_in, conv_w[l])
        y_c = _spatial_gate(u, v_c, w_spatial[l], b_spatial[l])
        w_p = min(WINDOW, yp.shape[1])
        pk.append(k[:, -w_p:])
        pv.append(v[:, -w_p:])
        pc.append(conv_last)
        yp = _merge(yp, y_a, gate_b * conv_out, y_c, z_a, z_b, z_c, gates, p)

        q, k, v, z_a, gate_b, conv_in, z_b, u, v_c, z_c, gates = _mixer_inputs(ys, p)
        y_a, new_k, new_v = _attn_sample(q, k, v, cache_k[l], cache_v[l], sinks[l])
        conv_out, conv_last = _causal_conv(state_conv[l], conv_in, conv_w[l])
        y_c = _spatial_gate(u, v_c, w_spatial[l], b_spatial[l])
        sk.append(new_k)
        sv.append(new_v)
        sc.append(conv_last)
        scv.append(v_c)
        ys = _merge(ys, y_a, gate_b * conv_out, y_c, z_a, z_b, z_c, gates, p)

    return (yp, ys, jnp.stack(pk), jnp.stack(pv), jnp.stack(pc),
            jnp.stack(sk), jnp.stack(sv), jnp.stack(sc), jnp.stack(scv))
```

```python
import functools

import jax
import jax.numpy as jnp
from jax import lax
from jax.experimental import pallas as pl
from jax.experimental.pallas import tpu as pltpu

F32 = jnp.float32
BF16 = jnp.bfloat16

D_MODEL = 1024
DEPTH = 4
N_HEADS = 8
N_KV_HEADS = 2
HEAD_DIM = 64
Q_GROUP = N_HEADS // N_KV_HEADS
ATTN_WIDTH = N_HEADS * HEAD_DIM
KV_WIDTH = N_KV_HEADS * HEAD_DIM
WINDOW = 128
CONV_WIDTH = 512
CONV_K = 3
CHUNK = 128
MLP_WIDTH = 512
N_SPATIAL_GROUPS = 4
EPS = 1e-6
NEG_INF = -1e30

COL_A = 0
COL_B = COL_A + 2 * ATTN_WIDTH + 2 * KV_WIDTH
COL_C = COL_B + 4 * CONV_WIDTH
COL_G = COL_C + 3 * MLP_WIDTH
IN_COLS = COL_G + 3 * D_MODEL

LANES = 128
SUBLANES = 8
PROMPT_ROWS = 256
SAMPLE_ROWS = 128
SAMPLE_KEYS = 2 * WINDOW
VMEM_LIMIT_BYTES = 56 * 1024 * 1024


def _dot(a, b):
    return jnp.dot(a, b, preferred_element_type=F32)


def _dot_nt(a, b):
    return lax.dot_general(a, b, (((1,), (1,)), ((), ())), preferred_element_type=F32)


def _rms(x, w):
    ms = jnp.mean(x * x, axis=-1, keepdims=True)
    return (x * lax.rsqrt(ms + EPS)) * w


def _lo_lanes(shape):
    return lax.broadcasted_iota(jnp.int32, shape, len(shape) - 1) < HEAD_DIM


def _pair_rms(x, w):
    lo = _lo_lanes(x.shape)
    sq = x * x
    s_lo = jnp.sum(jnp.where(lo, sq, 0.0), axis=-1, keepdims=True)
    s_hi = jnp.sum(jnp.where(lo, 0.0, sq), axis=-1, keepdims=True)
    ms = jnp.where(lo, s_lo, s_hi) * (1.0 / HEAD_DIM)
    return (x * lax.rsqrt(ms + EPS)) * w


def _norm_heads(x, w):
    groups = [_pair_rms(x[:, g * LANES:(g + 1) * LANES], w) for g in range(x.shape[1] // LANES)]
    return groups[0] if len(groups) == 1 else jnp.concatenate(groups, axis=1)


def _softmax_pv(s, sink, v):
    mx = jnp.maximum(jnp.max(s, axis=-1, keepdims=True), sink)
    e = jnp.exp(s - mx)
    den = jnp.sum(e, axis=-1, keepdims=True) + jnp.exp(sink - mx)
    return _dot(e.astype(BF16), v) * (1.0 / den)


def _tail(x, xn, y_a, z_a, conv_shift, mix_mask, w_in_ref, b_gate_ref, conv_w_ref, vnw_ref,
          mix_ref, spb_ref, woa_ref, wob_ref, woc_ref, wo_ref):
    rows = x.shape[0]

    def gate(idx):
        g = _dot(xn, w_in_ref[:, COL_G + idx * D_MODEL:COL_G + (idx + 1) * D_MODEL])
        return jax.nn.sigmoid(g + b_gate_ref[:, idx * D_MODEL:(idx + 1) * D_MODEL])

    a = _dot((jax.nn.silu(z_a) * y_a).astype(BF16), woa_ref[...])
    m = gate(0) * a

    hb = _dot(xn, w_in_ref[:, COL_B:COL_C])
    gate_b = hb[:, 0:CONV_WIDTH]
    gate_c = hb[:, CONV_WIDTH:2 * CONV_WIDTH]
    h_b = hb[:, 2 * CONV_WIDTH:3 * CONV_WIDTH]
    z_b = hb[:, 3 * CONV_WIDTH:4 * CONV_WIDTH]
    conv_in = gate_c * h_b
    xm1, xm2 = conv_shift(conv_in)
    cw = conv_w_ref[...]
    conv_out = cw[0:1] * xm2 + cw[1:2] * xm1 + cw[2:3] * conv_in
    y_b = gate_b * conv_out
    b = _dot((jax.nn.silu(z_b) * y_b).astype(BF16), wob_ref[...])
    m = m + gate(1) * b

    hc = _dot(xn, w_in_ref[:, COL_C:COL_G])
    u = hc[:, 0:MLP_WIDTH]
    v_c = _rms(hc[:, MLP_WIDTH:2 * MLP_WIDTH], vnw_ref[...])
    z_c = hc[:, 2 * MLP_WIDTH:3 * MLP_WIDTH]
    vb = v_c.astype(BF16)
    spb = spb_ref[...]
    gw = MLP_WIDTH // N_SPATIAL_GROUPS
    mixes = [jnp.where(mix_mask, mix_ref[g], 0.0).astype(BF16) for g in range(N_SPATIAL_GROUPS)]
    sp_chunks = []
    for c in range(rows // CHUNK):
        r0 = c * CHUNK
        parts = [_dot(mixes[g], vb[r0:r0 + CHUNK, g * gw:(g + 1) * gw])
                 for g in range(N_SPATIAL_GROUPS)]
        sp_chunks.append(jnp.concatenate(parts, axis=1) + spb)
    sp = sp_chunks[0] if len(sp_chunks) == 1 else jnp.concatenate(sp_chunks, axis=0)
    y_c = u * sp
    c_out = _dot((jax.nn.silu(z_c) * y_c).astype(BF16), woc_ref[...])
    m = m + gate(2) * c_out

    y = x + _dot(m.astype(BF16), wo_ref[...])
    return y, conv_in, v_c


def _slope(head):
    return 2.0 ** (-(head + 1))


def _prompt_kernel(sinks_ref, x_ref, norm_w_ref, w_in_ref, b_gate_ref, qw_ref, kw_ref, conv_w_ref,
                   vnw_ref, mix_ref, spb_ref, woa_ref, wob_ref, woc_ref, wo_ref,
                   y_ref, pk_ref, pv_ref, pc_ref,
                   bias_scr, ka_scr, kb_scr, va_scr, vb_scr, carry_scr, *, rows, nblk):
    i = pl.program_id(0)
    first = lax.rem(i, nblk) == 0
    nsub = rows // WINDOW
    stack = 2 * WINDOW

    @pl.when(i == 0)
    def _():
        r = lax.broadcasted_iota(jnp.int32, (stack, 2 * WINDOW), 0)
        col = lax.broadcasted_iota(jnp.int32, (stack, 2 * WINDOW), 1)
        top = r < WINDOW
        dist = jnp.where(top, r, r - WINDOW) + WINDOW - col
        band = (dist >= 0) & (dist < WINDOW)
        distf = dist.astype(F32)
        for flag in range(2):
            valid = band & (col >= WINDOW) if flag else band
            for h in range(N_KV_HEADS):
                for half in range(2):
                    slope = jnp.where(top, _slope(h * Q_GROUP + half), _slope(h * Q_GROUP + 2 + half))
                    bias_scr[flag * 4 + h * 2 + half] = jnp.where(valid, -(slope * distf), NEG_INF)

    @pl.when(first)
    def _():
        zeros = jnp.zeros((WINDOW, KV_WIDTH), BF16)
        ka_scr[0:WINDOW, :] = zeros
        kb_scr[0:WINDOW, :] = zeros
        va_scr[0:WINDOW, :] = zeros
        vb_scr[0:WINDOW, :] = zeros
        carry_scr[...] = jnp.zeros(carry_scr.shape, F32)

    x = x_ref[...]
    xn = _rms(x, norm_w_ref[...]).astype(BF16)
    ha = _dot(xn, w_in_ref[:, COL_A:COL_B])
    q = _norm_heads(ha[:, 0:ATTN_WIDTH], qw_ref[...]) * (HEAD_DIM ** -0.5)
    k = _pair_rms(ha[:, ATTN_WIDTH:ATTN_WIDTH + KV_WIDTH], kw_ref[...])
    v = ha[:, ATTN_WIDTH + KV_WIDTH:ATTN_WIDTH + 2 * KV_WIDTH]
    z_a = ha[:, ATTN_WIDTH + 2 * KV_WIDTH:COL_B]

    pk_ref[0] = k[rows - WINDOW:rows, :]
    pv_ref[0] = v[rows - WINDOW:rows, :]
    ka_scr[WINDOW:WINDOW + rows, :] = k.astype(BF16)
    kb_scr[WINDOW:WINDOW + rows, :] = pltpu.roll(k, HEAD_DIM, axis=1).astype(BF16)
    va_scr[WINDOW:WINDOW + rows, :] = v.astype(BF16)
    vb_scr[WINDOW:WINDOW + rows, :] = pltpu.roll(v, HEAD_DIM, axis=1).astype(BF16)

    lo = _lo_lanes((WINDOW, LANES))
    srow = lax.broadcasted_iota(jnp.int32, (stack, 1), 0) < WINDOW
    flag = first.astype(jnp.int32)
    y_rows = []
    for c in range(nsub):
        r0 = c * WINDOW
        keys = slice(r0, r0 + 2 * WINDOW)
        k_nat, k_swp = ka_scr[keys, :], kb_scr[keys, :]
        v_nat, v_swp = va_scr[keys, :], vb_scr[keys, :]
        pairs = []
        for h in range(N_KV_HEADS):
            p0 = q[r0:r0 + WINDOW, (2 * h) * LANES:(2 * h + 1) * LANES]
            p1 = q[r0:r0 + WINDOW, (2 * h + 1) * LANES:(2 * h + 2) * LANES]
            outs = []
            for half in range(2):
                keep = lo if half == 0 else jnp.logical_not(lo)
                qs = jnp.concatenate([jnp.where(keep, p0, 0.0), jnp.where(keep, p1, 0.0)],
                                     axis=0).astype(BF16)
                aligned = (h == half)
                kk = k_nat if aligned else k_swp
                vv = v_nat if aligned else v_swp
                tbl = h * 2 + half
                bias = bias_scr[flag * 4 + tbl] if c == 0 else bias_scr[tbl]
                s = _dot_nt(qs, kk) + bias
                sink = jnp.where(srow, sinks_ref[h * Q_GROUP + half], sinks_ref[h * Q_GROUP + 2 + half])
                outs.append(_softmax_pv(s, sink, vv))
            pairs.append(jnp.where(lo, outs[0][0:WINDOW], outs[1][0:WINDOW]))
            pairs.append(jnp.where(lo, outs[0][WINDOW:stack], outs[1][WINDOW:stack]))
        y_rows.append(jnp.concatenate(pairs, axis=1))
    y_a = y_rows[0] if nsub == 1 else jnp.concatenate(y_rows, axis=0)

    ka_scr[0:WINDOW, :] = ka_scr[rows:rows + WINDOW, :]
    kb_scr[0:WINDOW, :] = kb_scr[rows:rows + WINDOW, :]
    va_scr[0:WINDOW, :] = va_scr[rows:rows + WINDOW, :]
    vb_scr[0:WINDOW, :] = vb_scr[rows:rows + WINDOW, :]

    carry = carry_scr[...]
    prev1 = carry[SUBLANES - 1:SUBLANES, :]
    prev2 = carry[SUBLANES - 2:SUBLANES - 1, :]

    def conv_shift(ci):
        rid = lax.broadcasted_iota(jnp.int32, ci.shape, 0)
        xm1 = jnp.where(rid == 0, prev1, pltpu.roll(ci, 1, axis=0))
        xm2 = jnp.where(rid == 0, prev2, jnp.where(rid == 1, prev1, pltpu.roll(ci, 2, axis=0)))
        return xm1, xm2

    mr = lax.broadcasted_iota(jnp.int32, (CHUNK, CHUNK), 0)
    mc = lax.broadcasted_iota(jnp.int32, (CHUNK, CHUNK), 1)
    y, conv_in, _ = _tail(x, xn, y_a, z_a, conv_shift, mc <= mr, w_in_ref, b_gate_ref, conv_w_ref,
                          vnw_ref, mix_ref, spb_ref, woa_ref, wob_ref, woc_ref, wo_ref)
    y_ref[...] = y
    last = conv_in[rows - SUBLANES:rows, :]
    carry_scr[...] = last
    pc_ref[0] = last


def _sample_kernel(sinks_ref, x_ref, ck_ref, cv_ref, st_ref, norm_w_ref, w_in_ref, b_gate_ref, qw_ref,
                   kw_ref, conv_w_ref, vnw_ref, mix_ref, spb_ref, woa_ref, wob_ref, woc_ref, wo_ref,
                   y_ref, sk_ref, sv_ref, ci_ref, scv_ref,
                   bias_scr, kcat_scr, vcat_scr, q_scr, ya_scr, *, rows, dec_seq):
    i = pl.program_id(0)
    nseq = rows // dec_seq
    w = WINDOW
    stack = Q_GROUP * dec_seq

    @pl.when(i == 0)
    def _():
        r = lax.broadcasted_iota(jnp.int32, (stack, SAMPLE_KEYS), 0)
        col = lax.broadcasted_iota(jnp.int32, (stack, SAMPLE_KEYS), 1)
        g = r // dec_seq
        dist = (r - g * dec_seq) + w - col
        valid = (dist >= 0) & (dist < WINDOW)
        distf = dist.astype(F32)
        for h in range(N_KV_HEADS):
            slope = jnp.zeros((stack, SAMPLE_KEYS), F32)
            for gg in range(Q_GROUP):
                slope = jnp.where(g == gg, _slope(h * Q_GROUP + gg), slope)
            bias_scr[h] = jnp.where(valid, -(slope * distf), NEG_INF)
        pad = jnp.zeros((nseq, SAMPLE_KEYS - w - dec_seq, KV_WIDTH), F32)
        kcat_scr[:, w + dec_seq:SAMPLE_KEYS, :] = pad
        vcat_scr[:, w + dec_seq:SAMPLE_KEYS, :] = pad

    x = x_ref[...]
    xn = _rms(x, norm_w_ref[...]).astype(BF16)
    ha = _dot(xn, w_in_ref[:, COL_A:COL_B])
    q_scr[...] = _norm_heads(ha[:, 0:ATTN_WIDTH], qw_ref[...]) * (HEAD_DIM ** -0.5)
    k = _pair_rms(ha[:, ATTN_WIDTH:ATTN_WIDTH + KV_WIDTH], kw_ref[...])
    v = ha[:, ATTN_WIDTH + KV_WIDTH:ATTN_WIDTH + 2 * KV_WIDTH]
    z_a = ha[:, ATTN_WIDTH + 2 * KV_WIDTH:COL_B]

    k3 = k.reshape(nseq, dec_seq, KV_WIDTH)
    v3 = v.reshape(nseq, dec_seq, KV_WIDTH)
    ck = ck_ref[...]
    cv = cv_ref[...]
    kcat_scr[:, 0:w, :] = ck
    vcat_scr[:, 0:w, :] = cv
    kcat_scr[:, w:w + dec_seq, :] = k3
    vcat_scr[:, w:w + dec_seq, :] = v3
    sk_ref[:, 0:w - dec_seq, :] = ck[:, dec_seq:w, :]
    sv_ref[:, 0:w - dec_seq, :] = cv[:, dec_seq:w, :]
    sk_ref[:, w - dec_seq:w, :] = k3
    sv_ref[:, w - dec_seq:w, :] = v3

    lo = _lo_lanes((dec_seq, LANES))
    grow = lax.broadcasted_iota(jnp.int32, (stack, 1), 0) // dec_seq

    def seq_body(b, carry):
        r0 = pl.multiple_of(b * dec_seq, dec_seq)
        qb = q_scr[pl.ds(r0, dec_seq), :]
        kb = kcat_scr[b].astype(BF16)
        vb = vcat_scr[b].astype(BF16)
        pairs = []
        for h in range(N_KV_HEADS):
            keep = lo if h == 0 else jnp.logical_not(lo)
            pieces = []
            for g in range(Q_GROUP):
                pair = qb[:, (h * 2 + g // 2) * LANES:(h * 2 + g // 2 + 1) * LANES]
                if g % 2 != h:
                    pair = pltpu.roll(pair, HEAD_DIM, axis=1)
                pieces.append(jnp.where(keep, pair, 0.0))
            qs = jnp.concatenate(pieces, axis=0).astype(BF16)
            s = _dot_nt(qs, kb) + bias_scr[h]
            sink = jnp.zeros((stack, 1), F32)
            for g in range(Q_GROUP):
                sink = jnp.where(grow == g, sinks_ref[h * Q_GROUP + g], sink)
            o = _softmax_pv(s, sink, vb)
            heads = []
            for g in range(Q_GROUP):
                og = o[g * dec_seq:(g + 1) * dec_seq, :]
                if g % 2 != h:
                    og = pltpu.roll(og, HEAD_DIM, axis=1)
                heads.append(og)
            pairs.append(jnp.where(lo, heads[0], heads[1]))
            pairs.append(jnp.where(lo, heads[2], heads[3]))
        ya_scr[pl.ds(r0, dec_seq), :] = jnp.concatenate(pairs, axis=1)
        return carry

    lax.fori_loop(0, nseq, seq_body, 0)
    y_a = ya_scr[...]

    st = st_ref[...]

    def conv_shift(ci):
        t = lax.rem(lax.broadcasted_iota(jnp.int32, ci.shape, 0), dec_seq)
        xm1 = jnp.where(t == 0, pltpu.roll(st, rows - 1, axis=0), pltpu.roll(ci, 1, axis=0))
        xm2 = jnp.where(t < CONV_K - 1, st, pltpu.roll(ci, 2, axis=0))
        return xm1, xm2

    mr = lax.broadcasted_iota(jnp.int32, (CHUNK, CHUNK), 0)
    mc = lax.broadcasted_iota(jnp.int32, (CHUNK, CHUNK), 1)
    mix_mask = (mr // dec_seq == mc // dec_seq) & (mc <= mr)
    y, conv_in, v_c = _tail(x, xn, y_a, z_a, conv_shift, mix_mask, w_in_ref, b_gate_ref, conv_w_ref,
                            vnw_ref, mix_ref, spb_ref, woa_ref, wob_ref, woc_ref, wo_ref)
    y_ref[...] = y
    ci_ref[...] = conv_in
    scv_ref[...] = v_c


def _resident(shape):
    nd = len(shape)
    return pl.BlockSpec(shape, lambda i, _nd=nd: (0,) * _nd, pipeline_mode=pl.Buffered(1))


def _weight_specs(wl):
    return [_resident(wl[name].shape) for name in
            ("norm_w", "w_in", "b_gate", "qw", "kw", "conv_w", "vnw")]


def _prompt_layer(x, wl, mix, spb, batch):
    n, d = x.shape
    rows = PROMPT_ROWS
    nblk = n // batch // rows
    smem = pl.BlockSpec(memory_space=pltpu.SMEM)
    in_specs = ([smem, pl.BlockSpec((rows, d), lambda i: (i, 0))] + _weight_specs(wl)
                + [_resident(mix.shape), _resident(spb.shape)]
                + [_resident(wl[name].shape) for name in ("woa", "wob", "woc", "wo")])
    out_shape = (jax.ShapeDtypeStruct((n, d), F32),
                 jax.ShapeDtypeStruct((batch, WINDOW, KV_WIDTH), F32),
                 jax.ShapeDtypeStruct((batch, WINDOW, KV_WIDTH), F32),
                 jax.ShapeDtypeStruct((batch, SUBLANES, CONV_WIDTH), F32))
    out_specs = (pl.BlockSpec((rows, d), lambda i: (i, 0)),
                 pl.BlockSpec((1, WINDOW, KV_WIDTH), lambda i: (i // nblk, 0, 0)),
                 pl.BlockSpec((1, WINDOW, KV_WIDTH), lambda i: (i // nblk, 0, 0)),
                 pl.BlockSpec((1, SUBLANES, CONV_WIDTH), lambda i: (i // nblk, 0, 0)))
    scratch = [pltpu.VMEM((8, 2 * WINDOW, 2 * WINDOW), F32)]
    scratch += [pltpu.VMEM((WINDOW + rows, KV_WIDTH), BF16) for _ in range(4)]
    scratch += [pltpu.VMEM((SUBLANES, CONV_WIDTH), F32)]
    return pl.pallas_call(
        functools.partial(_prompt_kernel, rows=rows, nblk=nblk),
        grid=(n // rows,),
        in_specs=in_specs, out_specs=out_specs, out_shape=out_shape, scratch_shapes=scratch,
        compiler_params=pltpu.CompilerParams(dimension_semantics=("arbitrary",),
                                             vmem_limit_bytes=VMEM_LIMIT_BYTES),
        name="prompt_layer",
    )(wl["sinks"], x, wl["norm_w"], wl["w_in"], wl["b_gate"], wl["qw"], wl["kw"], wl["conv_w"],
      wl["vnw"], mix, spb, wl["woa"], wl["wob"], wl["woc"], wl["wo"])


def _sample_layer(x, ck, cv, st, wl, mix, spb, dec_seq):
    n, d = x.shape
    rows = SAMPLE_ROWS
    nseq = rows // dec_seq
    w = ck.shape[1]
    smem = pl.BlockSpec(memory_space=pltpu.SMEM)
    row_blk = lambda width: pl.BlockSpec((rows, width), lambda i: (i, 0))
    cache_blk = pl.BlockSpec((nseq, w, KV_WIDTH), lambda i: (i, 0, 0))
    in_specs = ([smem, row_blk(d), cache_blk, cache_blk, row_blk(CONV_WIDTH)] + _weight_specs(wl)
                + [_resident(mix.shape), _resident(spb.shape)]
                + [_resident(wl[name].shape) for name in ("woa", "wob", "woc", "wo")])
    out_shape = (jax.ShapeDtypeStruct((n, d), F32),
                 jax.ShapeDtypeStruct(ck.shape, F32),
                 jax.ShapeDtypeStruct(cv.shape, F32),
                 jax.ShapeDtypeStruct((n, CONV_WIDTH), F32),
                 jax.ShapeDtypeStruct((n, MLP_WIDTH), F32))
    out_specs = (row_blk(d), cache_blk, cache_blk, row_blk(CONV_WIDTH), row_blk(MLP_WIDTH))
    scratch = [pltpu.VMEM((N_KV_HEADS, Q_GROUP * dec_seq, SAMPLE_KEYS), F32),
               pltpu.VMEM((nseq, SAMPLE_KEYS, KV_WIDTH), F32),
               pltpu.VMEM((nseq, SAMPLE_KEYS, KV_WIDTH), F32),
               pltpu.VMEM((rows, ATTN_WIDTH), F32),
               pltpu.VMEM((rows, ATTN_WIDTH), F32)]
    return pl.pallas_call(
        functools.partial(_sample_kernel, rows=rows, dec_seq=dec_seq),
        grid=(n // rows,),
        in_specs=in_specs, out_specs=out_specs, out_shape=out_shape, scratch_shapes=scratch,
        compiler_params=pltpu.CompilerParams(dimension_semantics=("arbitrary",),
                                             vmem_limit_bytes=VMEM_LIMIT_BYTES),
        name="sample_layer",
    )(wl["sinks"], x, ck, cv, st, wl["norm_w"], wl["w_in"], wl["b_gate"], wl["qw"], wl["kw"],
      wl["conv_w"], wl["vnw"], mix, spb, wl["woa"], wl["wob"], wl["woc"], wl["wo"])


def kernel(x_prompt, x_sample, cache_k, cache_v, state_conv, norm_w, w_in, b_gate, q_norm_w, k_norm_w,
           sinks, conv_w, v_norm_w, w_spatial, b_spatial, w_out_a, w_out_b, w_out_c, w_o):
    batch, seq, d = x_prompt.shape
    dec_batch, dec_seq, _ = x_sample.shape
    depth = w_in.shape[0]
    w_buf = cache_k.shape[2]
    assert d == D_MODEL and seq % PROMPT_ROWS == 0 and (dec_batch * dec_seq) % SAMPLE_ROWS == 0
    assert w_buf == WINDOW and dec_seq == SUBLANES and w_in.shape[2] == IN_COLS

    w_in_b = w_in.astype(BF16)
    woa_b, wob_b, woc_b, wo_b = (t.astype(BF16) for t in (w_out_a, w_out_b, w_out_c, w_o))
    qw = jnp.tile(q_norm_w, (1, LANES // HEAD_DIM))
    kw = jnp.tile(k_norm_w, (1, LANES // HEAD_DIM))
    gw = MLP_WIDTH // N_SPATIAL_GROUPS
    spb_p = jnp.repeat(jnp.swapaxes(b_spatial, 1, 2), gw, axis=2)
    reps = CHUNK // dec_seq
    mix_s = jnp.tile(w_spatial[:, :, :dec_seq, :dec_seq], (1, 1, reps, reps))
    spb_s = jnp.tile(spb_p[:, :dec_seq, :], (1, reps, 1))

    st_rows = jnp.pad(state_conv, ((0, 0), (0, 0), (0, dec_seq - (CONV_K - 1)), (0, 0)))
    st_rows = st_rows.reshape(depth, dec_batch * dec_seq, CONV_WIDTH)
    ck = cache_k.reshape(depth, dec_batch, w_buf, KV_WIDTH)
    cv = cache_v.reshape(depth, dec_batch, w_buf, KV_WIDTH)

    yp = x_prompt.reshape(batch * seq, d)
    ys = x_sample.reshape(dec_batch * dec_seq, d)
    pk, pv, pc, sk, sv, sc, scv = [], [], [], [], [], [], []
    for l in range(depth):
        wl = {"sinks": sinks[l], "norm_w": norm_w[l][None], "w_in": w_in_b[l], "b_gate": b_gate[l][None],
              "qw": qw[l][None], "kw": kw[l][None], "conv_w": conv_w[l], "vnw": v_norm_w[l][None],
              "woa": woa_b[l], "wob": wob_b[l], "woc": woc_b[l], "wo": wo_b[l]}
        yp, k_l, v_l, c_l = _prompt_layer(yp, wl, w_spatial[l], spb_p[l], batch)
        pk.append(k_l.reshape(batch, WINDOW, N_KV_HEADS, HEAD_DIM))
        pv.append(v_l.reshape(batch, WINDOW, N_KV_HEADS, HEAD_DIM))
        pc.append(c_l[:, SUBLANES - (CONV_K - 1):, :])
        ys, sk_l, sv_l, ci_l, scv_l = _sample_layer(ys, ck[l], cv[l], st_rows[l], wl, mix_s[l], spb_s[l],
                                                    dec_seq)
        sk.append(sk_l.reshape(dec_batch, w_buf, N_KV_HEADS, HEAD_DIM))
        sv.append(sv_l.reshape(dec_batch, w_buf, N_KV_HEADS, HEAD_DIM))
        sc.append(ci_l.reshape(dec_batch, dec_seq, CONV_WIDTH)[:, dec_seq - (CONV_K - 1):, :])
        scv.append(scv_l.reshape(dec_batch, dec_seq, MLP_WIDTH))

    return (yp.reshape(batch, seq, d), ys.reshape(dec_batch, dec_seq, d), jnp.stack(pk), jnp.stack(pv),
            jnp.stack(pc), jnp.stack(sk), jnp.stack(sv), jnp.stack(sc), jnp.stack(scv))
```

```python
import functools

import jax
import jax.numpy as jnp
from jax import lax
from jax.experimental import pallas as pl
from jax.experimental.pallas import tpu as pltpu

F32 = jnp.float32
BF16 = jnp.bfloat16

D_MODEL = 1024
N_HEADS = 8
N_KV_HEADS = 2
HEAD_DIM = 64
Q_GROUP = N_HEADS // N_KV_HEADS
ATTN_WIDTH = N_HEADS * HEAD_DIM
KV_WIDTH = N_KV_HEADS * HEAD_DIM
WINDOW = 128
CONV_WIDTH = 512
CONV_K = 3
CHUNK = 128
MLP_WIDTH = 512
N_SPATIAL_GROUPS = 4
EPS = 1e-6
NEG_INF = -1e30

COL_A = 0
COL_B = COL_A + 2 * ATTN_WIDTH + 2 * KV_WIDTH
COL_C = COL_B + 4 * CONV_WIDTH
COL_G = COL_C + 3 * MLP_WIDTH
IN_COLS = COL_G + 3 * D_MODEL

LANES = 128
SUBLANES = 8
PROMPT_ROWS = 256
SAMPLE_ROWS = 128
VMEM_LIMIT_BYTES = 56 * 1024 * 1024


def _dot(a, b):
    return jnp.dot(a, b, preferred_element_type=F32)


def _dot_nt(a, b):
    return lax.dot_general(a, b, (((1,), (1,)), ((), ())), preferred_element_type=F32)


def _rms(x, w):
    ms = jnp.mean(x * x, axis=-1, keepdims=True)
    return (x * lax.rsqrt(ms + EPS)) * w


def _lo_lanes(shape):
    return lax.broadcasted_iota(jnp.int32, shape, len(shape) - 1) < HEAD_DIM


def _pair_rms(x, w):
    lo = _lo_lanes(x.shape)
    sq = x * x
    s_lo = jnp.sum(jnp.where(lo, sq, 0.0), axis=-1, keepdims=True)
    s_hi = jnp.sum(jnp.where(lo, 0.0, sq), axis=-1, keepdims=True)
    ms = jnp.where(lo, s_lo, s_hi) * (1.0 / HEAD_DIM)
    return (x * lax.rsqrt(ms + EPS)) * w


def _norm_heads(x, w):
    groups = [_pair_rms(x[:, g * LANES:(g + 1) * LANES], w) for g in range(x.shape[1] // LANES)]
    return groups[0] if len(groups) == 1 else jnp.concatenate(groups, axis=1)


def _slope(head):
    return 2.0 ** (-(head + 1))


def _tail(x, xn, y_a, z_a, conv_shift, mix_mask, w_in_ref, b_gate_ref, conv_w_ref, vnw_ref,
          mix_ref, spb_ref, woa_ref, wob_ref, woc_ref, wo_ref):
    rows = x.shape[0]

    def gate(idx):
        g = _dot(xn, w_in_ref[:, COL_G + idx * D_MODEL:COL_G + (idx + 1) * D_MODEL])
        return jax.nn.sigmoid(g + b_gate_ref[:, idx * D_MODEL:(idx + 1) * D_MODEL])

    a = _dot((jax.nn.silu(z_a) * y_a).astype(BF16), woa_ref[...])
    m = gate(0) * a

    hb = _dot(xn, w_in_ref[:, COL_B:COL_C])
    gate_b = hb[:, 0:CONV_WIDTH]
    gate_c = hb[:, CONV_WIDTH:2 * CONV_WIDTH]
    h_b = hb[:, 2 * CONV_WIDTH:3 * CONV_WIDTH]
    z_b = hb[:, 3 * CONV_WIDTH:4 * CONV_WIDTH]
    conv_in = gate_c * h_b
    xm1, xm2 = conv_shift(conv_in)
    cw = conv_w_ref[...]
    conv_out = cw[0:1] * xm2 + cw[1:2] * xm1 + cw[2:3] * conv_in
    y_b = gate_b * conv_out
    b = _dot((jax.nn.silu(z_b) * y_b).astype(BF16), wob_ref[...])
    m = m + gate(1) * b

    hc = _dot(xn, w_in_ref[:, COL_C:COL_G])
    u = hc[:, 0:MLP_WIDTH]
    v_c = _rms(hc[:, MLP_WIDTH:2 * MLP_WIDTH], vnw_ref[...])
    z_c = hc[:, 2 * MLP_WIDTH:3 * MLP_WIDTH]
    vb = v_c.astype(BF16)
    spb = spb_ref[...]
    gw = MLP_WIDTH // N_SPATIAL_GROUPS
    mixes = [jnp.where(mix_mask, mix_ref[g], 0.0).astype(BF16) for g in range(N_SPATIAL_GROUPS)]
    sp_chunks = []
    for c in range(rows // CHUNK):
        r0 = c * CHUNK
        parts = [_dot(mixes[g], vb[r0:r0 + CHUNK, g * gw:(g + 1) * gw])
                 for g in range(N_SPATIAL_GROUPS)]
        sp_chunks.append(jnp.concatenate(parts, axis=1) + spb)
    sp = sp_chunks[0] if len(sp_chunks) == 1 else jnp.concatenate(sp_chunks, axis=0)
    y_c = u * sp
    c_out = _dot((jax.nn.silu(z_c) * y_c).astype(BF16), woc_ref[...])
    m = m + gate(2) * c_out

    y = x + _dot(m.astype(BF16), wo_ref[...])
    return y, conv_in, v_c


def _prompt_kernel(sinks_ref, x_ref, norm_w_ref, w_in_ref, b_gate_ref, qw_ref, kw_ref, conv_w_ref,
                   vnw_ref, mix_ref, spb_ref, woa_ref, wob_ref, woc_ref, wo_ref,
                   y_ref, pk_ref, pv_ref, pc_ref,
                   bias_scr, ka_scr, kb_scr, va_scr, vb_scr, carry_scr, *, rows, nblk, layer):
    i = pl.program_id(0)
    first = lax.rem(i, nblk) == 0
    nsub = rows // WINDOW
    stack = 2 * WINDOW

    @pl.when(i == 0)
    def _():
        r = lax.broadcasted_iota(jnp.int32, (stack, 2 * WINDOW), 0)
        col = lax.broadcasted_iota(jnp.int32, (stack, 2 * WINDOW), 1)
        top = r < WINDOW
        dist = jnp.where(top, r, r - WINDOW) + WINDOW - col
        band = (dist >= 0) & (dist < WINDOW)
        distf = dist.astype(F32)
        for flag in range(2):
            valid = band & (col >= WINDOW) if flag else band
            for h in range(N_KV_HEADS):
                for half in range(2):
                    slope = jnp.where(top, _slope(h * Q_GROUP + half), _slope(h * Q_GROUP + 2 + half))
                    bias_scr[flag * 4 + h * 2 + half] = jnp.where(valid, -(slope * distf), NEG_INF)

    @pl.when(first)
    def _():
        zeros = jnp.zeros((WINDOW, KV_WIDTH), BF16)
        ka_scr[0:WINDOW, :] = zeros
        kb_scr[0:WINDOW, :] = zeros
        va_scr[0:WINDOW, :] = zeros
        vb_scr[0:WINDOW, :] = zeros
        carry_scr[...] = jnp.zeros(carry_scr.shape, F32)

    x = x_ref[...]
    xn = _rms(x, norm_w_ref[...]).astype(BF16)
    ha = _dot(xn, w_in_ref[:, COL_A:COL_B])
    q = _norm_heads(ha[:, 0:ATTN_WIDTH], qw_ref[...]) * (HEAD_DIM ** -0.5)
    k = _pair_rms(ha[:, ATTN_WIDTH:ATTN_WIDTH + KV_WIDTH], kw_ref[...])
    v = ha[:, ATTN_WIDTH + KV_WIDTH:ATTN_WIDTH + 2 * KV_WIDTH]
    z_a = ha[:, ATTN_WIDTH + 2 * KV_WIDTH:COL_B]

    pk_ref[0] = k[rows - WINDOW:rows, :]
    pv_ref[0] = v[rows - WINDOW:rows, :]
    ka_scr[WINDOW:WINDOW + rows, :] = k.astype(BF16)
    kb_scr[WINDOW:WINDOW + rows, :] = pltpu.roll(k, HEAD_DIM, axis=1).astype(BF16)
    va_scr[WINDOW:WINDOW + rows, :] = v.astype(BF16)
    vb_scr[WINDOW:WINDOW + rows, :] = pltpu.roll(v, HEAD_DIM, axis=1).astype(BF16)

    lo = _lo_lanes((WINDOW, LANES))
    srow = lax.broadcasted_iota(jnp.int32, (stack, 1), 0) < WINDOW
    flag = first.astype(jnp.int32)
    y_rows = []
    for c in range(nsub):
        r0 = c * WINDOW
        keys = slice(r0, r0 + 2 * WINDOW)
        k_nat, k_swp = ka_scr[keys, :], kb_scr[keys, :]
        v_nat, v_swp = va_scr[keys, :], vb_scr[keys, :]
        pairs = []
        for h in range(N_KV_HEADS):
            p0 = q[r0:r0 + WINDOW, (2 * h) * LANES:(2 * h + 1) * LANES]
            p1 = q[r0:r0 + WINDOW, (2 * h + 1) * LANES:(2 * h + 2) * LANES]
            outs = []
            for half in range(2):
                keep = lo if half == 0 else jnp.logical_not(lo)
                qs = jnp.concatenate([jnp.where(keep, p0, 0.0), jnp.where(keep, p1, 0.0)],
                                     axis=0).astype(BF16)
                aligned = (h == half)
                kk = k_nat if aligned else k_swp
                vv = v_nat if aligned else v_swp
                tbl = h * 2 + half
                bias = bias_scr[flag * 4 + tbl] if c == 0 else bias_scr[tbl]
                s = _dot_nt(qs, kk) + bias
                sink = jnp.where(srow, sinks_ref[layer, h * Q_GROUP + half],
                                 sinks_ref[layer, h * Q_GROUP + 2 + half])
                mx = jnp.maximum(jnp.max(s, axis=-1, keepdims=True), sink)
                e = jnp.exp(s - mx)
                den = jnp.sum(e, axis=-1, keepdims=True) + jnp.exp(sink - mx)
                outs.append(_dot(e.astype(BF16), vv) * (1.0 / den))
            pairs.append(jnp.where(lo, outs[0][0:WINDOW], outs[1][0:WINDOW]))
            pairs.append(jnp.where(lo, outs[0][WINDOW:stack], outs[1][WINDOW:stack]))
        y_rows.append(jnp.concatenate(pairs, axis=1))
    y_a = y_rows[0] if nsub == 1 else jnp.concatenate(y_rows, axis=0)

    ka_scr[0:WINDOW, :] = ka_scr[rows:rows + WINDOW, :]
    kb_scr[0:WINDOW, :] = kb_scr[rows:rows + WINDOW, :]
    va_scr[0:WINDOW, :] = va_scr[rows:rows + WINDOW, :]
    vb_scr[0:WINDOW, :] = vb_scr[rows:rows + WINDOW, :]

    carry = carry_scr[...]
    prev1 = carry[SUBLANES - 1:SUBLANES, :]
    prev2 = carry[SUBLANES - 2:SUBLANES - 1, :]

    def conv_shift(ci):
        rid = lax.broadcasted_iota(jnp.int32, ci.shape, 0)
        xm1 = jnp.where(rid == 0, prev1, pltpu.roll(ci, 1, axis=0))
        xm2 = jnp.where(rid == 0, prev2, jnp.where(rid == 1, prev1, pltpu.roll(ci, 2, axis=0)))
        return xm1, xm2

    mr = lax.broadcasted_iota(jnp.int32, (CHUNK, CHUNK), 0)
    mc = lax.broadcasted_iota(jnp.int32, (CHUNK, CHUNK), 1)
    y, conv_in, _ = _tail(x, xn, y_a, z_a, conv_shift, mc <= mr, w_in_ref, b_gate_ref, conv_w_ref,
                          vnw_ref, mix_ref, spb_ref, woa_ref, wob_ref, woc_ref, wo_ref)
    y_ref[...] = y
    last = conv_in[rows - SUBLANES:rows, :]
    carry_scr[...] = last
    pc_ref[0] = last


def _sample_kernel(sinks_ref, x_ref, ck_ref, cv_ref, st_ref, norm_w_ref, w_in_ref, b_gate_ref, qw_ref,
                   kw_ref, conv_w_ref, vnw_ref, mix_ref, spb_ref, woa_ref, wob_ref, woc_ref, wo_ref,
                   y_ref, sk_ref, sv_ref, ci_ref, scv_ref,
                   x_scr, bold_scr, bnew_scr, q_scr, en_scr, oo_scr, inv_scr, *, dec_seq):
    layer = pl.program_id(0)
    j = pl.program_id(1)
    rows = SAMPLE_ROWS
    nseq = rows // dec_seq
    stack = N_HEADS * dec_seq
    w = WINDOW

    @pl.when((layer == 0) & (j == 0))
    def _():
        r = lax.broadcasted_iota(jnp.int32, (stack, LANES), 0)
        col = lax.broadcasted_iota(jnp.int32, (stack, LANES), 1)
        tok = lax.rem(r, dec_seq)
        head = r // dec_seq
        slope = jnp.zeros((stack, LANES), F32)
        for n in range(N_HEADS):
            slope = jnp.where(head == n, _slope(n), slope)
        dist = tok + w - col
        bold_scr[...] = jnp.where(dist < WINDOW, -(slope * dist.astype(F32)), NEG_INF)
        kseq = col // dec_seq
        dist = tok - lax.rem(col, dec_seq)
        pen = -(slope * dist.astype(F32))
        for b in range(nseq):
            bnew_scr[b] = jnp.where((kseq == b) & (dist >= 0), pen, NEG_INF)

    r0 = pl.multiple_of(j * rows, rows)

    @pl.when(layer == 0)
    def _():
        x_scr[pl.ds(r0, rows), :] = x_ref[...]

    x = x_scr[pl.ds(r0, rows), :]
    xn = _rms(x, norm_w_ref[...]).astype(BF16)
    ha = _dot(xn, w_in_ref[:, COL_A:COL_B])
    q = _norm_heads(ha[:, 0:ATTN_WIDTH], qw_ref[...]) * (HEAD_DIM ** -0.5)
    k = _pair_rms(ha[:, ATTN_WIDTH:ATTN_WIDTH + KV_WIDTH], kw_ref[...])
    v = ha[:, ATTN_WIDTH + KV_WIDTH:ATTN_WIDTH + 2 * KV_WIDTH]
    z_a = ha[:, ATTN_WIDTH + 2 * KV_WIDTH:COL_B]

    lo = _lo_lanes((rows, LANES))
    for h in range(N_KV_HEADS):
        keep = lo if h == 0 else jnp.logical_not(lo)
        for g in range(Q_GROUP):
            pair = q[:, (h * 2 + g // 2) * LANES:(h * 2 + g // 2 + 1) * LANES]
            if g % 2 != h:
                pair = pltpu.roll(pair, HEAD_DIM, axis=1)
            piece = jnp.where(keep, pair, 0.0).reshape(nseq, dec_seq, LANES)
            s0 = (h * Q_GROUP + g) * dec_seq
            q_scr[:, s0:s0 + dec_seq, :] = piece

    kt_new = jnp.transpose(k)
    vt_new = jnp.transpose(v)
    qall = q_scr[...].reshape(nseq * stack, LANES).astype(BF16)
    s_new_all = _dot(qall, kt_new.astype(BF16)).reshape(nseq, stack, LANES)

    srow = lax.broadcasted_iota(jnp.int32, (stack, 1), 0) // dec_seq
    sink = jnp.zeros((stack, 1), F32)
    for n in range(N_HEADS):
        sink = jnp.where(srow == n, sinks_ref[layer, n], sink)
    bias_old = bold_scr[...]
    lane = lax.broadcasted_iota(jnp.int32, (KV_WIDTH, w), 1)
    for b in range(nseq):
        kt = ck_ref[b].reshape(KV_WIDTH, w)
        vt = cv_ref[b].reshape(KV_WIDTH, w)
        s_old = _dot(q_scr[b].astype(BF16), kt.astype(BF16)) + bias_old
        s_new = s_new_all[b] + bnew_scr[b]
        mx = jnp.maximum(jnp.maximum(jnp.max(s_old, axis=-1, keepdims=True),
                                     jnp.max(s_new, axis=-1, keepdims=True)), sink)
        e_old = jnp.exp(s_old - mx)
        e_new = jnp.exp(s_new - mx)
        den = (jnp.sum(e_old, axis=-1, keepdims=True) + jnp.sum(e_new, axis=-1, keepdims=True)
               + jnp.exp(sink - mx))
        en_scr[b] = e_new
        oo_scr[b] = _dot_nt(e_old.astype(BF16), vt.astype(BF16))
        inv_scr[b] = jnp.broadcast_to(1.0 / den, (stack, LANES))
        shift = w - dec_seq - b * dec_seq
        newk = pltpu.roll(kt_new, shift, axis=1) if shift else kt_new
        newv = pltpu.roll(vt_new, shift, axis=1) if shift else vt_new
        keep_old = lane < w - dec_seq
        sk_ref[b] = jnp.where(keep_old, pltpu.roll(kt, w - dec_seq, axis=1), newk).reshape(
            N_KV_HEADS, HEAD_DIM, w)
        sv_ref[b] = jnp.where(keep_old, pltpu.roll(vt, w - dec_seq, axis=1), newv).reshape(
            N_KV_HEADS, HEAD_DIM, w)

    o_new = _dot(en_scr[...].reshape(nseq * stack, LANES).astype(BF16), v.astype(BF16))
    o = (oo_scr[...] + o_new.reshape(nseq, stack, LANES)) * inv_scr[...]
    pairs = []
    for h in range(N_KV_HEADS):
        heads = []
        for g in range(Q_GROUP):
            s0 = (h * Q_GROUP + g) * dec_seq
            og = o[:, s0:s0 + dec_seq, :].reshape(rows, LANES)
            if g % 2 != h:
                og = pltpu.roll(og, HEAD_DIM, axis=1)
            heads.append(og)
        pairs.append(jnp.where(lo, heads[0], heads[1]))
        pairs.append(jnp.where(lo, heads[2], heads[3]))
    y_a = jnp.concatenate(pairs, axis=1)

    st = st_ref[...]

    def conv_shift(ci):
        t = lax.rem(lax.broadcasted_iota(jnp.int32, ci.shape, 0), dec_seq)
        xm1 = jnp.where(t == 0, pltpu.roll(st, rows - 1, axis=0), pltpu.roll(ci, 1, axis=0))
        xm2 = jnp.where(t < CONV_K - 1, st, pltpu.roll(ci, 2, axis=0))
        return xm1, xm2

    mr = lax.broadcasted_iota(jnp.int32, (CHUNK, CHUNK), 0)
    mc = lax.broadcasted_iota(jnp.int32, (CHUNK, CHUNK), 1)
    mix_mask = (mr // dec_seq == mc // dec_seq) & (mc <= mr)
    y, conv_in, v_c = _tail(x, xn, y_a, z_a, conv_shift, mix_mask, w_in_ref, b_gate_ref, conv_w_ref,
                            vnw_ref, mix_ref, spb_ref, woa_ref, wob_ref, woc_ref, wo_ref)
    x_scr[pl.ds(r0, rows), :] = y

    @pl.when(layer == pl.num_programs(0) - 1)
    def _():
        y_ref[...] = y
    ci_ref[...] = conv_in
    scv_ref[...] = v_c


WEIGHT_NAMES = ("norm_w", "w_in", "b_gate", "qw", "kw", "conv_w", "vnw", "mix", "spb",
                "woa", "wob", "woc", "wo")


def _layer_spec(arr, layer_of):
    nd = arr.ndim - 1
    return pl.BlockSpec((None,) + arr.shape[1:], lambda *g, _nd=nd: (layer_of(*g),) + (0,) * _nd,
                        pipeline_mode=pl.Buffered(1))


def _prompt_layer(x, wts, layer, batch):
    n, d = x.shape
    rows = PROMPT_ROWS
    nblk = n // batch // rows
    smem = pl.BlockSpec(memory_space=pltpu.SMEM)
    in_specs = ([smem, pl.BlockSpec((rows, d), lambda i: (i, 0))]
                + [_layer_spec(wts[name], lambda i: layer) for name in WEIGHT_NAMES])
    out_shape = (jax.ShapeDtypeStruct((n, d), F32),
                 jax.ShapeDtypeStruct((batch, WINDOW, KV_WIDTH), F32),
                 jax.ShapeDtypeStruct((batch, WINDOW, KV_WIDTH), F32),
                 jax.ShapeDtypeStruct((batch, SUBLANES, CONV_WIDTH), F32))
    out_specs = (pl.BlockSpec((rows, d), lambda i: (i, 0)),
                 pl.BlockSpec((1, WINDOW, KV_WIDTH), lambda i: (i // nblk, 0, 0)),
                 pl.BlockSpec((1, WINDOW, KV_WIDTH), lambda i: (i // nblk, 0, 0)),
                 pl.BlockSpec((1, SUBLANES, CONV_WIDTH), lambda i: (i // nblk, 0, 0)))
    n_tables = 2 * N_KV_HEADS * 2
    scratch = [pltpu.VMEM((n_tables, 2 * WINDOW, 2 * WINDOW), F32)]
    scratch += [pltpu.VMEM((WINDOW + rows, KV_WIDTH), BF16) for _ in range(4)]
    scratch += [pltpu.VMEM((SUBLANES, CONV_WIDTH), F32)]
    return pl.pallas_call(
        functools.partial(_prompt_kernel, rows=rows, nblk=nblk, layer=layer),
        grid=(n // rows,),
        in_specs=in_specs, out_specs=out_specs, out_shape=out_shape, scratch_shapes=scratch,
        compiler_params=pltpu.CompilerParams(dimension_semantics=("arbitrary",),
                                             vmem_limit_bytes=VMEM_LIMIT_BYTES),
        name="prompt_layer",
    )(wts["sinks"], x, *[wts[name] for name in WEIGHT_NAMES])


def _sample_layers(x, ck, cv, st, wts, dec_seq):
    n, d = x.shape
    depth = ck.shape[0]
    rows = SAMPLE_ROWS
    nseq = rows // dec_seq
    stack = N_HEADS * dec_seq
    w = ck.shape[-1]
    smem = pl.BlockSpec(memory_space=pltpu.SMEM)
    cache_blk = pl.BlockSpec((None, nseq, N_KV_HEADS, HEAD_DIM, w), lambda l, j: (l, j, 0, 0, 0))
    rows_blk = lambda width: pl.BlockSpec((None, rows, width), lambda l, j: (l, j, 0))
    nblk = n // rows
    x_blk = pl.BlockSpec((rows, d), lambda l, j: (jnp.where(l == 0, j, nblk - 1), 0))
    y_blk = pl.BlockSpec((rows, d), lambda l, j: (jnp.where(l == depth - 1, j, 0), 0))
    in_specs = ([smem, x_blk, cache_blk, cache_blk, rows_blk(CONV_WIDTH)]
                + [_layer_spec(wts[name], lambda l, j: l) for name in WEIGHT_NAMES])
    out_shape = (jax.ShapeDtypeStruct((n, d), F32),
                 jax.ShapeDtypeStruct(ck.shape, F32),
                 jax.ShapeDtypeStruct(cv.shape, F32),
                 jax.ShapeDtypeStruct((depth, n, CONV_WIDTH), F32),
                 jax.ShapeDtypeStruct((depth, n, MLP_WIDTH), F32))
    out_specs = (y_blk, cache_blk, cache_blk, rows_blk(CONV_WIDTH), rows_blk(MLP_WIDTH))
    scratch = [pltpu.VMEM((n, d), F32),
               pltpu.VMEM((stack, LANES), F32),
               pltpu.VMEM((nseq, stack, LANES), F32),
               pltpu.VMEM((nseq, stack, LANES), F32),
               pltpu.VMEM((nseq, stack, LANES), F32),
               pltpu.VMEM((nseq, stack, LANES), F32),
               pltpu.VMEM((nseq, stack, LANES), F32)]
    return pl.pallas_call(
        functools.partial(_sample_kernel, dec_seq=dec_seq),
        grid=(depth, n // rows),
        in_specs=in_specs, out_specs=out_specs, out_shape=out_shape, scratch_shapes=scratch,
        compiler_params=pltpu.CompilerParams(dimension_semantics=("arbitrary", "arbitrary"),
                                             vmem_limit_bytes=VMEM_LIMIT_BYTES),
        name="sample_layers",
    )(wts["sinks"], x, ck, cv, st, *[wts[name] for name in WEIGHT_NAMES])


def kernel(x_prompt, x_sample, cache_k, cache_v, state_conv, norm_w, w_in, b_gate, q_norm_w, k_norm_w,
           sinks, conv_w, v_norm_w, w_spatial, b_spatial, w_out_a, w_out_b, w_out_c, w_o):
    batch, seq, d = x_prompt.shape
    dec_batch, dec_seq, _ = x_sample.shape
    depth = w_in.shape[0]
    w_buf = cache_k.shape[2]
    assert d == D_MODEL and seq % PROMPT_ROWS == 0 and (dec_batch * dec_seq) % SAMPLE_ROWS == 0
    assert w_buf == WINDOW and dec_seq == SUBLANES and w_in.shape[2] == IN_COLS

    gw = MLP_WIDTH // N_SPATIAL_GROUPS
    reps = CHUNK // dec_seq
    spb_p = jnp.repeat(jnp.swapaxes(b_spatial, 1, 2), gw, axis=2)
    common = {
        "sinks": sinks,
        "norm_w": norm_w[:, None, :], "w_in": w_in.astype(BF16), "b_gate": b_gate[:, None, :],
        "qw": jnp.tile(q_norm_w, (1, LANES // HEAD_DIM))[:, None, :],
        "kw": jnp.tile(k_norm_w, (1, LANES // HEAD_DIM))[:, None, :],
        "conv_w": conv_w, "vnw": v_norm_w[:, None, :],
        "woa": w_out_a.astype(BF16), "wob": w_out_b.astype(BF16), "woc": w_out_c.astype(BF16),
        "wo": w_o.astype(BF16),
    }
    wts_p = dict(common, mix=w_spatial, spb=spb_p)
    wts_s = dict(common,
                 mix=jnp.tile(w_spatial[:, :, :dec_seq, :dec_seq], (1, 1, reps, reps)),
                 spb=jnp.tile(spb_p[:, :dec_seq, :], (1, reps, 1)))

    st_rows = jnp.pad(state_conv, ((0, 0), (0, 0), (0, dec_seq - (CONV_K - 1)), (0, 0)))
    st_rows = st_rows.reshape(depth, dec_batch * dec_seq, CONV_WIDTH)
    ck = jnp.transpose(cache_k, (0, 1, 3, 4, 2))
    cv = jnp.transpose(cache_v, (0, 1, 3, 4, 2))

    ys, sk, sv, ci, scv = _sample_layers(x_sample.reshape(dec_batch * dec_seq, d), ck, cv, st_rows,
                                         wts_s, dec_seq)
    sk = jnp.transpose(sk, (0, 1, 4, 2, 3))
    sv = jnp.transpose(sv, (0, 1, 4, 2, 3))
    sc = ci.reshape(depth, dec_batch, dec_seq, CONV_WIDTH)[:, :, dec_seq - (CONV_K - 1):, :]
    scv = scv.reshape(depth, dec_batch, dec_seq, MLP_WIDTH)

    yp = x_prompt.reshape(batch * seq, d)
    pk, pv, pc = [], [], []
    for l in range(depth):
        yp, k_l, v_l, c_l = _prompt_layer(yp, wts_p, l, batch)
        pk.append(k_l.reshape(batch, WINDOW, N_KV_HEADS, HEAD_DIM))
        pv.append(v_l.reshape(batch, WINDOW, N_KV_HEADS, HEAD_DIM))
        pc.append(c_l[:, SUBLANES - (CONV_K - 1):, :])

    return (yp.reshape(batch, seq, d), ys.reshape(dec_batch, dec_seq, d), jnp.stack(pk), jnp.stack(pv),
            jnp.stack(pc), sk, sv, sc, scv)
```

```python
import functools

import jax
import jax.numpy as jnp
from jax import lax
from jax.experimental import pallas as pl
from jax.experimental.pallas import tpu as pltpu

F32 = jnp.float32
BF16 = jnp.bfloat16

D_MODEL = 1024
N_HEADS = 8
N_KV_HEADS = 2
HEAD_DIM = 64
Q_GROUP = N_HEADS // N_KV_HEADS
ATTN_WIDTH = N_HEADS * HEAD_DIM
KV_WIDTH = N_KV_HEADS * HEAD_DIM
WINDOW = 128
CONV_WIDTH = 512
CONV_K = 3
CHUNK = 128
MLP_WIDTH = 512
N_SPATIAL_GROUPS = 4
EPS = 1e-6
NEG_INF = -1e30

COL_A = 0
COL_B = COL_A + 2 * ATTN_WIDTH + 2 * KV_WIDTH
COL_C = COL_B + 4 * CONV_WIDTH
COL_G = COL_C + 3 * MLP_WIDTH
IN_COLS = COL_G + 3 * D_MODEL

LANES = 128
SUBLANES = 8
PROMPT_ROWS = 256
SAMPLE_ROWS = 128
VMEM_LIMIT_BYTES = 56 * 1024 * 1024


def _dot(a, b):
    return jnp.dot(a, b, preferred_element_type=F32)


def _dot_nt(a, b):
    return lax.dot_general(a, b, (((1,), (1,)), ((), ())), preferred_element_type=F32)


def _rms(x, w):
    ms = jnp.mean(x * x, axis=-1, keepdims=True)
    return (x * lax.rsqrt(ms + EPS)) * w


def _lo_lanes(shape):
    return lax.broadcasted_iota(jnp.int32, shape, len(shape) - 1) < HEAD_DIM


def _pair_rms(x, w):
    lo = _lo_lanes(x.shape)
    sq = x * x
    s_lo = jnp.sum(jnp.where(lo, sq, 0.0), axis=-1, keepdims=True)
    s_hi = jnp.sum(jnp.where(lo, 0.0, sq), axis=-1, keepdims=True)
    ms = jnp.where(lo, s_lo, s_hi) * (1.0 / HEAD_DIM)
    return (x * lax.rsqrt(ms + EPS)) * w


def _norm_heads(x, w):
    groups = [_pair_rms(x[:, g * LANES:(g + 1) * LANES], w) for g in range(x.shape[1] // LANES)]
    return groups[0] if len(groups) == 1 else jnp.concatenate(groups, axis=1)


def _slope(head):
    return 2.0 ** (-(head + 1))


def _tail(x, hb, hc, gates, y_a, z_a, conv_shift, mix_mask, conv_w_ref, vnw_ref,
          mix_ref, spb_ref, woa_ref, wob_ref, woc_ref, wo_ref):
    rows = x.shape[0]

    a = _dot((jax.nn.silu(z_a) * y_a).astype(BF16), woa_ref[...])
    m = gates[0] * a

    gate_b = hb[:, 0:CONV_WIDTH]
    gate_c = hb[:, CONV_WIDTH:2 * CONV_WIDTH]
    h_b = hb[:, 2 * CONV_WIDTH:3 * CONV_WIDTH]
    z_b = hb[:, 3 * CONV_WIDTH:4 * CONV_WIDTH]
    conv_in = gate_c * h_b
    xm1, xm2 = conv_shift(conv_in)
    cw = conv_w_ref[...]
    conv_out = cw[0:1] * xm2 + cw[1:2] * xm1 + cw[2:3] * conv_in
    y_b = gate_b * conv_out
    b = _dot((jax.nn.silu(z_b) * y_b).astype(BF16), wob_ref[...])
    m = m + gates[1] * b

    u = hc[:, 0:MLP_WIDTH]
    v_c = _rms(hc[:, MLP_WIDTH:2 * MLP_WIDTH], vnw_ref[...])
    z_c = hc[:, 2 * MLP_WIDTH:3 * MLP_WIDTH]
    vb = v_c.astype(BF16)
    spb = spb_ref[...]
    gw = MLP_WIDTH // N_SPATIAL_GROUPS
    mixes = [jnp.where(mix_mask, mix_ref[g], 0.0).astype(BF16) for g in range(N_SPATIAL_GROUPS)]
    sp_chunks = []
    for c in range(rows // CHUNK):
        r0 = c * CHUNK
        parts = [_dot(mixes[g], vb[r0:r0 + CHUNK, g * gw:(g + 1) * gw])
                 for g in range(N_SPATIAL_GROUPS)]
        sp_chunks.append(jnp.concatenate(parts, axis=1) + spb)
    sp = sp_chunks[0] if len(sp_chunks) == 1 else jnp.concatenate(sp_chunks, axis=0)
    y_c = u * sp
    c_out = _dot((jax.nn.silu(z_c) * y_c).astype(BF16), woc_ref[...])
    m = m + gates[2] * c_out

    y = x + _dot(m.astype(BF16), wo_ref[...])
    return y, conv_in, v_c


def _prompt_kernel(sinks_ref, x_ref, norm_w_ref, w_in_ref, b_gate_ref, qw_ref, kw_ref, conv_w_ref,
                   vnw_ref, mix_ref, spb_ref, woa_ref, wob_ref, woc_ref, wo_ref,
                   y_ref, pk_ref, pv_ref, pc_ref,
                   bias_scr, ka_scr, kb_scr, va_scr, vb_scr, carry_scr, *, rows, nblk, layer):
    i = pl.program_id(0)
    first = lax.rem(i, nblk) == 0
    nsub = rows // WINDOW
    stack = 2 * WINDOW

    @pl.when(i == 0)
    def _():
        r = lax.broadcasted_iota(jnp.int32, (stack, 2 * WINDOW), 0)
        col = lax.broadcasted_iota(jnp.int32, (stack, 2 * WINDOW), 1)
        top = r < WINDOW
        dist = jnp.where(top, r, r - WINDOW) + WINDOW - col
        band = (dist >= 0) & (dist < WINDOW)
        distf = dist.astype(F32)
        for flag in range(2):
            valid = band & (col >= WINDOW) if flag else band
            for h in range(N_KV_HEADS):
                for half in range(2):
                    slope = jnp.where(top, _slope(h * Q_GROUP + half), _slope(h * Q_GROUP + 2 + half))
                    bias_scr[flag * 4 + h * 2 + half] = jnp.where(valid, -(slope * distf), NEG_INF)

    @pl.when(first)
    def _():
        zeros = jnp.zeros((WINDOW, KV_WIDTH), BF16)
        ka_scr[0:WINDOW, :] = zeros
        kb_scr[0:WINDOW, :] = zeros
        va_scr[0:WINDOW, :] = zeros
        vb_scr[0:WINDOW, :] = zeros
        carry_scr[...] = jnp.zeros(carry_scr.shape, F32)

    x = x_ref[...]
    xn = _rms(x, norm_w_ref[...]).astype(BF16)
    def proj(c0, c1):
        return _dot(xn, w_in_ref[:, c0:c1])

    def gate(idx):
        g = proj(COL_G + idx * D_MODEL, COL_G + (idx + 1) * D_MODEL)
        return jax.nn.sigmoid(g + b_gate_ref[:, idx * D_MODEL:(idx + 1) * D_MODEL])

    ha = proj(COL_A, COL_B)
    hb = proj(COL_B, COL_C)
    q = _norm_heads(ha[:, 0:ATTN_WIDTH], qw_ref[...]) * (HEAD_DIM ** -0.5)
    k = _pair_rms(ha[:, ATTN_WIDTH:ATTN_WIDTH + KV_WIDTH], kw_ref[...])
    v = ha[:, ATTN_WIDTH + KV_WIDTH:ATTN_WIDTH + 2 * KV_WIDTH]
    z_a = ha[:, ATTN_WIDTH + 2 * KV_WIDTH:COL_B]

    pk_ref[0] = k[rows - WINDOW:rows, :]
    pv_ref[0] = v[rows - WINDOW:rows, :]
    ka_scr[WINDOW:WINDOW + rows, :] = k.astype(BF16)
    kb_scr[WINDOW:WINDOW + rows, :] = pltpu.roll(k, HEAD_DIM, axis=1).astype(BF16)
    va_scr[WINDOW:WINDOW + rows, :] = v.astype(BF16)
    vb_scr[WINDOW:WINDOW + rows, :] = pltpu.roll(v, HEAD_DIM, axis=1).astype(BF16)

    lo = _lo_lanes((WINDOW, LANES))
    srow = lax.broadcasted_iota(jnp.int32, (stack, 1), 0) < WINDOW
    flag = first.astype(jnp.int32)
    fillers = [lambda: proj(COL_C, COL_G), lambda: gate(0), lambda: gate(1), lambda: gate(2)]
    filled = []
    y_rows = []
    for c in range(nsub):
        r0 = c * WINDOW
        keys = slice(r0, r0 + 2 * WINDOW)
        k_nat, k_swp = ka_scr[keys, :], kb_scr[keys, :]
        v_nat, v_swp = va_scr[keys, :], vb_scr[keys, :]
        scores = []
        for h in range(N_KV_HEADS):
            p0 = q[r0:r0 + WINDOW, (2 * h) * LANES:(2 * h + 1) * LANES]
            p1 = q[r0:r0 + WINDOW, (2 * h + 1) * LANES:(2 * h + 2) * LANES]
            for half in range(2):
                keep = lo if half == 0 else jnp.logical_not(lo)
                qs = jnp.concatenate([jnp.where(keep, p0, 0.0), jnp.where(keep, p1, 0.0)],
                                     axis=0).astype(BF16)
                kk = k_nat if h == half else k_swp
                tbl = h * 2 + half
                bias = bias_scr[flag * 4 + tbl] if c == 0 else bias_scr[tbl]
                scores.append(_dot_nt(qs, kk) + bias)
        if fillers:
            filled.append(fillers.pop(0)())
        pairs = []
        for h in range(N_KV_HEADS):
            outs = []
            for half in range(2):
                s = scores[h * 2 + half]
                vv = v_nat if h == half else v_swp
                sink = jnp.where(srow, sinks_ref[layer, h * Q_GROUP + half],
                                 sinks_ref[layer, h * Q_GROUP + 2 + half])
                mx = jnp.maximum(jnp.max(s, axis=-1, keepdims=True), sink)
                e = jnp.exp(s - mx)
                den = jnp.sum(e, axis=-1, keepdims=True) + jnp.exp(sink - mx)
                outs.append(_dot(e.astype(BF16), vv) * (1.0 / den))
            pairs.append(jnp.where(lo, outs[0][0:WINDOW], outs[1][0:WINDOW]))
            pairs.append(jnp.where(lo, outs[0][WINDOW:stack], outs[1][WINDOW:stack]))
        y_rows.append(jnp.concatenate(pairs, axis=1))
        if fillers:
            filled.append(fillers.pop(0)())
    while fillers:
        filled.append(fillers.pop(0)())
    hc, gates = filled[0], filled[1:]
    y_a = y_rows[0] if nsub == 1 else jnp.concatenate(y_rows, axis=0)

    ka_scr[0:WINDOW, :] = ka_scr[rows:rows + WINDOW, :]
    kb_scr[0:WINDOW, :] = kb_scr[rows:rows + WINDOW, :]
    va_scr[0:WINDOW, :] = va_scr[rows:rows + WINDOW, :]
    vb_scr[0:WINDOW, :] = vb_scr[rows:rows + WINDOW, :]

    carry = carry_scr[...]
    prev1 = carry[SUBLANES - 1:SUBLANES, :]
    prev2 = carry[SUBLANES - 2:SUBLANES - 1, :]

    def conv_shift(ci):
        rid = lax.broadcasted_iota(jnp.int32, ci.shape, 0)
        xm1 = jnp.where(rid == 0, prev1, pltpu.roll(ci, 1, axis=0))
        xm2 = jnp.where(rid == 0, prev2, jnp.where(rid == 1, prev1, pltpu.roll(ci, 2, axis=0)))
        return xm1, xm2

    mr = lax.broadcasted_iota(jnp.int32, (CHUNK, CHUNK), 0)
    mc = lax.broadcasted_iota(jnp.int32, (CHUNK, CHUNK), 1)
    y, conv_in, _ = _tail(x, hb, hc, gates, y_a, z_a, conv_shift, mc <= mr, conv_w_ref,
                          vnw_ref, mix_ref, spb_ref, woa_ref, wob_ref, woc_ref, wo_ref)
    y_ref[...] = y
    last = conv_in[rows - SUBLANES:rows, :]
    carry_scr[...] = last
    pc_ref[0] = last


def _sample_kernel(sinks_ref, x_ref, ck_ref, cv_ref, st_ref, norm_w_ref, w_in_ref, b_gate_ref, qw_ref,
                   kw_ref, conv_w_ref, vnw_ref, mix_ref, spb_ref, woa_ref, wob_ref, woc_ref, wo_ref,
                   y_ref, sk_ref, sv_ref, ci_ref, scv_ref,
                   x_scr, bold_scr, bnew_scr, q_scr, en_scr, oo_scr, inv_scr, *, dec_seq):
    layer = pl.program_id(0)
    j = pl.program_id(1)
    rows = SAMPLE_ROWS
    nseq = rows // dec_seq
    stack = N_HEADS * dec_seq
    w = WINDOW

    @pl.when((layer == 0) & (j == 0))
    def _():
        r = lax.broadcasted_iota(jnp.int32, (stack, LANES), 0)
        col = lax.broadcasted_iota(jnp.int32, (stack, LANES), 1)
        tok = lax.rem(r, dec_seq)
        head = r // dec_seq
        slope = jnp.zeros((stack, LANES), F32)
        for n in range(N_HEADS):
            slope = jnp.where(head == n, _slope(n), slope)
        dist = tok + w - col
        bold_scr[...] = jnp.where(dist < WINDOW, -(slope * dist.astype(F32)), NEG_INF)
        kseq = col // dec_seq
        dist = tok - lax.rem(col, dec_seq)
        pen = -(slope * dist.astype(F32))
        for b in range(nseq):
            bnew_scr[b] = jnp.where((kseq == b) & (dist >= 0), pen, NEG_INF)

    r0 = pl.multiple_of(j * rows, rows)

    @pl.when(layer == 0)
    def _():
        x_scr[pl.ds(r0, rows), :] = x_ref[...]

    x = x_scr[pl.ds(r0, rows), :]
    xn = _rms(x, norm_w_ref[...]).astype(BF16)
    def proj(c0, c1):
        return _dot(xn, w_in_ref[:, c0:c1])

    def gate(idx):
        g = proj(COL_G + idx * D_MODEL, COL_G + (idx + 1) * D_MODEL)
        return jax.nn.sigmoid(g + b_gate_ref[:, idx * D_MODEL:(idx + 1) * D_MODEL])

    ha = proj(COL_A, COL_B)
    hb = proj(COL_B, COL_C)
    q = _norm_heads(ha[:, 0:ATTN_WIDTH], qw_ref[...]) * (HEAD_DIM ** -0.5)
    k = _pair_rms(ha[:, ATTN_WIDTH:ATTN_WIDTH + KV_WIDTH], kw_ref[...])
    v = ha[:, ATTN_WIDTH + KV_WIDTH:ATTN_WIDTH + 2 * KV_WIDTH]
    z_a = ha[:, ATTN_WIDTH + 2 * KV_WIDTH:COL_B]

    lo = _lo_lanes((rows, LANES))
    for h in range(N_KV_HEADS):
        keep = lo if h == 0 else jnp.logical_not(lo)
        for g in range(Q_GROUP):
            pair = q[:, (h * 2 + g // 2) * LANES:(h * 2 + g // 2 + 1) * LANES]
            if g % 2 != h:
                pair = pltpu.roll(pair, HEAD_DIM, axis=1)
            piece = jnp.where(keep, pair, 0.0).reshape(nseq, dec_seq, LANES)
            s0 = (h * Q_GROUP + g) * dec_seq
            q_scr[:, s0:s0 + dec_seq, :] = piece

    kt_new = jnp.transpose(k)
    vt_new = jnp.transpose(v)
    qall = q_scr[...].reshape(nseq * stack, LANES).astype(BF16)
    s_new_all = _dot(qall, kt_new.astype(BF16)).reshape(nseq, stack, LANES)

    srow = lax.broadcasted_iota(jnp.int32, (stack, 1), 0) // dec_seq
    sink = jnp.zeros((stack, 1), F32)
    for n in range(N_HEADS):
        sink = jnp.where(srow == n, sinks_ref[layer, n], sink)
    bias_old = bold_scr[...]
    lane = lax.broadcasted_iota(jnp.int32, (KV_WIDTH, w), 1)
    s_olds = [_dot(q_scr[b].astype(BF16), ck_ref[b].reshape(KV_WIDTH, w).astype(BF16)) + bias_old
              for b in range(nseq)]
    hc = proj(COL_C, COL_G)
    gates = []
    for b in range(nseq):
        if b % (nseq // 4) == 0 and 0 < b:
            gates.append(gate(len(gates)))
        kt = ck_ref[b].reshape(KV_WIDTH, w)
        vt = cv_ref[b].reshape(KV_WIDTH, w)
        s_old = s_olds[b]
        s_new = s_new_all[b] + bnew_scr[b]
        mx = jnp.maximum(jnp.maximum(jnp.max(s_old, axis=-1, keepdims=True),
                                     jnp.max(s_new, axis=-1, keepdims=True)), sink)
        e_old = jnp.exp(s_old - mx)
        e_new = jnp.exp(s_new - mx)
        den = (jnp.sum(e_old, axis=-1, keepdims=True) + jnp.sum(e_new, axis=-1, keepdims=True)
               + jnp.exp(sink - mx))
        en_scr[b] = e_new
        oo_scr[b] = _dot_nt(e_old.astype(BF16), vt.astype(BF16))
        inv_scr[b] = jnp.broadcast_to(1.0 / den, (stack, LANES))
        shift = w - dec_seq - b * dec_seq
        newk = pltpu.roll(kt_new, shift, axis=1) if shift else kt_new
        newv = pltpu.roll(vt_new, shift, axis=1) if shift else vt_new
        keep_old = lane < w - dec_seq
        sk_ref[b] = jnp.where(keep_old, pltpu.roll(kt, w - dec_seq, axis=1), newk).reshape(
            N_KV_HEADS, HEAD_DIM, w)
        sv_ref[b] = jnp.where(keep_old, pltpu.roll(vt, w - dec_seq, axis=1), newv).reshape(
            N_KV_HEADS, HEAD_DIM, w)

    o_new = _dot(en_scr[...].reshape(nseq * stack, LANES).astype(BF16), v.astype(BF16))
    o = (oo_scr[...] + o_new.reshape(nseq, stack, LANES)) * inv_scr[...]
    pairs = []
    for h in range(N_KV_HEADS):
        heads = []
        for g in range(Q_GROUP):
            s0 = (h * Q_GROUP + g) * dec_seq
            og = o[:, s0:s0 + dec_seq, :].reshape(rows, LANES)
            if g % 2 != h:
                og = pltpu.roll(og, HEAD_DIM, axis=1)
            heads.append(og)
        pairs.append(jnp.where(lo, heads[0], heads[1]))
        pairs.append(jnp.where(lo, heads[2], heads[3]))
    y_a = jnp.concatenate(pairs, axis=1)

    st = st_ref[...]

    def conv_shift(ci):
        t = lax.rem(lax.broadcasted_iota(jnp.int32, ci.shape, 0), dec_seq)
        xm1 = jnp.where(t == 0, pltpu.roll(st, rows - 1, axis=0), pltpu.roll(ci, 1, axis=0))
        xm2 = jnp.where(t < CONV_K - 1, st, pltpu.roll(ci, 2, axis=0))
        return xm1, xm2

    mr = lax.broadcasted_iota(jnp.int32, (CHUNK, CHUNK), 0)
    mc = lax.broadcasted_iota(jnp.int32, (CHUNK, CHUNK), 1)
    mix_mask = (mr // dec_seq == mc // dec_seq) & (mc <= mr)
    y, conv_in, v_c = _tail(x, hb, hc, gates, y_a, z_a, conv_shift, mix_mask, conv_w_ref,
                            vnw_ref, mix_ref, spb_ref, woa_ref, wob_ref, woc_ref, wo_ref)
    x_scr[pl.ds(r0, rows), :] = y

    @pl.when(layer == pl.num_programs(0) - 1)
    def _():
        y_ref[...] = y
    ci_ref[...] = conv_in
    scv_ref[...] = v_c


WEIGHT_NAMES = ("norm_w", "w_in", "b_gate", "qw", "kw", "conv_w", "vnw", "mix", "spb",
                "woa", "wob", "woc", "wo")


def _layer_spec(arr, layer_of):
    nd = arr.ndim - 1
    return pl.BlockSpec((None,) + arr.shape[1:], lambda *g, _nd=nd: (layer_of(*g),) + (0,) * _nd,
                        pipeline_mode=pl.Buffered(1))


def _prompt_layer(x, wts, layer, batch):
    n, d = x.shape
    rows = PROMPT_ROWS
    nblk = n // batch // rows
    smem = pl.BlockSpec(memory_space=pltpu.SMEM)
    in_specs = ([smem, pl.BlockSpec((rows, d), lambda i: (i, 0))]
                + [_layer_spec(wts[name], lambda i: layer) for name in WEIGHT_NAMES])
    out_shape = (jax.ShapeDtypeStruct((n, d), F32),
                 jax.ShapeDtypeStruct((batch, WINDOW, KV_WIDTH), F32),
                 jax.ShapeDtypeStruct((batch, WINDOW, KV_WIDTH), F32),
                 jax.ShapeDtypeStruct((batch, SUBLANES, CONV_WIDTH), F32))
    out_specs = (pl.BlockSpec((rows, d), lambda i: (i, 0)),
                 pl.BlockSpec((1, WINDOW, KV_WIDTH), lambda i: (i // nblk, 0, 0)),
                 pl.BlockSpec((1, WINDOW, KV_WIDTH), lambda i: (i // nblk, 0, 0)),
                 pl.BlockSpec((1, SUBLANES, CONV_WIDTH), lambda i: (i // nblk, 0, 0)))
    n_tables = 2 * N_KV_HEADS * 2
    scratch = [pltpu.VMEM((n_tables, 2 * WINDOW, 2 * WINDOW), F32)]
    scratch += [pltpu.VMEM((WINDOW + rows, KV_WIDTH), BF16) for _ in range(4)]
    scratch += [pltpu.VMEM((SUBLANES, CONV_WIDTH), F32)]
    return pl.pallas_call(
        functools.partial(_prompt_kernel, rows=rows, nblk=nblk, layer=layer),
        grid=(n // rows,),
        in_specs=in_specs, out_specs=out_specs, out_shape=out_shape, scratch_shapes=scratch,
        compiler_params=pltpu.CompilerParams(dimension_semantics=("arbitrary",),
                                             vmem_limit_bytes=VMEM_LIMIT_BYTES),
        name="prompt_layer",
    )(wts["sinks"], x, *[wts[name] for name in WEIGHT_NAMES])


def _sample_layers(x, ck, cv, st, wts, dec_seq):
    n, d = x.shape
    depth = ck.shape[0]
    rows = SAMPLE_ROWS
    nseq = rows // dec_seq
    stack = N_HEADS * dec_seq
    w = ck.shape[-1]
    smem = pl.BlockSpec(memory_space=pltpu.SMEM)
    cache_blk = pl.BlockSpec((None, nseq, N_KV_HEADS, HEAD_DIM, w), lambda l, j: (l, j, 0, 0, 0))
    rows_blk = lambda width: pl.BlockSpec((None, rows, width), lambda l, j: (l, j, 0))
    nblk = n // rows
    x_blk = pl.BlockSpec((rows, d), lambda l, j: (jnp.where(l == 0, j, nblk - 1), 0))
    y_blk = pl.BlockSpec((rows, d), lambda l, j: (jnp.where(l == depth - 1, j, 0), 0))
    in_specs = ([smem, x_blk, cache_blk, cache_blk, rows_blk(CONV_WIDTH)]
                + [_layer_spec(wts[name], lambda l, j: l) for name in WEIGHT_NAMES])
    out_shape = (jax.ShapeDtypeStruct((n, d), F32),
                 jax.ShapeDtypeStruct(ck.shape, F32),
                 jax.ShapeDtypeStruct(cv.shape, F32),
                 jax.ShapeDtypeStruct((depth, n, CONV_WIDTH), F32),
                 jax.ShapeDtypeStruct((depth, n, MLP_WIDTH), F32))
    out_specs = (y_blk, cache_blk, cache_blk, rows_blk(CONV_WIDTH), rows_blk(MLP_WIDTH))
    scratch = [pltpu.VMEM((n, d), F32),
               pltpu.VMEM((stack, LANES), F32),
               pltpu.VMEM((nseq, stack, LANES), F32),
               pltpu.VMEM((nseq, stack, LANES), F32),
               pltpu.VMEM((nseq, stack, LANES), F32),
               pltpu.VMEM((nseq, stack, LANES), F32),
               pltpu.VMEM((nseq, stack, LANES), F32)]
    return pl.pallas_call(
        functools.partial(_sample_kernel, dec_seq=dec_seq),
        grid=(depth, n // rows),
        in_specs=in_specs, out_specs=out_specs, out_shape=out_shape, scratch_shapes=scratch,
        compiler_params=pltpu.CompilerParams(dimension_semantics=("arbitrary", "arbitrary"),
                                             vmem_limit_bytes=VMEM_LIMIT_BYTES),
        name="sample_layers",
    )(wts["sinks"], x, ck, cv, st, *[wts[name] for name in WEIGHT_NAMES])


def kernel(x_prompt, x_sample, cache_k, cache_v, state_conv, norm_w, w_in, b_gate, q_norm_w, k_norm_w,
           sinks, conv_w, v_norm_w, w_spatial, b_spatial, w_out_a, w_out_b, w_out_c, w_o):
    batch, seq, d = x_prompt.shape
    dec_batch, dec_seq, _ = x_sample.shape
    depth = w_in.shape[0]
    w_buf = cache_k.shape[2]
    assert d == D_MODEL and seq % PROMPT_ROWS == 0 and (dec_batch * dec_seq) % SAMPLE_ROWS == 0
    assert w_buf == WINDOW and dec_seq == SUBLANES and w_in.shape[2] == IN_COLS

    gw = MLP_WIDTH // N_SPATIAL_GROUPS
    reps = CHUNK // dec_seq
    spb_p = jnp.repeat(jnp.swapaxes(b_spatial, 1, 2), gw, axis=2)
    common = {
        "sinks": sinks,
        "norm_w": norm_w[:, None, :], "w_in": w_in.astype(BF16), "b_gate": b_gate[:, None, :],
        "qw": jnp.tile(q_norm_w, (1, LANES // HEAD_DIM))[:, None, :],
        "kw": jnp.tile(k_norm_w, (1, LANES // HEAD_DIM))[:, None, :],
        "conv_w": conv_w, "vnw": v_norm_w[:, None, :],
        "woa": w_out_a.astype(BF16), "wob": w_out_b.astype(BF16), "woc": w_out_c.astype(BF16),
        "wo": w_o.astype(BF16),
    }
    wts_p = dict(common, mix=w_spatial, spb=spb_p)
    wts_s = dict(common,
                 mix=jnp.tile(w_spatial[:, :, :dec_seq, :dec_seq], (1, 1, reps, reps)),
                 spb=jnp.tile(spb_p[:, :dec_seq, :], (1, reps, 1)))

    st_rows = jnp.pad(state_conv, ((0, 0), (0, 0), (0, dec_seq - (CONV_K - 1)), (0, 0)))
    st_rows = st_rows.reshape(depth, dec_batch * dec_seq, CONV_WIDTH)
    ck = jnp.transpose(cache_k, (0, 1, 3, 4, 2))
    cv = jnp.transpose(cache_v, (0, 1, 3, 4, 2))

    ys, sk, sv, ci, scv = _sample_layers(x_sample.reshape(dec_batch * dec_seq, d), ck, cv, st_rows,
                                         wts_s, dec_seq)
    sk = jnp.transpose(sk, (0, 1, 4, 2, 3))
    sv = jnp.transpose(sv, (0, 1, 4, 2, 3))
    sc = ci.reshape(depth, dec_batch, dec_seq, CONV_WIDTH)[:, :, dec_seq - (CONV_K - 1):, :]
    scv = scv.reshape(depth, dec_batch, dec_seq, MLP_WIDTH)

    yp = x_prompt.reshape(batch * seq, d)
    pk, pv, pc = [], [], []
    for l in range(depth):
        yp, k_l, v_l, c_l = _prompt_layer(yp, wts_p, l, batch)
        pk.append(k_l.reshape(batch, WINDOW, N_KV_HEADS, HEAD_DIM))
        pv.append(v_l.reshape(batch, WINDOW, N_KV_HEADS, HEAD_DIM))
        pc.append(c_l[:, SUBLANES - (CONV_K - 1):, :])

    return (yp.reshape(batch, seq, d), ys.reshape(dec_batch, dec_seq, d), jnp.stack(pk), jnp.stack(pv),
            jnp.stack(pc), sk, sv, sc, scv)
```

```python
import functools

import jax
import jax.numpy as jnp
from jax import lax
from jax.experimental import pallas as pl
from jax.experimental.pallas import tpu as pltpu

F32 = jnp.float32
BF16 = jnp.bfloat16

D_MODEL = 1024
N_HEADS = 8
N_KV_HEADS = 2
HEAD_DIM = 64
Q_GROUP = N_HEADS // N_KV_HEADS
ATTN_WIDTH = N_HEADS * HEAD_DIM
KV_WIDTH = N_KV_HEADS * HEAD_DIM
WINDOW = 128
CONV_WIDTH = 512
CONV_K = 3
CHUNK = 128
MLP_WIDTH = 512
N_SPATIAL_GROUPS = 4
EPS = 1e-6
NEG_INF = -1e30

COL_A = 0
COL_B = COL_A + 2 * ATTN_WIDTH + 2 * KV_WIDTH
COL_C = COL_B + 4 * CONV_WIDTH
COL_G = COL_C + 3 * MLP_WIDTH
IN_COLS = COL_G + 3 * D_MODEL

LANES = 128
SUBLANES = 8
PROMPT_ROWS = 512
SAMPLE_ROWS = 128
VMEM_LIMIT_BYTES = 56 * 1024 * 1024


def _dot(a, b):
    return jnp.dot(a, b, preferred_element_type=F32)


def _dot_nt(a, b):
    return lax.dot_general(a, b, (((1,), (1,)), ((), ())), preferred_element_type=F32)


def _rms(x, w):
    ms = jnp.mean(x * x, axis=-1, keepdims=True)
    return (x * lax.rsqrt(ms + EPS)) * w


def _lo_lanes(shape):
    return lax.broadcasted_iota(jnp.int32, shape, len(shape) - 1) < HEAD_DIM


def _pair_rms(x, w):
    lo = _lo_lanes(x.shape)
    sq = x * x
    s_lo = jnp.sum(jnp.where(lo, sq, 0.0), axis=-1, keepdims=True)
    s_hi = jnp.sum(jnp.where(lo, 0.0, sq), axis=-1, keepdims=True)
    ms = jnp.where(lo, s_lo, s_hi) * (1.0 / HEAD_DIM)
    return (x * lax.rsqrt(ms + EPS)) * w


def _norm_heads(x, w):
    groups = [_pair_rms(x[:, g * LANES:(g + 1) * LANES], w) for g in range(x.shape[1] // LANES)]
    return groups[0] if len(groups) == 1 else jnp.concatenate(groups, axis=1)


def _slope(head):
    return 2.0 ** (-(head + 1))


def _branch_b(hb, conv_shift, conv_w_ref, wob_ref):
    gate_b = hb[:, 0:CONV_WIDTH]
    gate_c = hb[:, CONV_WIDTH:2 * CONV_WIDTH]
    h_b = hb[:, 2 * CONV_WIDTH:3 * CONV_WIDTH]
    z_b = hb[:, 3 * CONV_WIDTH:4 * CONV_WIDTH]
    conv_in = gate_c * h_b
    xm1, xm2 = conv_shift(conv_in)
    cw = conv_w_ref[...]
    conv_out = cw[0:1] * xm2 + cw[1:2] * xm1 + cw[2:3] * conv_in
    y_b = gate_b * conv_out
    return _dot((jax.nn.silu(z_b) * y_b).astype(BF16), wob_ref[...]), conv_in


def _branch_c(hc, mix_mask, mix_rows, spb, vnw_ref, woc_ref):
    rows = hc.shape[0]
    u = hc[:, 0:MLP_WIDTH]
    v_c = _rms(hc[:, MLP_WIDTH:2 * MLP_WIDTH], vnw_ref[...])
    z_c = hc[:, 2 * MLP_WIDTH:3 * MLP_WIDTH]
    vb = v_c.astype(BF16)
    gw = MLP_WIDTH // N_SPATIAL_GROUPS
    mixes = [jnp.where(mix_mask, mix_rows(g), 0.0).astype(BF16) for g in range(N_SPATIAL_GROUPS)]
    sp_chunks = []
    for c in range(rows // CHUNK):
        r0 = c * CHUNK
        parts = [_dot(mixes[g], vb[r0:r0 + CHUNK, g * gw:(g + 1) * gw])
                 for g in range(N_SPATIAL_GROUPS)]
        sp_chunks.append(jnp.concatenate(parts, axis=1) + spb)
    sp = sp_chunks[0] if len(sp_chunks) == 1 else jnp.concatenate(sp_chunks, axis=0)
    y_c = u * sp
    return _dot((jax.nn.silu(z_c) * y_c).astype(BF16), woc_ref[...]), v_c


def _branch_a(y_a, z_a, woa_ref):
    return _dot((jax.nn.silu(z_a) * y_a).astype(BF16), woa_ref[...])


def _merge(x, a, b, c_out, gates, wo_ref):
    m = gates[0] * a + gates[1] * b + gates[2] * c_out
    return x + _dot(m.astype(BF16), wo_ref[...])


def _prompt_kernel(sinks_ref, x_ref, norm_w_ref, w_in_ref, b_gate_ref, qw_ref, kw_ref, conv_w_ref,
                   vnw_ref, mix_ref, spb_ref, woa_ref, wob_ref, woc_ref, wo_ref,
                   y_ref, pk_ref, pv_ref, pc_ref,
                   bias_scr, ka_scr, kb_scr, va_scr, vb_scr, carry_scr, *, rows, nblk, layer):
    i = pl.program_id(0)
    first = lax.rem(i, nblk) == 0
    nsub = rows // WINDOW
    stack = 2 * WINDOW

    @pl.when(i == 0)
    def _():
        r = lax.broadcasted_iota(jnp.int32, (stack, 2 * WINDOW), 0)
        col = lax.broadcasted_iota(jnp.int32, (stack, 2 * WINDOW), 1)
        top = r < WINDOW
        dist = jnp.where(top, r, r - WINDOW) + WINDOW - col
        band = (dist >= 0) & (dist < WINDOW)
        distf = dist.astype(F32)
        for flag in range(2):
            valid = band & (col >= WINDOW) if flag else band
            for h in range(N_KV_HEADS):
                for half in range(2):
                    slope = jnp.where(top, _slope(h * Q_GROUP + half), _slope(h * Q_GROUP + 2 + half))
                    bias_scr[flag * 4 + h * 2 + half] = jnp.where(valid, -(slope * distf), NEG_INF)

    @pl.when(first)
    def _():
        zeros = jnp.zeros((WINDOW, KV_WIDTH), BF16)
        ka_scr[0:WINDOW, :] = zeros
        kb_scr[0:WINDOW, :] = zeros
        va_scr[0:WINDOW, :] = zeros
        vb_scr[0:WINDOW, :] = zeros
        carry_scr[...] = jnp.zeros(carry_scr.shape, F32)

    x = x_ref[...]
    xn = _rms(x, norm_w_ref[...]).astype(BF16)
    def proj(c0, c1):
        return _dot(xn, w_in_ref[:, c0:c1])

    def gate(idx):
        g = proj(COL_G + idx * D_MODEL, COL_G + (idx + 1) * D_MODEL)
        return jax.nn.sigmoid(g + b_gate_ref[:, idx * D_MODEL:(idx + 1) * D_MODEL])

    ha = proj(COL_A, COL_B)
    hb = proj(COL_B, COL_C)
    q = _norm_heads(ha[:, 0:ATTN_WIDTH], qw_ref[...]) * (HEAD_DIM ** -0.5)
    k = _pair_rms(ha[:, ATTN_WIDTH:ATTN_WIDTH + KV_WIDTH], kw_ref[...])
    v = ha[:, ATTN_WIDTH + KV_WIDTH:ATTN_WIDTH + 2 * KV_WIDTH]
    z_a = ha[:, ATTN_WIDTH + 2 * KV_WIDTH:COL_B]

    pk_ref[0] = k[rows - WINDOW:rows, :]
    pv_ref[0] = v[rows - WINDOW:rows, :]
    ka_scr[WINDOW:WINDOW + rows, :] = k.astype(BF16)
    kb_scr[WINDOW:WINDOW + rows, :] = pltpu.roll(k, HEAD_DIM, axis=1).astype(BF16)
    va_scr[WINDOW:WINDOW + rows, :] = v.astype(BF16)
    vb_scr[WINDOW:WINDOW + rows, :] = pltpu.roll(v, HEAD_DIM, axis=1).astype(BF16)

    lo = _lo_lanes((WINDOW, LANES))
    srow = lax.broadcasted_iota(jnp.int32, (stack, 1), 0) < WINDOW
    flag = first.astype(jnp.int32)
    carry = carry_scr[...]
    prev1 = carry[SUBLANES - 1:SUBLANES, :]
    prev2 = carry[SUBLANES - 2:SUBLANES - 1, :]

    def conv_shift(ci):
        rid = lax.broadcasted_iota(jnp.int32, ci.shape, 0)
        xm1 = jnp.where(rid == 0, prev1, pltpu.roll(ci, 1, axis=0))
        xm2 = jnp.where(rid == 0, prev2, jnp.where(rid == 1, prev1, pltpu.roll(ci, 2, axis=0)))
        return xm1, xm2

    mr = lax.broadcasted_iota(jnp.int32, (CHUNK, CHUNK), 0)
    mc = lax.broadcasted_iota(jnp.int32, (CHUNK, CHUNK), 1)
    done = {}
    stages = [
        lambda: done.update(hc=proj(COL_C, COL_G)),
        lambda: done.update(g0=gate(0)),
        lambda: done.update(g1=gate(1)),
        lambda: done.update(g2=gate(2)),
    ]
    slots = 2 * nsub
    order = list(range(0, slots, 2)) + list(range(1, slots, 2))
    per_slot = [0] * slots
    for n in range(len(stages)):
        per_slot[order[n % slots]] += 1

    def run_stages(slot):
        for _ in range(per_slot[slot]):
            stages.pop(0)()

    y_rows = []
    for c in range(nsub):
        r0 = c * WINDOW
        keys = slice(r0, r0 + 2 * WINDOW)
        k_nat, k_swp = ka_scr[keys, :], kb_scr[keys, :]
        v_nat, v_swp = va_scr[keys, :], vb_scr[keys, :]
        scores = []
        for h in range(N_KV_HEADS):
            p0 = q[r0:r0 + WINDOW, (2 * h) * LANES:(2 * h + 1) * LANES]
            p1 = q[r0:r0 + WINDOW, (2 * h + 1) * LANES:(2 * h + 2) * LANES]
            for half in range(2):
                keep = lo if half == 0 else jnp.logical_not(lo)
                qs = jnp.concatenate([jnp.where(keep, p0, 0.0), jnp.where(keep, p1, 0.0)],
                                     axis=0).astype(BF16)
                kk = k_nat if h == half else k_swp
                tbl = h * 2 + half
                bias = bias_scr[flag * 4 + tbl] if c == 0 else bias_scr[tbl]
                scores.append(_dot_nt(qs, kk) + bias)
        run_stages(2 * c)
        pairs = []
        for h in range(N_KV_HEADS):
            outs = []
            for half in range(2):
                s = scores[h * 2 + half]
                vv = v_nat if h == half else v_swp
                sink = jnp.where(srow, sinks_ref[layer, h * Q_GROUP + half],
                                 sinks_ref[layer, h * Q_GROUP + 2 + half])
                mx = jnp.maximum(jnp.max(s, axis=-1, keepdims=True), sink)
                e = jnp.exp(s - mx)
                den = jnp.sum(e, axis=-1, keepdims=True) + jnp.exp(sink - mx)
                outs.append(_dot(e.astype(BF16), vv) * (1.0 / den))
            pairs.append(jnp.where(lo, outs[0][0:WINDOW], outs[1][0:WINDOW]))
            pairs.append(jnp.where(lo, outs[0][WINDOW:stack], outs[1][WINDOW:stack]))
        y_rows.append(jnp.concatenate(pairs, axis=1))
        run_stages(2 * c + 1)
    y_a = y_rows[0] if nsub == 1 else jnp.concatenate(y_rows, axis=0)

    ka_scr[0:WINDOW, :] = ka_scr[rows:rows + WINDOW, :]
    kb_scr[0:WINDOW, :] = kb_scr[rows:rows + WINDOW, :]
    va_scr[0:WINDOW, :] = va_scr[rows:rows + WINDOW, :]
    vb_scr[0:WINDOW, :] = vb_scr[rows:rows + WINDOW, :]

    a_out = _branch_a(y_a, z_a, woa_ref)
    b_out, conv_in = _branch_b(hb, conv_shift, conv_w_ref, wob_ref)
    c_out, _ = _branch_c(done["hc"], mc <= mr, lambda g: mix_ref[g], spb_ref[...], vnw_ref, woc_ref)
    y_ref[...] = _merge(x, a_out, b_out, c_out, [done["g0"], done["g1"], done["g2"]], wo_ref)
    last = conv_in[rows - SUBLANES:rows, :]
    carry_scr[...] = last
    pc_ref[0] = last


def _sample_kernel(sinks_ref, x_ref, ck_ref, cv_ref, st_ref, norm_w_ref, w_in_ref, b_gate_ref, qw_ref,
                   kw_ref, conv_w_ref, vnw_ref, mix_ref, spb_ref, woa_ref, wob_ref, woc_ref, wo_ref,
                   y_ref, sk_ref, sv_ref, ci_ref, scv_ref,
                   x_scr, bold_scr, bnew_scr, q_scr, en_scr, oo_scr, inv_scr, *, dec_seq):
    layer = pl.program_id(0)
    j = pl.program_id(1)
    rows = SAMPLE_ROWS
    nseq = rows // dec_seq
    stack = N_HEADS * dec_seq
    w = WINDOW

    @pl.when((layer == 0) & (j == 0))
    def _():
        r = lax.broadcasted_iota(jnp.int32, (stack, LANES), 0)
        col = lax.broadcasted_iota(jnp.int32, (stack, LANES), 1)
        tok = lax.rem(r, dec_seq)
        head = r // dec_seq
        slope = jnp.zeros((stack, LANES), F32)
        for n in range(N_HEADS):
            slope = jnp.where(head == n, _slope(n), slope)
        dist = tok + w - col
        bold_scr[...] = jnp.where(dist < WINDOW, -(slope * dist.astype(F32)), NEG_INF)
        kseq = col // dec_seq
        dist = tok - lax.rem(col, dec_seq)
        pen = -(slope * dist.astype(F32))
        for b in range(nseq):
            bnew_scr[b] = jnp.where((kseq == b) & (dist >= 0), pen, NEG_INF)

    r0 = pl.multiple_of(j * rows, rows)

    @pl.when(layer == 0)
    def _():
        x_scr[pl.ds(r0, rows), :] = x_ref[...]

    x = x_scr[pl.ds(r0, rows), :]
    xn = _rms(x, norm_w_ref[...]).astype(BF16)
    def proj(c0, c1):
        return _dot(xn, w_in_ref[:, c0:c1])

    def gate(idx):
        g = proj(COL_G + idx * D_MODEL, COL_G + (idx + 1) * D_MODEL)
        return jax.nn.sigmoid(g + b_gate_ref[:, idx * D_MODEL:(idx + 1) * D_MODEL])

    ha = proj(COL_A, COL_B)
    hb = proj(COL_B, COL_C)
    q = _norm_heads(ha[:, 0:ATTN_WIDTH], qw_ref[...]) * (HEAD_DIM ** -0.5)
    k = _pair_rms(ha[:, ATTN_WIDTH:ATTN_WIDTH + KV_WIDTH], kw_ref[...])
    v = ha[:, ATTN_WIDTH + KV_WIDTH:ATTN_WIDTH + 2 * KV_WIDTH]
    z_a = ha[:, ATTN_WIDTH + 2 * KV_WIDTH:COL_B]

    lo = _lo_lanes((rows, LANES))
    for h in range(N_KV_HEADS):
        keep = lo if h == 0 else jnp.logical_not(lo)
        for g in range(Q_GROUP):
            pair = q[:, (h * 2 + g // 2) * LANES:(h * 2 + g // 2 + 1) * LANES]
            if g % 2 != h:
                pair = pltpu.roll(pair, HEAD_DIM, axis=1)
            piece = jnp.where(keep, pair, 0.0).reshape(nseq, dec_seq, LANES)
            s0 = (h * Q_GROUP + g) * dec_seq
            q_scr[:, s0:s0 + dec_seq, :] = piece

    kt_new = jnp.transpose(k)
    vt_new = jnp.transpose(v)
    qall = q_scr[...].reshape(nseq * stack, LANES).astype(BF16)
    s_new_all = _dot(qall, kt_new.astype(BF16)).reshape(nseq, stack, LANES)

    srow = lax.broadcasted_iota(jnp.int32, (stack, 1), 0) // dec_seq
    sink = jnp.zeros((stack, 1), F32)
    for n in range(N_HEADS):
        sink = jnp.where(srow == n, sinks_ref[layer, n], sink)
    bias_old = bold_scr[...]
    lane = lax.broadcasted_iota(jnp.int32, (KV_WIDTH, w), 1)
    s_olds = [_dot(q_scr[b].astype(BF16), ck_ref[b].reshape(KV_WIDTH, w).astype(BF16)) + bias_old
              for b in range(nseq)]
    hc = proj(COL_C, COL_G)
    gates = []
    for b in range(nseq):
        if b % (nseq // 4) == 0 and 0 < b:
            gates.append(gate(len(gates)))
        kt = ck_ref[b].reshape(KV_WIDTH, w)
        vt = cv_ref[b].reshape(KV_WIDTH, w)
        s_old = s_olds[b]
        s_new = s_new_all[b] + bnew_scr[b]
        mx = jnp.maximum(jnp.maximum(jnp.max(s_old, axis=-1, keepdims=True),
                                     jnp.max(s_new, axis=-1, keepdims=True)), sink)
        e_old = jnp.exp(s_old - mx)
        e_new = jnp.exp(s_new - mx)
        den = (jnp.sum(e_old, axis=-1, keepdims=True) + jnp.sum(e_new, axis=-1, keepdims=True)
               + jnp.exp(sink - mx))
        en_scr[b] = e_new
        oo_scr[b] = _dot_nt(e_old.astype(BF16), vt.astype(BF16))
        inv_scr[b] = jnp.broadcast_to(1.0 / den, (stack, LANES))
        shift = w - dec_seq - b * dec_seq
        newk = pltpu.roll(kt_new, shift, axis=1) if shift else kt_new
        newv = pltpu.roll(vt_new, shift, axis=1) if shift else vt_new
        keep_old = lane < w - dec_seq
        sk_ref[b] = jnp.where(keep_old, pltpu.roll(kt, w - dec_seq, axis=1), newk).reshape(
            N_KV_HEADS, HEAD_DIM, w)
        sv_ref[b] = jnp.where(keep_old, pltpu.roll(vt, w - dec_seq, axis=1), newv).reshape(
            N_KV_HEADS, HEAD_DIM, w)

    o_new = _dot(en_scr[...].reshape(nseq * stack, LANES).astype(BF16), v.astype(BF16))
    o = (oo_scr[...] + o_new.reshape(nseq, stack, LANES)) * inv_scr[...]
    pairs = []
    for h in range(N_KV_HEADS):
        heads = []
        for g in range(Q_GROUP):
            s0 = (h * Q_GROUP + g) * dec_seq
            og = o[:, s0:s0 + dec_seq, :].reshape(rows, LANES)
            if g % 2 != h:
                og = pltpu.roll(og, HEAD_DIM, axis=1)
            heads.append(og)
        pairs.append(jnp.where(lo, heads[0], heads[1]))
        pairs.append(jnp.where(lo, heads[2], heads[3]))
    y_a = jnp.concatenate(pairs, axis=1)

    st = st_ref[...]

    def conv_shift(ci):
        t = lax.rem(lax.broadcasted_iota(jnp.int32, ci.shape, 0), dec_seq)
        xm1 = jnp.where(t == 0, pltpu.roll(st, rows - 1, axis=0), pltpu.roll(ci, 1, axis=0))
        xm2 = jnp.where(t < CONV_K - 1, st, pltpu.roll(ci, 2, axis=0))
        return xm1, xm2

    mr = lax.broadcasted_iota(jnp.int32, (CHUNK, CHUNK), 0)
    mc = lax.broadcasted_iota(jnp.int32, (CHUNK, CHUNK), 1)
    mix_mask = (mr // dec_seq == mc // dec_seq) & (mc <= mr)
    b_out, conv_in = _branch_b(hb, conv_shift, conv_w_ref, wob_ref)
    reps = CHUNK // dec_seq

    def mix_rows(g):
        return jnp.broadcast_to(mix_ref[g][None], (reps, dec_seq, CHUNK)).reshape(CHUNK, CHUNK)

    spb = jnp.broadcast_to(spb_ref[...][None], (reps, dec_seq, MLP_WIDTH)).reshape(CHUNK, MLP_WIDTH)
    c_out, v_c = _branch_c(hc, mix_mask, mix_rows, spb, vnw_ref, woc_ref)
    y = _merge(x, _branch_a(y_a, z_a, woa_ref), b_out, c_out, gates, wo_ref)
    x_scr[pl.ds(r0, rows), :] = y

    @pl.when(layer == pl.num_programs(0) - 1)
    def _():
        y_ref[...] = y
    ci_ref[...] = conv_in
    scv_ref[...] = v_c


WEIGHT_NAMES = ("norm_w", "w_in", "b_gate", "qw", "kw", "conv_w", "vnw", "mix", "spb",
                "woa", "wob", "woc", "wo")


def _layer_spec(arr, layer_of):
    nd = arr.ndim - 1
    return pl.BlockSpec((None,) + arr.shape[1:], lambda *g, _nd=nd: (layer_of(*g),) + (0,) * _nd,
                        pipeline_mode=pl.Buffered(1))


def _prompt_layer(x, wts, layer, batch):
    n, d = x.shape
    rows = PROMPT_ROWS
    nblk = n // batch // rows
    smem = pl.BlockSpec(memory_space=pltpu.SMEM)
    in_specs = ([smem, pl.BlockSpec((rows, d), lambda i: (i, 0))]
                + [_layer_spec(wts[name], lambda i: layer) for name in WEIGHT_NAMES])
    out_shape = (jax.ShapeDtypeStruct((n, d), F32),
                 jax.ShapeDtypeStruct((batch, WINDOW, KV_WIDTH), F32),
                 jax.ShapeDtypeStruct((batch, WINDOW, KV_WIDTH), F32),
                 jax.ShapeDtypeStruct((batch, SUBLANES, CONV_WIDTH), F32))
    out_specs = (pl.BlockSpec((rows, d), lambda i: (i, 0)),
                 pl.BlockSpec((1, WINDOW, KV_WIDTH), lambda i: (i // nblk, 0, 0)),
                 pl.BlockSpec((1, WINDOW, KV_WIDTH), lambda i: (i // nblk, 0, 0)),
                 pl.BlockSpec((1, SUBLANES, CONV_WIDTH), lambda i: (i // nblk, 0, 0)))
    n_tables = 2 * N_KV_HEADS * 2
    scratch = [pltpu.VMEM((n_tables, 2 * WINDOW, 2 * WINDOW), F32)]
    scratch += [pltpu.VMEM((WINDOW + rows, KV_WIDTH), BF16) for _ in range(4)]
    scratch += [pltpu.VMEM((SUBLANES, CONV_WIDTH), F32)]
    return pl.pallas_call(
        functools.partial(_prompt_kernel, rows=rows, nblk=nblk, layer=layer),
        grid=(n // rows,),
        in_specs=in_specs, out_specs=out_specs, out_shape=out_shape, scratch_shapes=scratch,
        compiler_params=pltpu.CompilerParams(dimension_semantics=("arbitrary",),
                                             vmem_limit_bytes=VMEM_LIMIT_BYTES),
        name="prompt_layer",
    )(wts["sinks"], x, *[wts[name] for name in WEIGHT_NAMES])


def _sample_layers(x, ck, cv, st, wts, dec_seq):
    n, d = x.shape
    depth = ck.shape[0]
    rows = SAMPLE_ROWS
    nseq = rows // dec_seq
    stack = N_HEADS * dec_seq
    w = ck.shape[-1]
    smem = pl.BlockSpec(memory_space=pltpu.SMEM)
    cache_blk = pl.BlockSpec((None, nseq, N_KV_HEADS, HEAD_DIM, w), lambda l, j: (l, j, 0, 0, 0))
    rows_blk = lambda width: pl.BlockSpec((None, rows, width), lambda l, j: (l, j, 0))
    nblk = n // rows
    x_blk = pl.BlockSpec((rows, d), lambda l, j: (jnp.where(l == 0, j, nblk - 1), 0))
    y_blk = pl.BlockSpec((rows, d), lambda l, j: (jnp.where(l == depth - 1, j, 0), 0))
    in_specs = ([smem, x_blk, cache_blk, cache_blk, rows_blk(CONV_WIDTH)]
                + [_layer_spec(wts[name], lambda l, j: l) for name in WEIGHT_NAMES])
    out_shape = (jax.ShapeDtypeStruct((n, d), F32),
                 jax.ShapeDtypeStruct(ck.shape, F32),
                 jax.ShapeDtypeStruct(cv.shape, F32),
                 jax.ShapeDtypeStruct((depth, n, CONV_WIDTH), F32),
                 jax.ShapeDtypeStruct((depth, n, MLP_WIDTH), F32))
    out_specs = (y_blk, cache_blk, cache_blk, rows_blk(CONV_WIDTH), rows_blk(MLP_WIDTH))
    scratch = [pltpu.VMEM((n, d), F32),
               pltpu.VMEM((stack, LANES), F32),
               pltpu.VMEM((nseq, stack, LANES), F32),
               pltpu.VMEM((nseq, stack, LANES), F32),
               pltpu.VMEM((nseq, stack, LANES), F32),
               pltpu.VMEM((nseq, stack, LANES), F32),
               pltpu.VMEM((nseq, stack, LANES), F32)]
    return pl.pallas_call(
        functools.partial(_sample_kernel, dec_seq=dec_seq),
        grid=(depth, n // rows),
        in_specs=in_specs, out_specs=out_specs, out_shape=out_shape, scratch_shapes=scratch,
        compiler_params=pltpu.CompilerParams(dimension_semantics=("arbitrary", "arbitrary"),
                                             vmem_limit_bytes=VMEM_LIMIT_BYTES),
        name="sample_layers",
    )(wts["sinks"], x, ck, cv, st, *[wts[name] for name in WEIGHT_NAMES])


def kernel(x_prompt, x_sample, cache_k, cache_v, state_conv, norm_w, w_in, b_gate, q_norm_w, k_norm_w,
           sinks, conv_w, v_norm_w, w_spatial, b_spatial, w_out_a, w_out_b, w_out_c, w_o):
    batch, seq, d = x_prompt.shape
    dec_batch, dec_seq, _ = x_sample.shape
    depth = w_in.shape[0]
    w_buf = cache_k.shape[2]
    assert d == D_MODEL and seq % PROMPT_ROWS == 0 and (dec_batch * dec_seq) % SAMPLE_ROWS == 0
    assert w_buf == WINDOW and dec_seq == SUBLANES and w_in.shape[2] == IN_COLS

    gw = MLP_WIDTH // N_SPATIAL_GROUPS
    reps = CHUNK // dec_seq
    spb_p = jnp.repeat(jnp.swapaxes(b_spatial, 1, 2), gw, axis=2)
    common = {
        "sinks": sinks,
        "norm_w": norm_w[:, None, :], "w_in": w_in.astype(BF16), "b_gate": b_gate[:, None, :],
        "qw": jnp.tile(q_norm_w, (1, LANES // HEAD_DIM))[:, None, :],
        "kw": jnp.tile(k_norm_w, (1, LANES // HEAD_DIM))[:, None, :],
        "conv_w": conv_w, "vnw": v_norm_w[:, None, :],
        "woa": w_out_a.astype(BF16), "wob": w_out_b.astype(BF16), "woc": w_out_c.astype(BF16),
        "wo": w_o.astype(BF16),
    }
    wts_p = dict(common, mix=w_spatial, spb=spb_p)
    wts_s = dict(common,
                 mix=jnp.tile(w_spatial[:, :, :dec_seq, :dec_seq], (1, 1, 1, reps)),
                 spb=spb_p[:, :dec_seq, :])

    st_rows = jnp.pad(state_conv, ((0, 0), (0, 0), (0, dec_seq - (CONV_K - 1)), (0, 0)))
    st_rows = st_rows.reshape(depth, dec_batch * dec_seq, CONV_WIDTH)
    ck = jnp.transpose(cache_k, (0, 1, 3, 4, 2))
    cv = jnp.transpose(cache_v, (0, 1, 3, 4, 2))

    ys, sk, sv, ci, scv = _sample_layers(x_sample.reshape(dec_batch * dec_seq, d), ck, cv, st_rows,
                                         wts_s, dec_seq)
    sk = jnp.transpose(sk, (0, 1, 4, 2, 3))
    sv = jnp.transpose(sv, (0, 1, 4, 2, 3))
    sc = ci.reshape(depth, dec_batch, dec_seq, CONV_WIDTH)[:, :, dec_seq - (CONV_K - 1):, :]
    scv = scv.reshape(depth, dec_batch, dec_seq, MLP_WIDTH)

    yp = x_prompt.reshape(batch * seq, d)
    pk, pv, pc = [], [], []
    for l in range(depth):
        yp, k_l, v_l, c_l = _prompt_layer(yp, wts_p, l, batch)
        pk.append(k_l.reshape(batch, WINDOW, N_KV_HEADS, HEAD_DIM))
        pv.append(v_l.reshape(batch, WINDOW, N_KV_HEADS, HEAD_DIM))
        pc.append(c_l[:, SUBLANES - (CONV_K - 1):, :])

    return (yp.reshape(batch, seq, d), ys.reshape(dec_batch, dec_seq, d), jnp.stack(pk), jnp.stack(pv),
            jnp.stack(pc), sk, sv, sc, scv)
```

```python
import functools

import jax
import jax.numpy as jnp
from jax import lax
from jax.experimental import pallas as pl
from jax.experimental.pallas import tpu as pltpu

F32 = jnp.float32
BF16 = jnp.bfloat16

D_MODEL = 1024
N_HEADS = 8
N_KV_HEADS = 2
HEAD_DIM = 64
Q_GROUP = N_HEADS // N_KV_HEADS
ATTN_WIDTH = N_HEADS * HEAD_DIM
KV_WIDTH = N_KV_HEADS * HEAD_DIM
WINDOW = 128
CONV_WIDTH = 512
CONV_K = 3
CHUNK = 128
MLP_WIDTH = 512
N_SPATIAL_GROUPS = 4
EPS = 1e-6
NEG_INF = -1e30

COL_A = 0
COL_B = COL_A + 2 * ATTN_WIDTH + 2 * KV_WIDTH
COL_C = COL_B + 4 * CONV_WIDTH
COL_G = COL_C + 3 * MLP_WIDTH
IN_COLS = COL_G + 3 * D_MODEL

LANES = 128
SUBLANES = 8
PROMPT_ROWS = 512
SAMPLE_ROWS = 128
VMEM_LIMIT_BYTES = 56 * 1024 * 1024


def _dot(a, b):
    return jnp.dot(a, b, preferred_element_type=F32)


def _dot_nt(a, b):
    return lax.dot_general(a, b, (((1,), (1,)), ((), ())), preferred_element_type=F32)


def _rms(x, w):
    ms = jnp.mean(x * x, axis=-1, keepdims=True)
    return (x * lax.rsqrt(ms + EPS)) * w


def _lo_lanes(shape):
    return lax.broadcasted_iota(jnp.int32, shape, len(shape) - 1) < HEAD_DIM


def _pair_rms(x, w):
    lo = _lo_lanes(x.shape)
    sq = x * x
    s_lo = jnp.sum(jnp.where(lo, sq, 0.0), axis=-1, keepdims=True)
    s_hi = jnp.sum(jnp.where(lo, 0.0, sq), axis=-1, keepdims=True)
    ms = jnp.where(lo, s_lo, s_hi) * (1.0 / HEAD_DIM)
    return (x * lax.rsqrt(ms + EPS)) * w


def _norm_heads(x, w):
    groups = [_pair_rms(x[:, g * LANES:(g + 1) * LANES], w) for g in range(x.shape[1] // LANES)]
    return groups[0] if len(groups) == 1 else jnp.concatenate(groups, axis=1)


def _slope(head):
    return 2.0 ** (-(head + 1))


def _branch_b(hb, conv_shift, conv_w_ref, wob_ref):
    gate_b = hb[:, 0:CONV_WIDTH]
    gate_c = hb[:, CONV_WIDTH:2 * CONV_WIDTH]
    h_b = hb[:, 2 * CONV_WIDTH:3 * CONV_WIDTH]
    z_b = hb[:, 3 * CONV_WIDTH:4 * CONV_WIDTH]
    conv_in = gate_c * h_b
    xm1, xm2 = conv_shift(conv_in)
    cw = conv_w_ref[...]
    conv_out = cw[0:1] * xm2 + cw[1:2] * xm1 + cw[2:3] * conv_in
    y_b = gate_b * conv_out
    return _dot((jax.nn.silu(z_b) * y_b).astype(BF16), wob_ref[...]), conv_in


def _branch_c(hc, mix_mask, mix_rows, spb, vnw_ref, woc_ref):
    rows = hc.shape[0]
    u = hc[:, 0:MLP_WIDTH]
    v_c = _rms(hc[:, MLP_WIDTH:2 * MLP_WIDTH], vnw_ref[...])
    z_c = hc[:, 2 * MLP_WIDTH:3 * MLP_WIDTH]
    vb = v_c.astype(BF16)
    gw = MLP_WIDTH // N_SPATIAL_GROUPS
    mixes = [jnp.where(mix_mask, mix_rows(g), 0.0).astype(BF16) for g in range(N_SPATIAL_GROUPS)]
    sp_chunks = []
    for c in range(rows // CHUNK):
        r0 = c * CHUNK
        parts = [_dot(mixes[g], vb[r0:r0 + CHUNK, g * gw:(g + 1) * gw])
                 for g in range(N_SPATIAL_GROUPS)]
        sp_chunks.append(jnp.concatenate(parts, axis=1) + spb)
    sp = sp_chunks[0] if len(sp_chunks) == 1 else jnp.concatenate(sp_chunks, axis=0)
    y_c = u * sp
    return _dot((jax.nn.silu(z_c) * y_c).astype(BF16), woc_ref[...]), v_c


def _branch_a(y_a, z_a, woa_ref):
    return _dot((jax.nn.silu(z_a) * y_a).astype(BF16), woa_ref[...])


def _merge(x, a, b, c_out, gates, wo_ref):
    m = gates[0] * a + gates[1] * b + gates[2] * c_out
    return x + _dot(m.astype(BF16), wo_ref[...])


def _prompt_kernel(sinks_ref, x_ref, norm_w_ref, w_in_ref, b_gate_ref, qw_ref, kw_ref, conv_w_ref,
                   vnw_ref, mix_ref, spb_ref, woa_ref, wob_ref, woc_ref, wo_ref,
                   y_ref, pk_ref, pv_ref, pc_ref,
                   bias_scr, ka_scr, kb_scr, va_scr, vb_scr, carry_scr, *, rows, nblk, layer):
    i = pl.program_id(0)
    first = lax.rem(i, nblk) == 0
    nsub = rows // WINDOW
    stack = 2 * WINDOW

    @pl.when(i == 0)
    def _():
        r = lax.broadcasted_iota(jnp.int32, (stack, 2 * WINDOW), 0)
        col = lax.broadcasted_iota(jnp.int32, (stack, 2 * WINDOW), 1)
        top = r < WINDOW
        dist = jnp.where(top, r, r - WINDOW) + WINDOW - col
        band = (dist >= 0) & (dist < WINDOW)
        distf = dist.astype(F32)
        for flag in range(2):
            valid = band & (col >= WINDOW) if flag else band
            for h in range(N_KV_HEADS):
                for half in range(2):
                    slope = jnp.where(top, _slope(h * Q_GROUP + half), _slope(h * Q_GROUP + 2 + half))
                    bias_scr[flag * 4 + h * 2 + half] = jnp.where(valid, -(slope * distf), NEG_INF)

    @pl.when(first)
    def _():
        zeros = jnp.zeros((WINDOW, KV_WIDTH), BF16)
        ka_scr[0:WINDOW, :] = zeros
        kb_scr[0:WINDOW, :] = zeros
        va_scr[0:WINDOW, :] = zeros
        vb_scr[0:WINDOW, :] = zeros
        carry_scr[...] = jnp.zeros(carry_scr.shape, F32)

    x = x_ref[...]
    xn = _rms(x, norm_w_ref[...]).astype(BF16)
    def proj(c0, c1):
        return _dot(xn, w_in_ref[:, c0:c1])

    def gate(idx):
        g = proj(COL_G + idx * D_MODEL, COL_G + (idx + 1) * D_MODEL)
        return jax.nn.sigmoid(g + b_gate_ref[:, idx * D_MODEL:(idx + 1) * D_MODEL])

    ha = proj(COL_A, COL_B)
    hb = proj(COL_B, COL_C)
    q = _norm_heads(ha[:, 0:ATTN_WIDTH], qw_ref[...]) * (HEAD_DIM ** -0.5)
    k = _pair_rms(ha[:, ATTN_WIDTH:ATTN_WIDTH + KV_WIDTH], kw_ref[...])
    v = ha[:, ATTN_WIDTH + KV_WIDTH:ATTN_WIDTH + 2 * KV_WIDTH]
    z_a = ha[:, ATTN_WIDTH + 2 * KV_WIDTH:COL_B]

    pk_ref[0] = k[rows - WINDOW:rows, :]
    pv_ref[0] = v[rows - WINDOW:rows, :]
    ka_scr[WINDOW:WINDOW + rows, :] = k.astype(BF16)
    kb_scr[WINDOW:WINDOW + rows, :] = pltpu.roll(k, HEAD_DIM, axis=1).astype(BF16)
    va_scr[WINDOW:WINDOW + rows, :] = v.astype(BF16)
    vb_scr[WINDOW:WINDOW + rows, :] = pltpu.roll(v, HEAD_DIM, axis=1).astype(BF16)

    lo = _lo_lanes((WINDOW, LANES))
    srow = lax.broadcasted_iota(jnp.int32, (stack, 1), 0) < WINDOW
    flag = first.astype(jnp.int32)
    carry = carry_scr[...]
    prev1 = carry[SUBLANES - 1:SUBLANES, :]
    prev2 = carry[SUBLANES - 2:SUBLANES - 1, :]

    def conv_shift(ci):
        rid = lax.broadcasted_iota(jnp.int32, ci.shape, 0)
        xm1 = jnp.where(rid == 0, prev1, pltpu.roll(ci, 1, axis=0))
        xm2 = jnp.where(rid == 0, prev2, jnp.where(rid == 1, prev1, pltpu.roll(ci, 2, axis=0)))
        return xm1, xm2

    mr = lax.broadcasted_iota(jnp.int32, (CHUNK, CHUNK), 0)
    mc = lax.broadcasted_iota(jnp.int32, (CHUNK, CHUNK), 1)
    done = {}
    stages = [
        lambda: done.update(hc=proj(COL_C, COL_G)),
        lambda: done.update(g0=gate(0)),
        lambda: done.update(g1=gate(1)),
        lambda: done.update(g2=gate(2)),
    ]
    slots = 2 * nsub
    order = list(range(0, slots, 2)) + list(range(1, slots, 2))
    per_slot = [0] * slots
    for n in range(len(stages)):
        per_slot[order[n % slots]] += 1

    def run_stages(slot):
        for _ in range(per_slot[slot]):
            stages.pop(0)()

    y_rows = []
    for c in range(nsub):
        r0 = c * WINDOW
        keys = slice(r0, r0 + 2 * WINDOW)
        k_nat, k_swp = ka_scr[keys, :], kb_scr[keys, :]
        v_nat, v_swp = va_scr[keys, :], vb_scr[keys, :]
        scores = []
        for h in range(N_KV_HEADS):
            p0 = q[r0:r0 + WINDOW, (2 * h) * LANES:(2 * h + 1) * LANES]
            p1 = q[r0:r0 + WINDOW, (2 * h + 1) * LANES:(2 * h + 2) * LANES]
            for half in range(2):
                keep = lo if half == 0 else jnp.logical_not(lo)
                qs = jnp.concatenate([jnp.where(keep, p0, 0.0), jnp.where(keep, p1, 0.0)],
                                     axis=0).astype(BF16)
                kk = k_nat if h == half else k_swp
                tbl = h * 2 + half
                bias = bias_scr[flag * 4 + tbl] if c == 0 else bias_scr[tbl]
                scores.append(_dot_nt(qs, kk) + bias)
        run_stages(2 * c)
        pairs = []
        for h in range(N_KV_HEADS):
            outs = []
            for half in range(2):
                s = scores[h * 2 + half]
                vv = v_nat if h == half else v_swp
                sink = jnp.where(srow, sinks_ref[layer, h * Q_GROUP + half],
                                 sinks_ref[layer, h * Q_GROUP + 2 + half])
                mx = jnp.maximum(jnp.max(s, axis=-1, keepdims=True), sink)
                e = jnp.exp(s - mx)
                den = jnp.sum(e, axis=-1, keepdims=True) + jnp.exp(sink - mx)
                outs.append(_dot(e.astype(BF16), vv) * (1.0 / den))
            pairs.append(jnp.where(lo, outs[0][0:WINDOW], outs[1][0:WINDOW]))
            pairs.append(jnp.where(lo, outs[0][WINDOW:stack], outs[1][WINDOW:stack]))
        y_rows.append(jnp.concatenate(pairs, axis=1))
        run_stages(2 * c + 1)
    y_a = y_rows[0] if nsub == 1 else jnp.concatenate(y_rows, axis=0)

    ka_scr[0:WINDOW, :] = ka_scr[rows:rows + WINDOW, :]
    kb_scr[0:WINDOW, :] = kb_scr[rows:rows + WINDOW, :]
    va_scr[0:WINDOW, :] = va_scr[rows:rows + WINDOW, :]
    vb_scr[0:WINDOW, :] = vb_scr[rows:rows + WINDOW, :]

    a_out = _branch_a(y_a, z_a, woa_ref)
    b_out, conv_in = _branch_b(hb, conv_shift, conv_w_ref, wob_ref)
    c_out, _ = _branch_c(done["hc"], mc <= mr, lambda g: mix_ref[g], spb_ref[...], vnw_ref, woc_ref)
    y_ref[...] = _merge(x, a_out, b_out, c_out, [done["g0"], done["g1"], done["g2"]], wo_ref)
    last = conv_in[rows - SUBLANES:rows, :]
    carry_scr[...] = last
    pc_ref[0] = last


def _stage_bf16(src_hbm, dst_scr, stage, sem, layer, chunk_rows):
    nchunks = dst_scr.shape[0] // chunk_rows

    def copy(k):
        return pltpu.make_async_copy(src_hbm.at[layer, pl.ds(k * chunk_rows, chunk_rows), :],
                                     stage.at[k % 2], sem.at[k % 2])

    copy(0).start()
    for k in range(nchunks):
        if k + 1 < nchunks:
            copy(k + 1).start()
        copy(k).wait()
        dst_scr[k * chunk_rows:(k + 1) * chunk_rows, :] = stage[k % 2].astype(BF16)


def _sample_kernel(sinks_ref, x_ref, ck_ref, cv_ref, st_ref, norm_w_ref, b_gate_ref, qw_ref,
                   kw_ref, conv_w_ref, vnw_ref, mix_ref, spb_ref,
                   w_in_hbm, woa_hbm, wob_hbm, woc_hbm, wo_hbm,
                   y_ref, sk_ref, sv_ref, ci_ref, scv_ref,
                   w_in_out, woa_out, wob_out, woc_out, wo_out,
                   x_scr, bold_scr, bnew_scr, q_scr, en_scr, oo_scr, inv_scr,
                   w_in_ref, woa_ref, wob_ref, woc_ref, wo_ref, stage_in, stage_sq,
                   sem_in, sem_sq, sem_out, *, dec_seq):
    layer = pl.program_id(0)
    j = pl.program_id(1)
    rows = SAMPLE_ROWS
    nseq = rows // dec_seq
    stack = N_HEADS * dec_seq
    w = WINDOW

    @pl.when((layer == 0) & (j == 0))
    def _():
        r = lax.broadcasted_iota(jnp.int32, (stack, LANES), 0)
        col = lax.broadcasted_iota(jnp.int32, (stack, LANES), 1)
        tok = lax.rem(r, dec_seq)
        head = r // dec_seq
        slope = jnp.zeros((stack, LANES), F32)
        for n in range(N_HEADS):
            slope = jnp.where(head == n, _slope(n), slope)
        dist = tok + w - col
        bold_scr[...] = jnp.where(dist < WINDOW, -(slope * dist.astype(F32)), NEG_INF)
        kseq = col // dec_seq
        dist = tok - lax.rem(col, dec_seq)
        pen = -(slope * dist.astype(F32))
        for b in range(nseq):
            bnew_scr[b] = jnp.where((kseq == b) & (dist >= 0), pen, NEG_INF)

    r0 = pl.multiple_of(j * rows, rows)

    resident = ((w_in_hbm, w_in_ref, w_in_out), (woa_hbm, woa_ref, woa_out), (wob_hbm, wob_ref, wob_out),
                (woc_hbm, woc_ref, woc_out), (wo_hbm, wo_ref, wo_out))

    def publish(n, lyr):
        return pltpu.make_async_copy(resident[n][1], resident[n][2].at[lyr], sem_out.at[n])

    @pl.when(j == 0)
    def _():
        @pl.when(layer > 0)
        def _():
            for n in range(len(resident)):
                publish(n, layer - 1).wait()

        _stage_bf16(w_in_hbm, w_in_ref, stage_in, sem_in, layer, stage_in.shape[1])
        for src, dst, _ in resident[1:]:
            _stage_bf16(src, dst, stage_sq, sem_sq, layer, stage_sq.shape[1])
        for n in range(len(resident)):
            publish(n, layer).start()

    @pl.when(layer == 0)
    def _():
        x_scr[pl.ds(r0, rows), :] = x_ref[...]

    x = x_scr[pl.ds(r0, rows), :]
    xn = _rms(x, norm_w_ref[...]).astype(BF16)
    def proj(c0, c1):
        return _dot(xn, w_in_ref[:, c0:c1])

    def gate(idx):
        g = proj(COL_G + idx * D_MODEL, COL_G + (idx + 1) * D_MODEL)
        return jax.nn.sigmoid(g + b_gate_ref[:, idx * D_MODEL:(idx + 1) * D_MODEL])

    ha = proj(COL_A, COL_B)
    hb = proj(COL_B, COL_C)
    q = _norm_heads(ha[:, 0:ATTN_WIDTH], qw_ref[...]) * (HEAD_DIM ** -0.5)
    k = _pair_rms(ha[:, ATTN_WIDTH:ATTN_WIDTH + KV_WIDTH], kw_ref[...])
    v = ha[:, ATTN_WIDTH + KV_WIDTH:ATTN_WIDTH + 2 * KV_WIDTH]
    z_a = ha[:, ATTN_WIDTH + 2 * KV_WIDTH:COL_B]

    lo = _lo_lanes((rows, LANES))
    for h in range(N_KV_HEADS):
        keep = lo if h == 0 else jnp.logical_not(lo)
        for g in range(Q_GROUP):
            pair = q[:, (h * 2 + g // 2) * LANES:(h * 2 + g // 2 + 1) * LANES]
            if g % 2 != h:
                pair = pltpu.roll(pair, HEAD_DIM, axis=1)
            piece = jnp.where(keep, pair, 0.0).reshape(nseq, dec_seq, LANES)
            s0 = (h * Q_GROUP + g) * dec_seq
            q_scr[:, s0:s0 + dec_seq, :] = piece

    kt_new = jnp.transpose(k)
    vt_new = jnp.transpose(v)
    qall = q_scr[...].reshape(nseq * stack, LANES).astype(BF16)
    s_new_all = _dot(qall, kt_new.astype(BF16)).reshape(nseq, stack, LANES)

    srow = lax.broadcasted_iota(jnp.int32, (stack, 1), 0) // dec_seq
    sink = jnp.zeros((stack, 1), F32)
    for n in range(N_HEADS):
        sink = jnp.where(srow == n, sinks_ref[layer, n], sink)
    bias_old = bold_scr[...]
    lane = lax.broadcasted_iota(jnp.int32, (KV_WIDTH, w), 1)
    s_olds = [_dot(q_scr[b].astype(BF16), ck_ref[b].reshape(KV_WIDTH, w).astype(BF16)) + bias_old
              for b in range(nseq)]
    hc = proj(COL_C, COL_G)
    gates = []
    for b in range(nseq):
        if b % (nseq // 4) == 0 and 0 < b:
            gates.append(gate(len(gates)))
        kt = ck_ref[b].reshape(KV_WIDTH, w)
        vt = cv_ref[b].reshape(KV_WIDTH, w)
        s_old = s_olds[b]
        s_new = s_new_all[b] + bnew_scr[b]
        mx = jnp.maximum(jnp.maximum(jnp.max(s_old, axis=-1, keepdims=True),
                                     jnp.max(s_new, axis=-1, keepdims=True)), sink)
        e_old = jnp.exp(s_old - mx)
        e_new = jnp.exp(s_new - mx)
        den = (jnp.sum(e_old, axis=-1, keepdims=True) + jnp.sum(e_new, axis=-1, keepdims=True)
               + jnp.exp(sink - mx))
        en_scr[b] = e_new
        oo_scr[b] = _dot_nt(e_old.astype(BF16), vt.astype(BF16))
        inv_scr[b] = jnp.broadcast_to(1.0 / den, (stack, LANES))
        shift = w - dec_seq - b * dec_seq
        newk = pltpu.roll(kt_new, shift, axis=1) if shift else kt_new
        newv = pltpu.roll(vt_new, shift, axis=1) if shift else vt_new
        keep_old = lane < w - dec_seq
        sk_ref[b] = jnp.where(keep_old, pltpu.roll(kt, w - dec_seq, axis=1), newk).reshape(
            N_KV_HEADS, HEAD_DIM, w)
        sv_ref[b] = jnp.where(keep_old, pltpu.roll(vt, w - dec_seq, axis=1), newv).reshape(
            N_KV_HEADS, HEAD_DIM, w)

    o_new = _dot(en_scr[...].reshape(nseq * stack, LANES).astype(BF16), v.astype(BF16))
    o = (oo_scr[...] + o_new.reshape(nseq, stack, LANES)) * inv_scr[...]
    pairs = []
    for h in range(N_KV_HEADS):
        heads = []
        for g in range(Q_GROUP):
            s0 = (h * Q_GROUP + g) * dec_seq
            og = o[:, s0:s0 + dec_seq, :].reshape(rows, LANES)
            if g % 2 != h:
                og = pltpu.roll(og, HEAD_DIM, axis=1)
            heads.append(og)
        pairs.append(jnp.where(lo, heads[0], heads[1]))
        pairs.append(jnp.where(lo, heads[2], heads[3]))
    y_a = jnp.concatenate(pairs, axis=1)

    st = st_ref[...]

    def conv_shift(ci):
        t = lax.rem(lax.broadcasted_iota(jnp.int32, ci.shape, 0), dec_seq)
        xm1 = jnp.where(t == 0, pltpu.roll(st, rows - 1, axis=0), pltpu.roll(ci, 1, axis=0))
        xm2 = jnp.where(t < CONV_K - 1, st, pltpu.roll(ci, 2, axis=0))
        return xm1, xm2

    mr = lax.broadcasted_iota(jnp.int32, (CHUNK, CHUNK), 0)
    mc = lax.broadcasted_iota(jnp.int32, (CHUNK, CHUNK), 1)
    mix_mask = (mr // dec_seq == mc // dec_seq) & (mc <= mr)
    b_out, conv_in = _branch_b(hb, conv_shift, conv_w_ref, wob_ref)
    reps = CHUNK // dec_seq

    def mix_rows(g):
        return jnp.broadcast_to(mix_ref[g][None], (reps, dec_seq, CHUNK)).reshape(CHUNK, CHUNK)

    spb = jnp.broadcast_to(spb_ref[...][None], (reps, dec_seq, MLP_WIDTH)).reshape(CHUNK, MLP_WIDTH)
    c_out, v_c = _branch_c(hc, mix_mask, mix_rows, spb, vnw_ref, woc_ref)
    y = _merge(x, _branch_a(y_a, z_a, woa_ref), b_out, c_out, gates, wo_ref)
    x_scr[pl.ds(r0, rows), :] = y

    @pl.when(layer == pl.num_programs(0) - 1)
    def _():
        y_ref[...] = y
    ci_ref[...] = conv_in
    scv_ref[...] = v_c

    @pl.when((layer == pl.num_programs(0) - 1) & (j == pl.num_programs(1) - 1))
    def _():
        for n in range(len(resident)):
            publish(n, layer).wait()


SMALL_WEIGHT_NAMES = ("norm_w", "b_gate", "qw", "kw", "conv_w", "vnw", "mix", "spb")
BIG_WEIGHT_NAMES = ("w_in", "woa", "wob", "woc", "wo")
STAGE_ROWS_W_IN = 64
STAGE_ROWS_SQUARE = 256
WEIGHT_NAMES = ("norm_w", "w_in", "b_gate", "qw", "kw", "conv_w", "vnw", "mix", "spb",
                "woa", "wob", "woc", "wo")


def _layer_spec(arr, layer_of):
    nd = arr.ndim - 1
    return pl.BlockSpec((None,) + arr.shape[1:], lambda *g, _nd=nd: (layer_of(*g),) + (0,) * _nd,
                        pipeline_mode=pl.Buffered(1))


def _prompt_layer(x, wts, layer, batch):
    n, d = x.shape
    rows = PROMPT_ROWS
    nblk = n // batch // rows
    smem = pl.BlockSpec(memory_space=pltpu.SMEM)
    in_specs = ([smem, pl.BlockSpec((rows, d), lambda i: (i, 0))]
                + [_layer_spec(wts[name], lambda i: layer) for name in WEIGHT_NAMES])
    out_shape = (jax.ShapeDtypeStruct((n, d), F32),
                 jax.ShapeDtypeStruct((batch, WINDOW, KV_WIDTH), F32),
                 jax.ShapeDtypeStruct((batch, WINDOW, KV_WIDTH), F32),
                 jax.ShapeDtypeStruct((batch, SUBLANES, CONV_WIDTH), F32))
    out_specs = (pl.BlockSpec((rows, d), lambda i: (i, 0)),
                 pl.BlockSpec((1, WINDOW, KV_WIDTH), lambda i: (i // nblk, 0, 0)),
                 pl.BlockSpec((1, WINDOW, KV_WIDTH), lambda i: (i // nblk, 0, 0)),
                 pl.BlockSpec((1, SUBLANES, CONV_WIDTH), lambda i: (i // nblk, 0, 0)))
    n_tables = 2 * N_KV_HEADS * 2
    scratch = [pltpu.VMEM((n_tables, 2 * WINDOW, 2 * WINDOW), F32)]
    scratch += [pltpu.VMEM((WINDOW + rows, KV_WIDTH), BF16) for _ in range(4)]
    scratch += [pltpu.VMEM((SUBLANES, CONV_WIDTH), F32)]
    return pl.pallas_call(
        functools.partial(_prompt_kernel, rows=rows, nblk=nblk, layer=layer),
        grid=(n // rows,),
        in_specs=in_specs, out_specs=out_specs, out_shape=out_shape, scratch_shapes=scratch,
        compiler_params=pltpu.CompilerParams(dimension_semantics=("arbitrary",),
                                             vmem_limit_bytes=VMEM_LIMIT_BYTES),
        name="prompt_layer",
    )(wts["sinks"], x, *[wts[name] for name in WEIGHT_NAMES])


def _sample_layers(x, ck, cv, st, wts, dec_seq):
    n, d = x.shape
    depth = ck.shape[0]
    rows = SAMPLE_ROWS
    nseq = rows // dec_seq
    stack = N_HEADS * dec_seq
    w = ck.shape[-1]
    smem = pl.BlockSpec(memory_space=pltpu.SMEM)
    cache_blk = pl.BlockSpec((None, nseq, N_KV_HEADS, HEAD_DIM, w), lambda l, j: (l, j, 0, 0, 0))
    rows_blk = lambda width: pl.BlockSpec((None, rows, width), lambda l, j: (l, j, 0))
    nblk = n // rows
    x_blk = pl.BlockSpec((rows, d), lambda l, j: (jnp.where(l == 0, j, nblk - 1), 0))
    y_blk = pl.BlockSpec((rows, d), lambda l, j: (jnp.where(l == depth - 1, j, 0), 0))
    hbm = pl.BlockSpec(memory_space=pl.ANY)
    in_specs = ([smem, x_blk, cache_blk, cache_blk, rows_blk(CONV_WIDTH)]
                + [_layer_spec(wts[name], lambda l, j: l) for name in SMALL_WEIGHT_NAMES]
                + [hbm] * len(BIG_WEIGHT_NAMES))
    out_shape = (jax.ShapeDtypeStruct((n, d), F32),
                 jax.ShapeDtypeStruct(ck.shape, F32),
                 jax.ShapeDtypeStruct(cv.shape, F32),
                 jax.ShapeDtypeStruct((depth, n, CONV_WIDTH), F32),
                 jax.ShapeDtypeStruct((depth, n, MLP_WIDTH), F32)
                 ) + tuple(jax.ShapeDtypeStruct(wts[name].shape, BF16) for name in BIG_WEIGHT_NAMES)
    out_specs = ((y_blk, cache_blk, cache_blk, rows_blk(CONV_WIDTH), rows_blk(MLP_WIDTH))
                 + (hbm,) * len(BIG_WEIGHT_NAMES))
    scratch = [pltpu.VMEM((n, d), F32),
               pltpu.VMEM((stack, LANES), F32),
               pltpu.VMEM((nseq, stack, LANES), F32),
               pltpu.VMEM((nseq, stack, LANES), F32),
               pltpu.VMEM((nseq, stack, LANES), F32),
               pltpu.VMEM((nseq, stack, LANES), F32),
               pltpu.VMEM((nseq, stack, LANES), F32)]
    scratch += [pltpu.VMEM(wts[name].shape[1:], BF16) for name in BIG_WEIGHT_NAMES]
    scratch += [pltpu.VMEM((2, STAGE_ROWS_W_IN, IN_COLS), F32),
                pltpu.VMEM((2, STAGE_ROWS_SQUARE, d), F32),
                pltpu.SemaphoreType.DMA((2,)), pltpu.SemaphoreType.DMA((2,)),
                pltpu.SemaphoreType.DMA((len(BIG_WEIGHT_NAMES),))]
    return pl.pallas_call(
        functools.partial(_sample_kernel, dec_seq=dec_seq),
        grid=(depth, n // rows),
        in_specs=in_specs, out_specs=out_specs, out_shape=out_shape, scratch_shapes=scratch,
        compiler_params=pltpu.CompilerParams(dimension_semantics=("arbitrary", "arbitrary"),
                                             vmem_limit_bytes=VMEM_LIMIT_BYTES),
        name="sample_layers",
    )(wts["sinks"], x, ck, cv, st, *[wts[name] for name in SMALL_WEIGHT_NAMES + BIG_WEIGHT_NAMES])


def kernel(x_prompt, x_sample, cache_k, cache_v, state_conv, norm_w, w_in, b_gate, q_norm_w, k_norm_w,
           sinks, conv_w, v_norm_w, w_spatial, b_spatial, w_out_a, w_out_b, w_out_c, w_o):
    batch, seq, d = x_prompt.shape
    dec_batch, dec_seq, _ = x_sample.shape
    depth = w_in.shape[0]
    w_buf = cache_k.shape[2]
    assert d == D_MODEL and seq % PROMPT_ROWS == 0 and (dec_batch * dec_seq) % SAMPLE_ROWS == 0
    assert w_buf == WINDOW and dec_seq == SUBLANES and w_in.shape[2] == IN_COLS

    gw = MLP_WIDTH // N_SPATIAL_GROUPS
    reps = CHUNK // dec_seq
    spb_p = jnp.repeat(jnp.swapaxes(b_spatial, 1, 2), gw, axis=2)
    common = {
        "sinks": sinks,
        "norm_w": norm_w[:, None, :], "w_in": w_in, "b_gate": b_gate[:, None, :],
        "qw": jnp.tile(q_norm_w, (1, LANES // HEAD_DIM))[:, None, :],
        "kw": jnp.tile(k_norm_w, (1, LANES // HEAD_DIM))[:, None, :],
        "conv_w": conv_w, "vnw": v_norm_w[:, None, :],
        "woa": w_out_a, "wob": w_out_b, "woc": w_out_c, "wo": w_o,
    }
    wts_s = dict(common,
                 mix=jnp.tile(w_spatial[:, :, :dec_seq, :dec_seq], (1, 1, 1, reps)),
                 spb=spb_p[:, :dec_seq, :])

    st_rows = jnp.pad(state_conv, ((0, 0), (0, 0), (0, dec_seq - (CONV_K - 1)), (0, 0)))
    st_rows = st_rows.reshape(depth, dec_batch * dec_seq, CONV_WIDTH)
    ck = jnp.transpose(cache_k, (0, 1, 3, 4, 2))
    cv = jnp.transpose(cache_v, (0, 1, 3, 4, 2))

    ys, sk, sv, ci, scv, *big_bf16 = _sample_layers(x_sample.reshape(dec_batch * dec_seq, d), ck, cv,
                                                    st_rows, wts_s, dec_seq)
    wts_p = dict(common, mix=w_spatial, spb=spb_p, **dict(zip(BIG_WEIGHT_NAMES, big_bf16)))
    sk = jnp.transpose(sk, (0, 1, 4, 2, 3))
    sv = jnp.transpose(sv, (0, 1, 4, 2, 3))
    sc = ci.reshape(depth, dec_batch, dec_seq, CONV_WIDTH)[:, :, dec_seq - (CONV_K - 1):, :]
    scv = scv.reshape(depth, dec_batch, dec_seq, MLP_WIDTH)

    yp = x_prompt.reshape(batch * seq, d)
    pk, pv, pc = [], [], []
    for l in range(depth):
        yp, k_l, v_l, c_l = _prompt_layer(yp, wts_p, l, batch)
        pk.append(k_l.reshape(batch, WINDOW, N_KV_HEADS, HEAD_DIM))
        pv.append(v_l.reshape(batch, WINDOW, N_KV_HEADS, HEAD_DIM))
        pc.append(c_l[:, SUBLANES - (CONV_K - 1):, :])

    return (yp.reshape(batch, seq, d), ys.reshape(dec_batch, dec_seq, d), jnp.stack(pk), jnp.stack(pv),
            jnp.stack(pc), sk, sv, sc, scv)
```

```python
import functools

import jax
import jax.numpy as jnp
from jax import lax
from jax.experimental import pallas as pl
from jax.experimental.pallas import tpu as pltpu

F32 = jnp.float32
BF16 = jnp.bfloat16

D_MODEL = 1024
N_HEADS = 8
N_KV_HEADS = 2
HEAD_DIM = 64
Q_GROUP = N_HEADS // N_KV_HEADS
ATTN_WIDTH = N_HEADS * HEAD_DIM
KV_WIDTH = N_KV_HEADS * HEAD_DIM
WINDOW = 128
CONV_WIDTH = 512
CONV_K = 3
CHUNK = 128
MLP_WIDTH = 512
N_SPATIAL_GROUPS = 4
EPS = 1e-6
NEG_INF = -1e30

COL_A = 0
COL_B = COL_A + 2 * ATTN_WIDTH + 2 * KV_WIDTH
COL_C = COL_B + 4 * CONV_WIDTH
COL_G = COL_C + 3 * MLP_WIDTH
IN_COLS = COL_G + 3 * D_MODEL

LANES = 128
SUBLANES = 8
PROMPT_ROWS = 512
SAMPLE_ROWS = 128
VMEM_LIMIT_BYTES = 56 * 1024 * 1024


def _dot(a, b):
    return jnp.dot(a, b, preferred_element_type=F32)


def _dot_nt(a, b):
    return lax.dot_general(a, b, (((1,), (1,)), ((), ())), preferred_element_type=F32)


def _rms(x, w):
    ms = jnp.mean(x * x, axis=-1, keepdims=True)
    return (x * lax.rsqrt(ms + EPS)) * w


def _lo_lanes(shape):
    return lax.broadcasted_iota(jnp.int32, shape, len(shape) - 1) < HEAD_DIM


def _pair_rms(x, w):
    lo = _lo_lanes(x.shape)
    sq = x * x
    s_lo = jnp.sum(jnp.where(lo, sq, 0.0), axis=-1, keepdims=True)
    s_hi = jnp.sum(jnp.where(lo, 0.0, sq), axis=-1, keepdims=True)
    ms = jnp.where(lo, s_lo, s_hi) * (1.0 / HEAD_DIM)
    return (x * lax.rsqrt(ms + EPS)) * w


def _norm_heads(x, w):
    groups = [_pair_rms(x[:, g * LANES:(g + 1) * LANES], w) for g in range(x.shape[1] // LANES)]
    return groups[0] if len(groups) == 1 else jnp.concatenate(groups, axis=1)


def _slope(head):
    return 2.0 ** (-(head + 1))


def _branch_b(hb, conv_shift, conv_w_ref, wob_ref):
    gate_b = hb[:, 0:CONV_WIDTH]
    gate_c = hb[:, CONV_WIDTH:2 * CONV_WIDTH]
    h_b = hb[:, 2 * CONV_WIDTH:3 * CONV_WIDTH]
    z_b = hb[:, 3 * CONV_WIDTH:4 * CONV_WIDTH]
    conv_in = gate_c * h_b
    xm1, xm2 = conv_shift(conv_in)
    cw = conv_w_ref[...]
    conv_out = cw[0:1] * xm2 + cw[1:2] * xm1 + cw[2:3] * conv_in
    y_b = gate_b * conv_out
    return _dot((jax.nn.silu(z_b) * y_b).astype(BF16), wob_ref[...]), conv_in


def _branch_c(hc, mix_mask, mix_rows, spb, vnw_ref, woc_ref):
    rows = hc.shape[0]
    u = hc[:, 0:MLP_WIDTH]
    v_c = _rms(hc[:, MLP_WIDTH:2 * MLP_WIDTH], vnw_ref[...])
    z_c = hc[:, 2 * MLP_WIDTH:3 * MLP_WIDTH]
    vb = v_c.astype(BF16)
    gw = MLP_WIDTH // N_SPATIAL_GROUPS
    mixes = [jnp.where(mix_mask, mix_rows(g), 0.0).astype(BF16) for g in range(N_SPATIAL_GROUPS)]
    sp_chunks = []
    for c in range(rows // CHUNK):
        r0 = c * CHUNK
        parts = [_dot(mixes[g], vb[r0:r0 + CHUNK, g * gw:(g + 1) * gw])
                 for g in range(N_SPATIAL_GROUPS)]
        sp_chunks.append(jnp.concatenate(parts, axis=1) + spb)
    sp = sp_chunks[0] if len(sp_chunks) == 1 else jnp.concatenate(sp_chunks, axis=0)
    y_c = u * sp
    return _dot((jax.nn.silu(z_c) * y_c).astype(BF16), woc_ref[...]), v_c


def _branch_a(y_a, z_a, woa_ref):
    return _dot((jax.nn.silu(z_a) * y_a).astype(BF16), woa_ref[...])


def _merge(x, a, b, c_out, gates, wo_ref):
    m = gates[0] * a + gates[1] * b + gates[2] * c_out
    return x + _dot(m.astype(BF16), wo_ref[...])


def _prompt_kernel(sinks_ref, x_ref, norm_w_ref, w_in_ref, b_gate_ref, qw_ref, kw_ref, conv_w_ref,
                   vnw_ref, mix_ref, spb_ref, woa_ref, wob_ref, woc_ref, wo_ref,
                   y_ref, pk_ref, pv_ref, pc_ref,
                   bias_scr, ka_scr, kb_scr, va_scr, vb_scr, carry_scr, *, rows, nblk, layer):
    i = pl.program_id(0)
    first = lax.rem(i, nblk) == 0
    nsub = rows // WINDOW
    stack = 2 * WINDOW

    @pl.when(i == 0)
    def _():
        r = lax.broadcasted_iota(jnp.int32, (stack, 2 * WINDOW), 0)
        col = lax.broadcasted_iota(jnp.int32, (stack, 2 * WINDOW), 1)
        top = r < WINDOW
        dist = jnp.where(top, r, r - WINDOW) + WINDOW - col
        band = (dist >= 0) & (dist < WINDOW)
        distf = dist.astype(F32)
        for flag in range(2):
            valid = band & (col >= WINDOW) if flag else band
            for h in range(N_KV_HEADS):
                for half in range(2):
                    slope = jnp.where(top, _slope(h * Q_GROUP + half), _slope(h * Q_GROUP + 2 + half))
                    bias_scr[flag * 4 + h * 2 + half] = jnp.where(valid, -(slope * distf), NEG_INF)

    @pl.when(first)
    def _():
        zeros = jnp.zeros((WINDOW, KV_WIDTH), BF16)
        ka_scr[0:WINDOW, :] = zeros
        kb_scr[0:WINDOW, :] = zeros
        va_scr[0:WINDOW, :] = zeros
        vb_scr[0:WINDOW, :] = zeros
        carry_scr[...] = jnp.zeros(carry_scr.shape, F32)

    x = x_ref[...]
    xn = _rms(x, norm_w_ref[...]).astype(BF16)
    def proj(c0, c1):
        return _dot(xn, w_in_ref[:, c0:c1])

    def gate(idx):
        g = proj(COL_G + idx * D_MODEL, COL_G + (idx + 1) * D_MODEL)
        return jax.nn.sigmoid(g + b_gate_ref[:, idx * D_MODEL:(idx + 1) * D_MODEL])

    ha = proj(COL_A, COL_B)
    hb = proj(COL_B, COL_C)
    q = _norm_heads(ha[:, 0:ATTN_WIDTH], qw_ref[...]) * (HEAD_DIM ** -0.5)
    k = _pair_rms(ha[:, ATTN_WIDTH:ATTN_WIDTH + KV_WIDTH], kw_ref[...])
    v = ha[:, ATTN_WIDTH + KV_WIDTH:ATTN_WIDTH + 2 * KV_WIDTH]
    z_a = ha[:, ATTN_WIDTH + 2 * KV_WIDTH:COL_B]

    pk_ref[0] = k[rows - WINDOW:rows, :]
    pv_ref[0] = v[rows - WINDOW:rows, :]
    ka_scr[WINDOW:WINDOW + rows, :] = k.astype(BF16)
    kb_scr[WINDOW:WINDOW + rows, :] = pltpu.roll(k, HEAD_DIM, axis=1).astype(BF16)
    va_scr[WINDOW:WINDOW + rows, :] = v.astype(BF16)
    vb_scr[WINDOW:WINDOW + rows, :] = pltpu.roll(v, HEAD_DIM, axis=1).astype(BF16)

    lo = _lo_lanes((WINDOW, LANES))
    srow = lax.broadcasted_iota(jnp.int32, (stack, 1), 0) < WINDOW
    flag = first.astype(jnp.int32)
    carry = carry_scr[...]
    prev1 = carry[SUBLANES - 1:SUBLANES, :]
    prev2 = carry[SUBLANES - 2:SUBLANES - 1, :]

    def conv_shift(ci):
        rid = lax.broadcasted_iota(jnp.int32, ci.shape, 0)
        xm1 = jnp.where(rid == 0, prev1, pltpu.roll(ci, 1, axis=0))
        xm2 = jnp.where(rid == 0, prev2, jnp.where(rid == 1, prev1, pltpu.roll(ci, 2, axis=0)))
        return xm1, xm2

    mr = lax.broadcasted_iota(jnp.int32, (CHUNK, CHUNK), 0)
    mc = lax.broadcasted_iota(jnp.int32, (CHUNK, CHUNK), 1)
    done = {}
    stages = [
        lambda: done.update(hc=proj(COL_C, COL_G)),
        lambda: done.update(g0=gate(0)),
        lambda: done.update(g1=gate(1)),
        lambda: done.update(g2=gate(2)),
    ]
    slots = 2 * nsub
    order = list(range(0, slots, 2)) + list(range(1, slots, 2))
    per_slot = [0] * slots
    for n in range(len(stages)):
        per_slot[order[n % slots]] += 1

    def run_stages(slot):
        for _ in range(per_slot[slot]):
            stages.pop(0)()

    y_rows = []
    for c in range(nsub):
        r0 = c * WINDOW
        keys = slice(r0, r0 + 2 * WINDOW)
        k_nat, k_swp = ka_scr[keys, :], kb_scr[keys, :]
        v_nat, v_swp = va_scr[keys, :], vb_scr[keys, :]
        scores = []
        for h in range(N_KV_HEADS):
            p0 = q[r0:r0 + WINDOW, (2 * h) * LANES:(2 * h + 1) * LANES]
            p1 = q[r0:r0 + WINDOW, (2 * h + 1) * LANES:(2 * h + 2) * LANES]
            for half in range(2):
                keep = lo if half == 0 else jnp.logical_not(lo)
                qs = jnp.concatenate([jnp.where(keep, p0, 0.0), jnp.where(keep, p1, 0.0)],
                                     axis=0).astype(BF16)
                kk = k_nat if h == half else k_swp
                tbl = h * 2 + half
                bias = bias_scr[flag * 4 + tbl] if c == 0 else bias_scr[tbl]
                scores.append(_dot_nt(qs, kk) + bias)
        run_stages(2 * c)
        pairs = []
        for h in range(N_KV_HEADS):
            outs = []
            for half in range(2):
                s = scores[h * 2 + half]
                vv = v_nat if h == half else v_swp
                sink = jnp.where(srow, sinks_ref[layer, h * Q_GROUP + half],
                                 sinks_ref[layer, h * Q_GROUP + 2 + half])
                mx = jnp.maximum(jnp.max(s, axis=-1, keepdims=True), sink)
                e = jnp.exp(s - mx)
                den = jnp.sum(e, axis=-1, keepdims=True) + jnp.exp(sink - mx)
                outs.append(_dot(e.astype(BF16), vv) * (1.0 / den))
            pairs.append(jnp.where(lo, outs[0][0:WINDOW], outs[1][0:WINDOW]))
            pairs.append(jnp.where(lo, outs[0][WINDOW:stack], outs[1][WINDOW:stack]))
        y_rows.append(jnp.concatenate(pairs, axis=1))
        run_stages(2 * c + 1)
    y_a = y_rows[0] if nsub == 1 else jnp.concatenate(y_rows, axis=0)

    ka_scr[0:WINDOW, :] = ka_scr[rows:rows + WINDOW, :]
    kb_scr[0:WINDOW, :] = kb_scr[rows:rows + WINDOW, :]
    va_scr[0:WINDOW, :] = va_scr[rows:rows + WINDOW, :]
    vb_scr[0:WINDOW, :] = vb_scr[rows:rows + WINDOW, :]

    a_out = _branch_a(y_a, z_a, woa_ref)
    b_out, conv_in = _branch_b(hb, conv_shift, conv_w_ref, wob_ref)
    c_out, _ = _branch_c(done["hc"], mc <= mr, lambda g: mix_ref[g], spb_ref[...], vnw_ref, woc_ref)
    y_ref[...] = _merge(x, a_out, b_out, c_out, [done["g0"], done["g1"], done["g2"]], wo_ref)
    last = conv_in[rows - SUBLANES:rows, :]
    carry_scr[...] = last
    pc_ref[0] = last


def _stage_bf16(src_hbm, dst_scr, stage, sem, layer):
    nslots, chunk_rows = stage.shape[0], stage.shape[1]
    nchunks = dst_scr.shape[0] // chunk_rows
    ahead = nslots - 1

    def copy(k):
        return pltpu.make_async_copy(src_hbm.at[layer, pl.ds(k * chunk_rows, chunk_rows), :],
                                     stage.at[k % nslots], sem.at[k % nslots])

    for k in range(min(ahead, nchunks)):
        copy(k).start()
    for k in range(nchunks):
        if k + ahead < nchunks:
            copy(k + ahead).start()
        copy(k).wait()
        dst_scr[k * chunk_rows:(k + 1) * chunk_rows, :] = stage[k % nslots].astype(BF16)


def _sample_kernel(sinks_ref, x_ref, ck_ref, cv_ref, st_ref, norm_w_ref, b_gate_ref, qw_ref,
                   kw_ref, conv_w_ref, vnw_ref, mix_ref, spb_ref,
                   w_in_hbm, woa_hbm, wob_hbm, woc_hbm, wo_hbm,
                   y_ref, sk_ref, sv_ref, ci_ref, scv_ref,
                   w_in_out, woa_out, wob_out, woc_out, wo_out,
                   x_scr, bold_scr, bnew_scr, q_scr, en_scr, oo_scr, inv_scr,
                   w_in_ref, woa_ref, wob_ref, woc_ref, wo_ref, stage_in, stage_sq,
                   sem_in, sem_sq, sem_out, *, dec_seq):
    layer = pl.program_id(0)
    j = pl.program_id(1)
    rows = SAMPLE_ROWS
    nseq = rows // dec_seq
    stack = N_HEADS * dec_seq
    w = WINDOW

    @pl.when((layer == 0) & (j == 0))
    def _():
        r = lax.broadcasted_iota(jnp.int32, (stack, LANES), 0)
        col = lax.broadcasted_iota(jnp.int32, (stack, LANES), 1)
        tok = lax.rem(r, dec_seq)
        head = r // dec_seq
        slope = jnp.zeros((stack, LANES), F32)
        for n in range(N_HEADS):
            slope = jnp.where(head == n, _slope(n), slope)
        dist = tok + w - col
        bold_scr[...] = jnp.where(dist < WINDOW, -(slope * dist.astype(F32)), NEG_INF)
        kseq = col // dec_seq
        dist = tok - lax.rem(col, dec_seq)
        pen = -(slope * dist.astype(F32))
        for b in range(nseq):
            bnew_scr[b] = jnp.where((kseq == b) & (dist >= 0), pen, NEG_INF)

    r0 = pl.multiple_of(j * rows, rows)

    resident = ((w_in_hbm, w_in_ref, w_in_out), (woa_hbm, woa_ref, woa_out), (wob_hbm, wob_ref, wob_out),
                (woc_hbm, woc_ref, woc_out), (wo_hbm, wo_ref, wo_out))

    def publish(n, lyr):
        return pltpu.make_async_copy(resident[n][1], resident[n][2].at[lyr], sem_out.at[n])

    @pl.when(j == 0)
    def _():
        @pl.when(layer > 0)
        def _():
            for n in range(len(resident)):
                publish(n, layer - 1).wait()

        _stage_bf16(w_in_hbm, w_in_ref, stage_in, sem_in, layer)
        for src, dst, _ in resident[1:]:
            _stage_bf16(src, dst, stage_sq, sem_sq, layer)
        for n in range(len(resident)):
            publish(n, layer).start()

    @pl.when(layer == 0)
    def _():
        x_scr[pl.ds(r0, rows), :] = x_ref[...]

    x = x_scr[pl.ds(r0, rows), :]
    xn = _rms(x, norm_w_ref[...]).astype(BF16)
    def proj(c0, c1):
        return _dot(xn, w_in_ref[:, c0:c1])

    def gate(idx):
        g = proj(COL_G + idx * D_MODEL, COL_G + (idx + 1) * D_MODEL)
        return jax.nn.sigmoid(g + b_gate_ref[:, idx * D_MODEL:(idx + 1) * D_MODEL])

    ha = proj(COL_A, COL_B)
    hb = proj(COL_B, COL_C)
    q = _norm_heads(ha[:, 0:ATTN_WIDTH], qw_ref[...]) * (HEAD_DIM ** -0.5)
    k = _pair_rms(ha[:, ATTN_WIDTH:ATTN_WIDTH + KV_WIDTH], kw_ref[...])
    v = ha[:, ATTN_WIDTH + KV_WIDTH:ATTN_WIDTH + 2 * KV_WIDTH]
    z_a = ha[:, ATTN_WIDTH + 2 * KV_WIDTH:COL_B]

    lo = _lo_lanes((rows, LANES))
    for h in range(N_KV_HEADS):
        keep = lo if h == 0 else jnp.logical_not(lo)
        for g in range(Q_GROUP):
            pair = q[:, (h * 2 + g // 2) * LANES:(h * 2 + g // 2 + 1) * LANES]
            if g % 2 != h:
                pair = pltpu.roll(pair, HEAD_DIM, axis=1)
            piece = jnp.where(keep, pair, 0.0).reshape(nseq, dec_seq, LANES)
            s0 = (h * Q_GROUP + g) * dec_seq
            q_scr[:, s0:s0 + dec_seq, :] = piece

    kt_new = jnp.transpose(k)
    vt_new = jnp.transpose(v)
    qall = q_scr[...].reshape(nseq * stack, LANES).astype(BF16)
    s_new_all = _dot(qall, kt_new.astype(BF16)).reshape(nseq, stack, LANES)

    srow = lax.broadcasted_iota(jnp.int32, (stack, 1), 0) // dec_seq
    sink = jnp.zeros((stack, 1), F32)
    for n in range(N_HEADS):
        sink = jnp.where(srow == n, sinks_ref[layer, n], sink)
    bias_old = bold_scr[...]
    lane = lax.broadcasted_iota(jnp.int32, (KV_WIDTH, w), 1)
    s_olds = [_dot(q_scr[b].astype(BF16), ck_ref[b].reshape(KV_WIDTH, w).astype(BF16)) + bias_old
              for b in range(nseq)]
    hc = proj(COL_C, COL_G)
    gates = []
    for b in range(nseq):
        if b % (nseq // 4) == 0 and 0 < b:
            gates.append(gate(len(gates)))
        kt = ck_ref[b].reshape(KV_WIDTH, w)
        vt = cv_ref[b].reshape(KV_WIDTH, w)
        s_old = s_olds[b]
        s_new = s_new_all[b] + bnew_scr[b]
        mx = jnp.maximum(jnp.maximum(jnp.max(s_old, axis=-1, keepdims=True),
                                     jnp.max(s_new, axis=-1, keepdims=True)), sink)
        e_old = jnp.exp(s_old - mx)
        e_new = jnp.exp(s_new - mx)
        den = (jnp.sum(e_old, axis=-1, keepdims=True) + jnp.sum(e_new, axis=-1, keepdims=True)
               + jnp.exp(sink - mx))
        en_scr[b] = e_new
        oo_scr[b] = _dot_nt(e_old.astype(BF16), vt.astype(BF16))
        inv_scr[b] = jnp.broadcast_to(1.0 / den, (stack, LANES))
        shift = w - dec_seq - b * dec_seq
        newk = pltpu.roll(kt_new, shift, axis=1) if shift else kt_new
        newv = pltpu.roll(vt_new, shift, axis=1) if shift else vt_new
        keep_old = lane < w - dec_seq
        sk_ref[b] = jnp.where(keep_old, pltpu.roll(kt, w - dec_seq, axis=1), newk).reshape(
            N_KV_HEADS, HEAD_DIM, w)
        sv_ref[b] = jnp.where(keep_old, pltpu.roll(vt, w - dec_seq, axis=1), newv).reshape(
            N_KV_HEADS, HEAD_DIM, w)

    o_new = _dot(en_scr[...].reshape(nseq * stack, LANES).astype(BF16), v.astype(BF16))
    o = (oo_scr[...] + o_new.reshape(nseq, stack, LANES)) * inv_scr[...]
    pairs = []
    for h in range(N_KV_HEADS):
        heads = []
        for g in range(Q_GROUP):
            s0 = (h * Q_GROUP + g) * dec_seq
            og = o[:, s0:s0 + dec_seq, :].reshape(rows, LANES)
            if g % 2 != h:
                og = pltpu.roll(og, HEAD_DIM, axis=1)
            heads.append(og)
        pairs.append(jnp.where(lo, heads[0], heads[1]))
        pairs.append(jnp.where(lo, heads[2], heads[3]))
    y_a = jnp.concatenate(pairs, axis=1)

    st = st_ref[...]

    def conv_shift(ci):
        t = lax.rem(lax.broadcasted_iota(jnp.int32, ci.shape, 0), dec_seq)
        xm1 = jnp.where(t == 0, pltpu.roll(st, rows - 1, axis=0), pltpu.roll(ci, 1, axis=0))
        xm2 = jnp.where(t < CONV_K - 1, st, pltpu.roll(ci, 2, axis=0))
        return xm1, xm2

    mr = lax.broadcasted_iota(jnp.int32, (CHUNK, CHUNK), 0)
    mc = lax.broadcasted_iota(jnp.int32, (CHUNK, CHUNK), 1)
    mix_mask = (mr // dec_seq == mc // dec_seq) & (mc <= mr)
    b_out, conv_in = _branch_b(hb, conv_shift, conv_w_ref, wob_ref)
    reps = CHUNK // dec_seq

    def mix_rows(g):
        return jnp.broadcast_to(mix_ref[g][None], (reps, dec_seq, CHUNK)).reshape(CHUNK, CHUNK)

    spb = jnp.broadcast_to(spb_ref[...][None], (reps, dec_seq, MLP_WIDTH)).reshape(CHUNK, MLP_WIDTH)
    c_out, v_c = _branch_c(hc, mix_mask, mix_rows, spb, vnw_ref, woc_ref)
    y = _merge(x, _branch_a(y_a, z_a, woa_ref), b_out, c_out, gates, wo_ref)
    x_scr[pl.ds(r0, rows), :] = y

    @pl.when(layer == pl.num_programs(0) - 1)
    def _():
        y_ref[...] = y
    ci_ref[...] = conv_in
    scv_ref[...] = v_c

    @pl.when((layer == pl.num_programs(0) - 1) & (j == pl.num_programs(1) - 1))
    def _():
        for n in range(len(resident)):
            publish(n, layer).wait()


SMALL_WEIGHT_NAMES = ("norm_w", "b_gate", "qw", "kw", "conv_w", "vnw", "mix", "spb")
BIG_WEIGHT_NAMES = ("w_in", "woa", "wob", "woc", "wo")
STAGE_SLOTS = 4
STAGE_ROWS_W_IN = 32
STAGE_ROWS_SQUARE = 128
WEIGHT_NAMES = ("norm_w", "w_in", "b_gate", "qw", "kw", "conv_w", "vnw", "mix", "spb",
                "woa", "wob", "woc", "wo")


def _layer_spec(arr, layer_of):
    nd = arr.ndim - 1
    return pl.BlockSpec((None,) + arr.shape[1:], lambda *g, _nd=nd: (layer_of(*g),) + (0,) * _nd,
                        pipeline_mode=pl.Buffered(1))


def _prompt_layer(x, wts, layer, batch):
    n, d = x.shape
    rows = PROMPT_ROWS
    nblk = n // batch // rows
    smem = pl.BlockSpec(memory_space=pltpu.SMEM)
    in_specs = ([smem, pl.BlockSpec((rows, d), lambda i: (i, 0))]
                + [_layer_spec(wts[name], lambda i: layer) for name in WEIGHT_NAMES])
    out_shape = (jax.ShapeDtypeStruct((n, d), F32),
                 jax.ShapeDtypeStruct((batch, WINDOW, KV_WIDTH), F32),
                 jax.ShapeDtypeStruct((batch, WINDOW, KV_WIDTH), F32),
                 jax.ShapeDtypeStruct((batch, SUBLANES, CONV_WIDTH), F32))
    out_specs = (pl.BlockSpec((rows, d), lambda i: (i, 0)),
                 pl.BlockSpec((1, WINDOW, KV_WIDTH), lambda i: (i // nblk, 0, 0)),
                 pl.BlockSpec((1, WINDOW, KV_WIDTH), lambda i: (i // nblk, 0, 0)),
                 pl.BlockSpec((1, SUBLANES, CONV_WIDTH), lambda i: (i // nblk, 0, 0)))
    n_tables = 2 * N_KV_HEADS * 2
    scratch = [pltpu.VMEM((n_tables, 2 * WINDOW, 2 * WINDOW), F32)]
    scratch += [pltpu.VMEM((WINDOW + rows, KV_WIDTH), BF16) for _ in range(4)]
    scratch += [pltpu.VMEM((SUBLANES, CONV_WIDTH), F32)]
    return pl.pallas_call(
        functools.partial(_prompt_kernel, rows=rows, nblk=nblk, layer=layer),
        grid=(n // rows,),
        in_specs=in_specs, out_specs=out_specs, out_shape=out_shape, scratch_shapes=scratch,
        compiler_params=pltpu.CompilerParams(dimension_semantics=("arbitrary",),
                                             vmem_limit_bytes=VMEM_LIMIT_BYTES),
        name="prompt_layer",
    )(wts["sinks"], x, *[wts[name] for name in WEIGHT_NAMES])


def _sample_layers(x, ck, cv, st, wts, dec_seq):
    n, d = x.shape
    depth = ck.shape[0]
    rows = SAMPLE_ROWS
    nseq = rows // dec_seq
    stack = N_HEADS * dec_seq
    w = ck.shape[-1]
    smem = pl.BlockSpec(memory_space=pltpu.SMEM)
    cache_blk = pl.BlockSpec((None, nseq, N_KV_HEADS, HEAD_DIM, w), lambda l, j: (l, j, 0, 0, 0))
    rows_blk = lambda width: pl.BlockSpec((None, rows, width), lambda l, j: (l, j, 0))
    nblk = n // rows
    x_blk = pl.BlockSpec((rows, d), lambda l, j: (jnp.where(l == 0, j, nblk - 1), 0))
    y_blk = pl.BlockSpec((rows, d), lambda l, j: (jnp.where(l == depth - 1, j, 0), 0))
    hbm = pl.BlockSpec(memory_space=pl.ANY)
    in_specs = ([smem, x_blk, cache_blk, cache_blk, rows_blk(CONV_WIDTH)]
                + [_layer_spec(wts[name], lambda l, j: l) for name in SMALL_WEIGHT_NAMES]
                + [hbm] * len(BIG_WEIGHT_NAMES))
    out_shape = (jax.ShapeDtypeStruct((n, d), F32),
                 jax.ShapeDtypeStruct(ck.shape, F32),
                 jax.ShapeDtypeStruct(cv.shape, F32),
                 jax.ShapeDtypeStruct((depth, n, CONV_WIDTH), F32),
                 jax.ShapeDtypeStruct((depth, n, MLP_WIDTH), F32)
                 ) + tuple(jax.ShapeDtypeStruct(wts[name].shape, BF16) for name in BIG_WEIGHT_NAMES)
    out_specs = ((y_blk, cache_blk, cache_blk, rows_blk(CONV_WIDTH), rows_blk(MLP_WIDTH))
                 + (hbm,) * len(BIG_WEIGHT_NAMES))
    scratch = [pltpu.VMEM((n, d), F32),
               pltpu.VMEM((stack, LANES), F32),
               pltpu.VMEM((nseq, stack, LANES), F32),
               pltpu.VMEM((nseq, stack, LANES), F32),
               pltpu.VMEM((nseq, stack, LANES), F32),
               pltpu.VMEM((nseq, stack, LANES), F32),
               pltpu.VMEM((nseq, stack, LANES), F32)]
    scratch += [pltpu.VMEM(wts[name].shape[1:], BF16) for name in BIG_WEIGHT_NAMES]
    scratch += [pltpu.VMEM((STAGE_SLOTS, STAGE_ROWS_W_IN, IN_COLS), F32),
                pltpu.VMEM((STAGE_SLOTS, STAGE_ROWS_SQUARE, d), F32),
                pltpu.SemaphoreType.DMA((STAGE_SLOTS,)), pltpu.SemaphoreType.DMA((STAGE_SLOTS,)),
                pltpu.SemaphoreType.DMA((len(BIG_WEIGHT_NAMES),))]
    return pl.pallas_call(
        functools.partial(_sample_kernel, dec_seq=dec_seq),
        grid=(depth, n // rows),
        in_specs=in_specs, out_specs=out_specs, out_shape=out_shape, scratch_shapes=scratch,
        compiler_params=pltpu.CompilerParams(dimension_semantics=("arbitrary", "arbitrary"),
                                             vmem_limit_bytes=VMEM_LIMIT_BYTES),
        name="sample_layers",
    )(wts["sinks"], x, ck, cv, st, *[wts[name] for name in SMALL_WEIGHT_NAMES + BIG_WEIGHT_NAMES])


def kernel(x_prompt, x_sample, cache_k, cache_v, state_conv, norm_w, w_in, b_gate, q_norm_w, k_norm_w,
           sinks, conv_w, v_norm_w, w_spatial, b_spatial, w_out_a, w_out_b, w_out_c, w_o):
    batch, seq, d = x_prompt.shape
    dec_batch, dec_seq, _ = x_sample.shape
    depth = w_in.shape[0]
    w_buf = cache_k.shape[2]
    assert d == D_MODEL and seq % PROMPT_ROWS == 0 and (dec_batch * dec_seq) % SAMPLE_ROWS == 0
    assert w_buf == WINDOW and dec_seq == SUBLANES and w_in.shape[2] == IN_COLS

    gw = MLP_WIDTH // N_SPATIAL_GROUPS
    reps = CHUNK // dec_seq
    spb_p = jnp.repeat(jnp.swapaxes(b_spatial, 1, 2), gw, axis=2)
    common = {
        "sinks": sinks,
        "norm_w": norm_w[:, None, :], "w_in": w_in, "b_gate": b_gate[:, None, :],
        "qw": jnp.tile(q_norm_w, (1, LANES // HEAD_DIM))[:, None, :],
        "kw": jnp.tile(k_norm_w, (1, LANES // HEAD_DIM))[:, None, :],
        "conv_w": conv_w, "vnw": v_norm_w[:, None, :],
        "woa": w_out_a, "wob": w_out_b, "woc": w_out_c, "wo": w_o,
    }
    wts_s = dict(common,
                 mix=jnp.tile(w_spatial[:, :, :dec_seq, :dec_seq], (1, 1, 1, reps)),
                 spb=spb_p[:, :dec_seq, :])

    st_rows = jnp.pad(state_conv, ((0, 0), (0, 0), (0, dec_seq - (CONV_K - 1)), (0, 0)))
    st_rows = st_rows.reshape(depth, dec_batch * dec_seq, CONV_WIDTH)
    ck = jnp.transpose(cache_k, (0, 1, 3, 4, 2))
    cv = jnp.transpose(cache_v, (0, 1, 3, 4, 2))

    ys, sk, sv, ci, scv, *big_bf16 = _sample_layers(x_sample.reshape(dec_batch * dec_seq, d), ck, cv,
                                                    st_rows, wts_s, dec_seq)
    wts_p = dict(common, mix=w_spatial, spb=spb_p, **dict(zip(BIG_WEIGHT_NAMES, big_bf16)))
    sk = jnp.transpose(sk, (0, 1, 4, 2, 3))
    sv = jnp.transpose(sv, (0, 1, 4, 2, 3))
    sc = ci.reshape(depth, dec_batch, dec_seq, CONV_WIDTH)[:, :, dec_seq - (CONV_K - 1):, :]
    scv = scv.reshape(depth, dec_batch, dec_seq, MLP_WIDTH)

    yp = x_prompt.reshape(batch * seq, d)
    pk, pv, pc = [], [], []
    for l in range(depth):
        yp, k_l, v_l, c_l = _prompt_layer(yp, wts_p, l, batch)
        pk.append(k_l.reshape(batch, WINDOW, N_KV_HEADS, HEAD_DIM))
        pv.append(v_l.reshape(batch, WINDOW, N_KV_HEADS, HEAD_DIM))
        pc.append(c_l[:, SUBLANES - (CONV_K - 1):, :])

    return (yp.reshape(batch, seq, d), ys.reshape(dec_batch, dec_seq, d), jnp.stack(pk), jnp.stack(pv),
            jnp.stack(pc), sk, sv, sc, scv)
```

```python
import functools

import jax
import jax.numpy as jnp
from jax import lax
from jax.experimental import pallas as pl
from jax.experimental.pallas import tpu as pltpu

F32 = jnp.float32
BF16 = jnp.bfloat16

D_MODEL = 1024
N_HEADS = 8
N_KV_HEADS = 2
HEAD_DIM = 64
Q_GROUP = N_HEADS // N_KV_HEADS
ATTN_WIDTH = N_HEADS * HEAD_DIM
KV_WIDTH = N_KV_HEADS * HEAD_DIM
WINDOW = 128
CONV_WIDTH = 512
CONV_K = 3
CHUNK = 128
MLP_WIDTH = 512
N_SPATIAL_GROUPS = 4
EPS = 1e-6
NEG_INF = -1e30

COL_A = 0
COL_B = COL_A + 2 * ATTN_WIDTH + 2 * KV_WIDTH
COL_C = COL_B + 4 * CONV_WIDTH
COL_G = COL_C + 3 * MLP_WIDTH
IN_COLS = COL_G + 3 * D_MODEL

LANES = 128
SUBLANES = 8
PROMPT_ROWS = 512
SAMPLE_ROWS = 128
VMEM_LIMIT_BYTES = 56 * 1024 * 1024


def _dot(a, b):
    return jnp.dot(a, b, preferred_element_type=F32)


def _dot_nt(a, b):
    return lax.dot_general(a, b, (((1,), (1,)), ((), ())), preferred_element_type=F32)


def _rms(x, w):
    ms = jnp.mean(x * x, axis=-1, keepdims=True)
    return (x * lax.rsqrt(ms + EPS)) * w


def _lo_lanes(shape):
    return lax.broadcasted_iota(jnp.int32, shape, len(shape) - 1) < HEAD_DIM


def _pair_rms(x, w):
    lo = _lo_lanes(x.shape)
    sq = x * x
    s_lo = jnp.sum(jnp.where(lo, sq, 0.0), axis=-1, keepdims=True)
    s_hi = jnp.sum(jnp.where(lo, 0.0, sq), axis=-1, keepdims=True)
    ms = jnp.where(lo, s_lo, s_hi) * (1.0 / HEAD_DIM)
    return (x * lax.rsqrt(ms + EPS)) * w


def _norm_heads(x, w):
    groups = [_pair_rms(x[:, g * LANES:(g + 1) * LANES], w) for g in range(x.shape[1] // LANES)]
    return groups[0] if len(groups) == 1 else jnp.concatenate(groups, axis=1)


def _slope(head):
    return 2.0 ** (-(head + 1))


def _branch_b(hb, conv_shift, conv_w_ref, wob_ref):
    gate_b = hb[:, 0:CONV_WIDTH]
    gate_c = hb[:, CONV_WIDTH:2 * CONV_WIDTH]
    h_b = hb[:, 2 * CONV_WIDTH:3 * CONV_WIDTH]
    z_b = hb[:, 3 * CONV_WIDTH:4 * CONV_WIDTH]
    conv_in = gate_c * h_b
    xm1, xm2 = conv_shift(conv_in)
    cw = conv_w_ref[...]
    conv_out = cw[0:1] * xm2 + cw[1:2] * xm1 + cw[2:3] * conv_in
    y_b = gate_b * conv_out
    return _dot((jax.nn.silu(z_b) * y_b).astype(BF16), wob_ref[...]), conv_in


def _branch_c(hc, mix_mask, mix_rows, spb, vnw_ref, woc_ref):
    rows = hc.shape[0]
    u = hc[:, 0:MLP_WIDTH]
    v_c = _rms(hc[:, MLP_WIDTH:2 * MLP_WIDTH], vnw_ref[...])
    z_c = hc[:, 2 * MLP_WIDTH:3 * MLP_WIDTH]
    vb = v_c.astype(BF16)
    gw = MLP_WIDTH // N_SPATIAL_GROUPS
    mixes = [jnp.where(mix_mask, mix_rows(g), 0.0).astype(BF16) for g in range(N_SPATIAL_GROUPS)]
    sp_chunks = []
    for c in range(rows // CHUNK):
        r0 = c * CHUNK
        parts = [_dot(mixes[g], vb[r0:r0 + CHUNK, g * gw:(g + 1) * gw])
                 for g in range(N_SPATIAL_GROUPS)]
        sp_chunks.append(jnp.concatenate(parts, axis=1) + spb)
    sp = sp_chunks[0] if len(sp_chunks) == 1 else jnp.concatenate(sp_chunks, axis=0)
    y_c = u * sp
    return _dot((jax.nn.silu(z_c) * y_c).astype(BF16), woc_ref[...]), v_c


def _branch_a(y_a, z_a, woa_ref):
    return _dot((jax.nn.silu(z_a) * y_a).astype(BF16), woa_ref[...])


def _merge(x, a, b, c_out, gates, wo_ref):
    m = gates[0] * a + gates[1] * b + gates[2] * c_out
    return x + _dot(m.astype(BF16), wo_ref[...])


def _prompt_kernel(sinks_ref, x_ref, norm_w_ref, w_in_ref, b_gate_ref, qw_ref, kw_ref, conv_w_ref,
                   vnw_ref, mix_ref, spb_ref, woa_ref, wob_ref, woc_ref, wo_ref,
                   y_ref, pk_ref, pv_ref, pc_ref,
                   bias_scr, ka_scr, kb_scr, vt_scr, carry_scr, *, rows, nblk, layer):
    i = pl.program_id(0)
    first = lax.rem(i, nblk) == 0
    nsub = rows // WINDOW
    stack = 2 * WINDOW

    @pl.when(i == 0)
    def _():
        key = lax.broadcasted_iota(jnp.int32, (2 * WINDOW, stack), 0)
        c = lax.broadcasted_iota(jnp.int32, (2 * WINDOW, stack), 1)
        left = c < WINDOW
        dist = jnp.where(left, c, c - WINDOW) + WINDOW - key
        band = (dist >= 0) & (dist < WINDOW)
        distf = dist.astype(F32)
        for flag in range(2):
            valid = band & (key >= WINDOW) if flag else band
            for h in range(N_KV_HEADS):
                for half in range(2):
                    slope = jnp.where(left, _slope(h * Q_GROUP + half), _slope(h * Q_GROUP + 2 + half))
                    bias_scr[flag * 4 + h * 2 + half] = jnp.where(valid, -(slope * distf), NEG_INF)

    @pl.when(first)
    def _():
        zeros = jnp.zeros((WINDOW, KV_WIDTH), BF16)
        ka_scr[0:WINDOW, :] = zeros
        kb_scr[0:WINDOW, :] = zeros
        vt_scr[:, 0:WINDOW] = zeros
        carry_scr[...] = jnp.zeros(carry_scr.shape, F32)

    x = x_ref[...]
    xn = _rms(x, norm_w_ref[...]).astype(BF16)
    def proj(c0, c1):
        return _dot(xn, w_in_ref[:, c0:c1])

    def gate(idx):
        g = proj(COL_G + idx * D_MODEL, COL_G + (idx + 1) * D_MODEL)
        return jax.nn.sigmoid(g + b_gate_ref[:, idx * D_MODEL:(idx + 1) * D_MODEL])

    ha = proj(COL_A, COL_B)
    hb = proj(COL_B, COL_C)
    q = _norm_heads(ha[:, 0:ATTN_WIDTH], qw_ref[...]) * (HEAD_DIM ** -0.5)
    k = _pair_rms(ha[:, ATTN_WIDTH:ATTN_WIDTH + KV_WIDTH], kw_ref[...])
    v = ha[:, ATTN_WIDTH + KV_WIDTH:ATTN_WIDTH + 2 * KV_WIDTH]
    z_a = ha[:, ATTN_WIDTH + 2 * KV_WIDTH:COL_B]

    pk_ref[0] = k[rows - WINDOW:rows, :]
    pv_ref[0] = v[rows - WINDOW:rows, :]
    ka_scr[WINDOW:WINDOW + rows, :] = k.astype(BF16)
    kb_scr[WINDOW:WINDOW + rows, :] = pltpu.roll(k, HEAD_DIM, axis=1).astype(BF16)
    for c in range(nsub):
        vt_scr[:, (c + 1) * WINDOW:(c + 2) * WINDOW] = jnp.transpose(
            v[c * WINDOW:(c + 1) * WINDOW, :]).astype(BF16)

    lo = _lo_lanes((WINDOW, LANES))
    scol = lax.broadcasted_iota(jnp.int32, (1, stack), 1) < WINDOW
    flag = first.astype(jnp.int32)
    carry = carry_scr[...]
    prev1 = carry[SUBLANES - 1:SUBLANES, :]
    prev2 = carry[SUBLANES - 2:SUBLANES - 1, :]

    def conv_shift(ci):
        rid = lax.broadcasted_iota(jnp.int32, ci.shape, 0)
        xm1 = jnp.where(rid == 0, prev1, pltpu.roll(ci, 1, axis=0))
        xm2 = jnp.where(rid == 0, prev2, jnp.where(rid == 1, prev1, pltpu.roll(ci, 2, axis=0)))
        return xm1, xm2

    mr = lax.broadcasted_iota(jnp.int32, (CHUNK, CHUNK), 0)
    mc = lax.broadcasted_iota(jnp.int32, (CHUNK, CHUNK), 1)
    done = {}
    stages = [
        lambda: done.update(hc=proj(COL_C, COL_G)),
        lambda: done.update(g0=gate(0)),
        lambda: done.update(g1=gate(1)),
        lambda: done.update(g2=gate(2)),
    ]
    slots = 2 * nsub
    order = list(range(0, slots, 2)) + list(range(1, slots, 2))
    per_slot = [0] * slots
    for n in range(len(stages)):
        per_slot[order[n % slots]] += 1

    def run_stages(slot):
        for _ in range(per_slot[slot]):
            stages.pop(0)()

    y_rows = []
    for c in range(nsub):
        r0 = c * WINDOW
        keys = slice(r0, r0 + 2 * WINDOW)
        k_nat, k_swp = ka_scr[keys, :], kb_scr[keys, :]
        vt = vt_scr[:, keys]
        scores = []
        for h in range(N_KV_HEADS):
            p0 = q[r0:r0 + WINDOW, (2 * h) * LANES:(2 * h + 1) * LANES]
            p1 = q[r0:r0 + WINDOW, (2 * h + 1) * LANES:(2 * h + 2) * LANES]
            for half in range(2):
                keep = lo if half == 0 else jnp.logical_not(lo)
                qs = jnp.concatenate([jnp.where(keep, p0, 0.0), jnp.where(keep, p1, 0.0)],
                                     axis=0).astype(BF16)
                kk = k_nat if h == half else k_swp
                tbl = h * 2 + half
                bias = bias_scr[flag * 4 + tbl] if c == 0 else bias_scr[tbl]
                scores.append(_dot_nt(kk, qs) + bias)
        run_stages(2 * c)
        pairs = []
        for h in range(N_KV_HEADS):
            outs = []
            for half in range(2):
                s = scores[h * 2 + half]
                sink = jnp.where(scol, sinks_ref[layer, h * Q_GROUP + half],
                                 sinks_ref[layer, h * Q_GROUP + 2 + half])
                mx = jnp.maximum(jnp.max(s, axis=0, keepdims=True), sink)
                e = jnp.exp(s - mx)
                den = jnp.sum(e, axis=0, keepdims=True) + jnp.exp(sink - mx)
                outs.append(_dot(vt[h * HEAD_DIM:(h + 1) * HEAD_DIM, :], e.astype(BF16)) * (1.0 / den))
            for p in range(2):
                cols = slice(p * WINDOW, (p + 1) * WINDOW)
                pairs.append(jnp.transpose(jnp.concatenate([outs[0][:, cols], outs[1][:, cols]], axis=0)))
        y_rows.append(jnp.concatenate(pairs, axis=1))
        run_stages(2 * c + 1)
    y_a = y_rows[0] if nsub == 1 else jnp.concatenate(y_rows, axis=0)

    ka_scr[0:WINDOW, :] = ka_scr[rows:rows + WINDOW, :]
    kb_scr[0:WINDOW, :] = kb_scr[rows:rows + WINDOW, :]
    vt_scr[:, 0:WINDOW] = vt_scr[:, rows:rows + WINDOW]

    a_out = _branch_a(y_a, z_a, woa_ref)
    b_out, conv_in = _branch_b(hb, conv_shift, conv_w_ref, wob_ref)
    c_out, _ = _branch_c(done["hc"], mc <= mr, lambda g: mix_ref[g], spb_ref[...], vnw_ref, woc_ref)
    y_ref[...] = _merge(x, a_out, b_out, c_out, [done["g0"], done["g1"], done["g2"]], wo_ref)
    last = conv_in[rows - SUBLANES:rows, :]
    carry_scr[...] = last
    pc_ref[0] = last


def _stage_bf16(src_hbm, dst_scr, stage, sem, layer):
    nslots, chunk_rows = stage.shape[0], stage.shape[1]
    nchunks = dst_scr.shape[0] // chunk_rows
    ahead = nslots - 1

    def copy(k):
        return pltpu.make_async_copy(src_hbm.at[layer, pl.ds(k * chunk_rows, chunk_rows), :],
                                     stage.at[k % nslots], sem.at[k % nslots])

    for k in range(min(ahead, nchunks)):
        copy(k).start(priority=k % 2)
    for k in range(nchunks):
        if k + ahead < nchunks:
            copy(k + ahead).start(priority=(k + ahead) % 2)
        copy(k).wait()
        dst_scr[k * chunk_rows:(k + 1) * chunk_rows, :] = stage[k % nslots].astype(BF16)


def _sample_kernel(sinks_ref, x_ref, ck_ref, cv_ref, st_ref, norm_w_ref, b_gate_ref, qw_ref,
                   kw_ref, conv_w_ref, vnw_ref, mix_ref, spb_ref,
                   w_in_hbm, woa_hbm, wob_hbm, woc_hbm, wo_hbm,
                   y_ref, sk_ref, sv_ref, ci_ref, scv_ref,
                   w_in_out, woa_out, wob_out, woc_out, wo_out,
                   x_scr, bold_scr, bnew_scr, q_scr, en_scr, oo_scr, inv_scr,
                   w_in_ref, woa_ref, wob_ref, woc_ref, wo_ref, stage_in, stage_sq,
                   sem_in, sem_sq, sem_out, *, dec_seq):
    layer = pl.program_id(0)
    j = pl.program_id(1)
    rows = SAMPLE_ROWS
    nseq = rows // dec_seq
    stack = N_HEADS * dec_seq
    w = WINDOW

    @pl.when((layer == 0) & (j == 0))
    def _():
        r = lax.broadcasted_iota(jnp.int32, (stack, LANES), 0)
        col = lax.broadcasted_iota(jnp.int32, (stack, LANES), 1)
        tok = lax.rem(r, dec_seq)
        head = r // dec_seq
        slope = jnp.zeros((stack, LANES), F32)
        for n in range(N_HEADS):
            slope = jnp.where(head == n, _slope(n), slope)
        dist = tok + w - col
        bold_scr[...] = jnp.where(dist < WINDOW, -(slope * dist.astype(F32)), NEG_INF)
        kseq = col // dec_seq
        dist = tok - lax.rem(col, dec_seq)
        pen = -(slope * dist.astype(F32))
        for b in range(nseq):
            bnew_scr[b] = jnp.where((kseq == b) & (dist >= 0), pen, NEG_INF)

    r0 = pl.multiple_of(j * rows, rows)

    resident = ((w_in_hbm, w_in_ref, w_in_out), (woa_hbm, woa_ref, woa_out), (wob_hbm, wob_ref, wob_out),
                (woc_hbm, woc_ref, woc_out), (wo_hbm, wo_ref, wo_out))

    def publish(n, lyr):
        return pltpu.make_async_copy(resident[n][1], resident[n][2].at[lyr], sem_out.at[n])

    @pl.when(j == 0)
    def _():
        @pl.when(layer > 0)
        def _():
            for n in range(len(resident)):
                publish(n, layer - 1).wait()

        _stage_bf16(w_in_hbm, w_in_ref, stage_in, sem_in, layer)
        for src, dst, _ in resident[1:]:
            _stage_bf16(src, dst, stage_sq, sem_sq, layer)
        for n in range(len(resident)):
            publish(n, layer).start()

    @pl.when(layer == 0)
    def _():
        x_scr[pl.ds(r0, rows), :] = x_ref[...]

    x = x_scr[pl.ds(r0, rows), :]
    xn = _rms(x, norm_w_ref[...]).astype(BF16)
    def proj(c0, c1):
        return _dot(xn, w_in_ref[:, c0:c1])

    def gate(idx):
        g = proj(COL_G + idx * D_MODEL, COL_G + (idx + 1) * D_MODEL)
        return jax.nn.sigmoid(g + b_gate_ref[:, idx * D_MODEL:(idx + 1) * D_MODEL])

    ha = proj(COL_A, COL_B)
    hb = proj(COL_B, COL_C)
    q = _norm_heads(ha[:, 0:ATTN_WIDTH], qw_ref[...]) * (HEAD_DIM ** -0.5)
    k = _pair_rms(ha[:, ATTN_WIDTH:ATTN_WIDTH + KV_WIDTH], kw_ref[...])
    v = ha[:, ATTN_WIDTH + KV_WIDTH:ATTN_WIDTH + 2 * KV_WIDTH]
    z_a = ha[:, ATTN_WIDTH + 2 * KV_WIDTH:COL_B]

    lo = _lo_lanes((rows, LANES))
    for h in range(N_KV_HEADS):
        keep = lo if h == 0 else jnp.logical_not(lo)
        for g in range(Q_GROUP):
            pair = q[:, (h * 2 + g // 2) * LANES:(h * 2 + g // 2 + 1) * LANES]
            if g % 2 != h:
                pair = pltpu.roll(pair, HEAD_DIM, axis=1)
            piece = jnp.where(keep, pair, 0.0).reshape(nseq, dec_seq, LANES)
            s0 = (h * Q_GROUP + g) * dec_seq
            q_scr[:, s0:s0 + dec_seq, :] = piece

    kt_new = jnp.transpose(k)
    vt_new = jnp.transpose(v)
    qall = q_scr[...].reshape(nseq * stack, LANES).astype(BF16)
    s_new_all = _dot(qall, kt_new.astype(BF16)).reshape(nseq, stack, LANES)

    srow = lax.broadcasted_iota(jnp.int32, (stack, 1), 0) // dec_seq
    sink = jnp.zeros((stack, 1), F32)
    for n in range(N_HEADS):
        sink = jnp.where(srow == n, sinks_ref[layer, n], sink)
    bias_old = bold_scr[...]
    lane = lax.broadcasted_iota(jnp.int32, (KV_WIDTH, w), 1)
    s_olds = [_dot(q_scr[b].astype(BF16), ck_ref[b].reshape(KV_WIDTH, w).astype(BF16)) + bias_old
              for b in range(nseq)]
    hc = proj(COL_C, COL_G)
    gates = []
    for b in range(nseq):
        if b % (nseq // 4) == 0 and 0 < b:
            gates.append(gate(len(gates)))
        kt = ck_ref[b].reshape(KV_WIDTH, w)
        vt = cv_ref[b].reshape(KV_WIDTH, w)
        s_old = s_olds[b]
        s_new = s_new_all[b] + bnew_scr[b]
        mx = jnp.maximum(jnp.maximum(jnp.max(s_old, axis=-1, keepdims=True),
                                     jnp.max(s_new, axis=-1, keepdims=True)), sink)
        e_old = jnp.exp(s_old - mx)
        e_new = jnp.exp(s_new - mx)
        den = (jnp.sum(e_old, axis=-1, keepdims=True) + jnp.sum(e_new, axis=-1, keepdims=True)
               + jnp.exp(sink - mx))
        en_scr[b] = e_new
        oo_scr[b] = _dot_nt(e_old.astype(BF16), vt.astype(BF16))
        inv_scr[b] = jnp.broadcast_to(1.0 / den, (stack, LANES))
        shift = w - dec_seq - b * dec_seq
        newk = pltpu.roll(kt_new, shift, axis=1) if shift else kt_new
        newv = pltpu.roll(vt_new, shift, axis=1) if shift else vt_new
        keep_old = lane < w - dec_seq
        sk_ref[b] = jnp.where(keep_old, pltpu.roll(kt, w - dec_seq, axis=1), newk).reshape(
            N_KV_HEADS, HEAD_DIM, w)
        sv_ref[b] = jnp.where(keep_old, pltpu.roll(vt, w - dec_seq, axis=1), newv).reshape(
            N_KV_HEADS, HEAD_DIM, w)

    o_new = _dot(en_scr[...].reshape(nseq * stack, LANES).astype(BF16), v.astype(BF16))
    o = (oo_scr[...] + o_new.reshape(nseq, stack, LANES)) * inv_scr[...]
    pairs = []
    for h in range(N_KV_HEADS):
        heads = []
        for g in range(Q_GROUP):
            s0 = (h * Q_GROUP + g) * dec_seq
            og = o[:, s0:s0 + dec_seq, :].reshape(rows, LANES)
            if g % 2 != h:
                og = pltpu.roll(og, HEAD_DIM, axis=1)
            heads.append(og)
        pairs.append(jnp.where(lo, heads[0], heads[1]))
        pairs.append(jnp.where(lo, heads[2], heads[3]))
    y_a = jnp.concatenate(pairs, axis=1)

    st = st_ref[...]

    def conv_shift(ci):
        t = lax.rem(lax.broadcasted_iota(jnp.int32, ci.shape, 0), dec_seq)
        xm1 = jnp.where(t == 0, pltpu.roll(st, rows - 1, axis=0), pltpu.roll(ci, 1, axis=0))
        xm2 = jnp.where(t < CONV_K - 1, st, pltpu.roll(ci, 2, axis=0))
        return xm1, xm2

    mr = lax.broadcasted_iota(jnp.int32, (CHUNK, CHUNK), 0)
    mc = lax.broadcasted_iota(jnp.int32, (CHUNK, CHUNK), 1)
    mix_mask = (mr // dec_seq == mc // dec_seq) & (mc <= mr)
    b_out, conv_in = _branch_b(hb, conv_shift, conv_w_ref, wob_ref)
    reps = CHUNK // dec_seq

    def mix_rows(g):
        return jnp.broadcast_to(mix_ref[g][None], (reps, dec_seq, CHUNK)).reshape(CHUNK, CHUNK)

    spb = jnp.broadcast_to(spb_ref[...][None], (reps, dec_seq, MLP_WIDTH)).reshape(CHUNK, MLP_WIDTH)
    c_out, v_c = _branch_c(hc, mix_mask, mix_rows, spb, vnw_ref, woc_ref)
    y = _merge(x, _branch_a(y_a, z_a, woa_ref), b_out, c_out, gates, wo_ref)
    x_scr[pl.ds(r0, rows), :] = y

    @pl.when(layer == pl.num_programs(0) - 1)
    def _():
        y_ref[...] = y
    ci_ref[...] = conv_in
    scv_ref[...] = v_c

    @pl.when((layer == pl.num_programs(0) - 1) & (j == pl.num_programs(1) - 1))
    def _():
        for n in range(len(resident)):
            publish(n, layer).wait()


SMALL_WEIGHT_NAMES = ("norm_w", "b_gate", "qw", "kw", "conv_w", "vnw", "mix", "spb")
BIG_WEIGHT_NAMES = ("w_in", "woa", "wob", "woc", "wo")
STAGE_SLOTS = 4
STAGE_ROWS_W_IN = 32
STAGE_ROWS_SQUARE = 128
WEIGHT_NAMES = ("norm_w", "w_in", "b_gate", "qw", "kw", "conv_w", "vnw", "mix", "spb",
                "woa", "wob", "woc", "wo")


def _layer_spec(arr, layer_of):
    nd = arr.ndim - 1
    return pl.BlockSpec((None,) + arr.shape[1:], lambda *g, _nd=nd: (layer_of(*g),) + (0,) * _nd,
                        pipeline_mode=pl.Buffered(1))


def _prompt_layer(x, wts, layer, batch):
    n, d = x.shape
    rows = PROMPT_ROWS
    nblk = n // batch // rows
    smem = pl.BlockSpec(memory_space=pltpu.SMEM)
    in_specs = ([smem, pl.BlockSpec((rows, d), lambda i: (i, 0))]
                + [_layer_spec(wts[name], lambda i: layer) for name in WEIGHT_NAMES])
    out_shape = (jax.ShapeDtypeStruct((n, d), F32),
                 jax.ShapeDtypeStruct((batch, WINDOW, KV_WIDTH), F32),
                 jax.ShapeDtypeStruct((batch, WINDOW, KV_WIDTH), F32),
                 jax.ShapeDtypeStruct((batch, SUBLANES, CONV_WIDTH), F32))
    out_specs = (pl.BlockSpec((rows, d), lambda i: (i, 0)),
                 pl.BlockSpec((1, WINDOW, KV_WIDTH), lambda i: (i // nblk, 0, 0)),
                 pl.BlockSpec((1, WINDOW, KV_WIDTH), lambda i: (i // nblk, 0, 0)),
                 pl.BlockSpec((1, SUBLANES, CONV_WIDTH), lambda i: (i // nblk, 0, 0)))
    n_tables = 2 * N_KV_HEADS * 2
    scratch = [pltpu.VMEM((n_tables, 2 * WINDOW, 2 * WINDOW), F32)]
    scratch += [pltpu.VMEM((WINDOW + rows, KV_WIDTH), BF16) for _ in range(2)]
    scratch += [pltpu.VMEM((KV_WIDTH, WINDOW + rows), BF16)]
    scratch += [pltpu.VMEM((SUBLANES, CONV_WIDTH), F32)]
    return pl.pallas_call(
        functools.partial(_prompt_kernel, rows=rows, nblk=nblk, layer=layer),
        grid=(n // rows,),
        in_specs=in_specs, out_specs=out_specs, out_shape=out_shape, scratch_shapes=scratch,
        compiler_params=pltpu.CompilerParams(dimension_semantics=("arbitrary",),
                                             vmem_limit_bytes=VMEM_LIMIT_BYTES),
        name="prompt_layer",
    )(wts["sinks"], x, *[wts[name] for name in WEIGHT_NAMES])


def _sample_layers(x, ck, cv, st, wts, dec_seq):
    n, d = x.shape
    depth = ck.shape[0]
    rows = SAMPLE_ROWS
    nseq = rows // dec_seq
    stack = N_HEADS * dec_seq
    w = ck.shape[-1]
    smem = pl.BlockSpec(memory_space=pltpu.SMEM)
    cache_blk = pl.BlockSpec((None, nseq, N_KV_HEADS, HEAD_DIM, w), lambda l, j: (l, j, 0, 0, 0))
    rows_blk = lambda width: pl.BlockSpec((None, rows, width), lambda l, j: (l, j, 0))
    nblk = n // rows
    x_blk = pl.BlockSpec((rows, d), lambda l, j: (jnp.where(l == 0, j, nblk - 1), 0))
    y_blk = pl.BlockSpec((rows, d), lambda l, j: (jnp.where(l == depth - 1, j, 0), 0))
    hbm = pl.BlockSpec(memory_space=pl.ANY)
    in_specs = ([smem, x_blk, cache_blk, cache_blk, rows_blk(CONV_WIDTH)]
                + [_layer_spec(wts[name], lambda l, j: l) for name in SMALL_WEIGHT_NAMES]
                + [hbm] * len(BIG_WEIGHT_NAMES))
    out_shape = (jax.ShapeDtypeStruct((n, d), F32),
                 jax.ShapeDtypeStruct(ck.shape, F32),
                 jax.ShapeDtypeStruct(cv.shape, F32),
                 jax.ShapeDtypeStruct((depth, n, CONV_WIDTH), F32),
                 jax.ShapeDtypeStruct((depth, n, MLP_WIDTH), F32)
                 ) + tuple(jax.ShapeDtypeStruct(wts[name].shape, BF16) for name in BIG_WEIGHT_NAMES)
    out_specs = ((y_blk, cache_blk, cache_blk, rows_blk(CONV_WIDTH), rows_blk(MLP_WIDTH))
                 + (hbm,) * len(BIG_WEIGHT_NAMES))
    scratch = [pltpu.VMEM((n, d), F32),
               pltpu.VMEM((stack, LANES), F32),
               pltpu.VMEM((nseq, stack, LANES), F32),
               pltpu.VMEM((nseq, stack, LANES), F32),
               pltpu.VMEM((nseq, stack, LANES), F32),
               pltpu.VMEM((nseq, stack, LANES), F32),
               pltpu.VMEM((nseq, stack, LANES), F32)]
    scratch += [pltpu.VMEM(wts[name].shape[1:], BF16) for name in BIG_WEIGHT_NAMES]
    scratch += [pltpu.VMEM((STAGE_SLOTS, STAGE_ROWS_W_IN, IN_COLS), F32),
                pltpu.VMEM((STAGE_SLOTS, STAGE_ROWS_SQUARE, d), F32),
                pltpu.SemaphoreType.DMA((STAGE_SLOTS,)), pltpu.SemaphoreType.DMA((STAGE_SLOTS,)),
                pltpu.SemaphoreType.DMA((len(BIG_WEIGHT_NAMES),))]
    return pl.pallas_call(
        functools.partial(_sample_kernel, dec_seq=dec_seq),
        grid=(depth, n // rows),
        in_specs=in_specs, out_specs=out_specs, out_shape=out_shape, scratch_shapes=scratch,
        compiler_params=pltpu.CompilerParams(dimension_semantics=("arbitrary", "arbitrary"),
                                             vmem_limit_bytes=VMEM_LIMIT_BYTES),
        name="sample_layers",
    )(wts["sinks"], x, ck, cv, st, *[wts[name] for name in SMALL_WEIGHT_NAMES + BIG_WEIGHT_NAMES])


def kernel(x_prompt, x_sample, cache_k, cache_v, state_conv, norm_w, w_in, b_gate, q_norm_w, k_norm_w,
           sinks, conv_w, v_norm_w, w_spatial, b_spatial, w_out_a, w_out_b, w_out_c, w_o):
    batch, seq, d = x_prompt.shape
    dec_batch, dec_seq, _ = x_sample.shape
    depth = w_in.shape[0]
    w_buf = cache_k.shape[2]
    assert d == D_MODEL and seq % PROMPT_ROWS == 0 and (dec_batch * dec_seq) % SAMPLE_ROWS == 0
    assert w_buf == WINDOW and dec_seq == SUBLANES and w_in.shape[2] == IN_COLS

    gw = MLP_WIDTH // N_SPATIAL_GROUPS
    reps = CHUNK // dec_seq
    spb_p = jnp.repeat(jnp.swapaxes(b_spatial, 1, 2), gw, axis=2)
    common = {
        "sinks": sinks,
        "norm_w": norm_w[:, None, :], "w_in": w_in, "b_gate": b_gate[:, None, :],
        "qw": jnp.tile(q_norm_w, (1, LANES // HEAD_DIM))[:, None, :],
        "kw": jnp.tile(k_norm_w, (1, LANES // HEAD_DIM))[:, None, :],
        "conv_w": conv_w, "vnw": v_norm_w[:, None, :],
        "woa": w_out_a, "wob": w_out_b, "woc": w_out_c, "wo": w_o,
    }
    wts_s = dict(common,
                 mix=jnp.tile(w_spatial[:, :, :dec_seq, :dec_seq], (1, 1, 1, reps)),
                 spb=spb_p[:, :dec_seq, :])

    st_rows = jnp.pad(state_conv, ((0, 0), (0, 0), (0, dec_seq - (CONV_K - 1)), (0, 0)))
    st_rows = st_rows.reshape(depth, dec_batch * dec_seq, CONV_WIDTH)
    ck = jnp.transpose(cache_k, (0, 1, 3, 4, 2))
    cv = jnp.transpose(cache_v, (0, 1, 3, 4, 2))

    ys, sk, sv, ci, scv, *big_bf16 = _sample_layers(x_sample.reshape(dec_batch * dec_seq, d), ck, cv,
                                                    st_rows, wts_s, dec_seq)
    wts_p = dict(common, mix=w_spatial, spb=spb_p, **dict(zip(BIG_WEIGHT_NAMES, big_bf16)))
    sk = jnp.transpose(sk, (0, 1, 4, 2, 3))
    sv = jnp.transpose(sv, (0, 1, 4, 2, 3))
    sc = ci.reshape(depth, dec_batch, dec_seq, CONV_WIDTH)[:, :, dec_seq - (CONV_K - 1):, :]
    scv = scv.reshape(depth, dec_batch, dec_seq, MLP_WIDTH)

    yp = x_prompt.reshape(batch * seq, d)
    pk, pv, pc = [], [], []
    for l in range(depth):
        yp, k_l, v_l, c_l = _prompt_layer(yp, wts_p, l, batch)
        pk.append(k_l.reshape(batch, WINDOW, N_KV_HEADS, HEAD_DIM))
        pv.append(v_l.reshape(batch, WINDOW, N_KV_HEADS, HEAD_DIM))
        pc.append(c_l[:, SUBLANES - (CONV_K - 1):, :])

    return (yp.reshape(batch, seq, d), ys.reshape(dec_batch, dec_seq, d), jnp.stack(pk), jnp.stack(pv),
            jnp.stack(pc), sk, sv, sc, scv)
```

```python
import functools

import jax
import jax.numpy as jnp
from jax import lax
from jax.experimental import pallas as pl
from jax.experimental.pallas import tpu as pltpu

F32 = jnp.float32
BF16 = jnp.bfloat16

D_MODEL = 1024
N_HEADS = 8
N_KV_HEADS = 2
HEAD_DIM = 64
Q_GROUP = N_HEADS // N_KV_HEADS
ATTN_WIDTH = N_HEADS * HEAD_DIM
KV_WIDTH = N_KV_HEADS * HEAD_DIM
WINDOW = 128
CONV_WIDTH = 512
CONV_K = 3
CHUNK = 128
MLP_WIDTH = 512
N_SPATIAL_GROUPS = 4
EPS = 1e-6
NEG_INF = -1e30

COL_A = 0
COL_B = COL_A + 2 * ATTN_WIDTH + 2 * KV_WIDTH
COL_C = COL_B + 4 * CONV_WIDTH
COL_G = COL_C + 3 * MLP_WIDTH
IN_COLS = COL_G + 3 * D_MODEL

LANES = 128
SUBLANES = 8
PROMPT_ROWS = 512
SAMPLE_ROWS = 128
VMEM_LIMIT_BYTES = 56 * 1024 * 1024


def _dot(a, b):
    return jnp.dot(a, b, preferred_element_type=F32)


def _dot_nt(a, b):
    return lax.dot_general(a, b, (((1,), (1,)), ((), ())), preferred_element_type=F32)


def _rms(x, w):
    ms = jnp.mean(x * x, axis=-1, keepdims=True)
    return (x * lax.rsqrt(ms + EPS)) * w


def _sigmoid(x):
    return 0.5 * jnp.tanh(0.5 * x) + 0.5


def _silu(x):
    return x * _sigmoid(x)


def _lo_lanes(shape):
    return lax.broadcasted_iota(jnp.int32, shape, len(shape) - 1) < HEAD_DIM


def _pair_rms(x, w):
    lo = _lo_lanes(x.shape)
    sq = x * x
    s_lo = jnp.sum(jnp.where(lo, sq, 0.0), axis=-1, keepdims=True)
    s_hi = jnp.sum(jnp.where(lo, 0.0, sq), axis=-1, keepdims=True)
    ms = jnp.where(lo, s_lo, s_hi) * (1.0 / HEAD_DIM)
    return (x * lax.rsqrt(ms + EPS)) * w


def _norm_heads(x, w):
    groups = [_pair_rms(x[:, g * LANES:(g + 1) * LANES], w) for g in range(x.shape[1] // LANES)]
    return groups[0] if len(groups) == 1 else jnp.concatenate(groups, axis=1)


def _slope(head):
    return 2.0 ** (-(head + 1))


def _branch_b(hb, conv_shift, conv_w_ref, wob_ref):
    gate_b = hb[:, 0:CONV_WIDTH]
    gate_c = hb[:, CONV_WIDTH:2 * CONV_WIDTH]
    h_b = hb[:, 2 * CONV_WIDTH:3 * CONV_WIDTH]
    z_b = hb[:, 3 * CONV_WIDTH:4 * CONV_WIDTH]
    conv_in = gate_c * h_b
    xm1, xm2 = conv_shift(conv_in)
    cw = conv_w_ref[...]
    conv_out = cw[0:1] * xm2 + cw[1:2] * xm1 + cw[2:3] * conv_in
    y_b = gate_b * conv_out
    return _dot((_silu(z_b) * y_b).astype(BF16), wob_ref[...]), conv_in


def _branch_c(hc, mix_mask, mix_rows, spb, vnw_ref, woc_ref):
    rows = hc.shape[0]
    u = hc[:, 0:MLP_WIDTH]
    v_c = _rms(hc[:, MLP_WIDTH:2 * MLP_WIDTH], vnw_ref[...])
    z_c = hc[:, 2 * MLP_WIDTH:3 * MLP_WIDTH]
    vb = v_c.astype(BF16)
    gw = MLP_WIDTH // N_SPATIAL_GROUPS
    mixes = [jnp.where(mix_mask, mix_rows(g), 0.0).astype(BF16) for g in range(N_SPATIAL_GROUPS)]
    sp_chunks = []
    for c in range(rows // CHUNK):
        r0 = c * CHUNK
        parts = [_dot(mixes[g], vb[r0:r0 + CHUNK, g * gw:(g + 1) * gw])
                 for g in range(N_SPATIAL_GROUPS)]
        sp_chunks.append(jnp.concatenate(parts, axis=1) + spb)
    sp = sp_chunks[0] if len(sp_chunks) == 1 else jnp.concatenate(sp_chunks, axis=0)
    y_c = u * sp
    return _dot((_silu(z_c) * y_c).astype(BF16), woc_ref[...]), v_c


def _branch_a(y_a, z_a, woa_ref):
    return _dot((_silu(z_a) * y_a).astype(BF16), woa_ref[...])


def _merge(x, a, b, c_out, gates, wo_ref):
    m = gates[0] * a + gates[1] * b + gates[2] * c_out
    return x + _dot(m.astype(BF16), wo_ref[...])


def _prompt_kernel(sinks_ref, x_ref, norm_w_ref, w_in_ref, b_gate_ref, qw_ref, kw_ref, conv_w_ref,
                   vnw_ref, mix_ref, spb_ref, woa_ref, wob_ref, woc_ref, wo_ref,
                   y_ref, pk_ref, pv_ref, pc_ref,
                   bias_scr, ka_scr, kb_scr, vt_scr, carry_scr, *, rows, nblk, layer):
    i = pl.program_id(0)
    first = lax.rem(i, nblk) == 0
    nsub = rows // WINDOW
    stack = 2 * WINDOW

    @pl.when(i == 0)
    def _():
        key = lax.broadcasted_iota(jnp.int32, (2 * WINDOW, stack), 0)
        c = lax.broadcasted_iota(jnp.int32, (2 * WINDOW, stack), 1)
        left = c < WINDOW
        dist = jnp.where(left, c, c - WINDOW) + WINDOW - key
        band = (dist >= 0) & (dist < WINDOW)
        distf = dist.astype(F32)
        for flag in range(2):
            valid = band & (key >= WINDOW) if flag else band
            for h in range(N_KV_HEADS):
                for half in range(2):
                    slope = jnp.where(left, _slope(h * Q_GROUP + half), _slope(h * Q_GROUP + 2 + half))
                    bias_scr[flag * 4 + h * 2 + half] = jnp.where(valid, -(slope * distf), NEG_INF)

    @pl.when(first)
    def _():
        zeros = jnp.zeros((WINDOW, KV_WIDTH), BF16)
        ka_scr[0:WINDOW, :] = zeros
        kb_scr[0:WINDOW, :] = zeros
        vt_scr[:, 0:WINDOW] = zeros
        carry_scr[...] = jnp.zeros(carry_scr.shape, F32)

    x = x_ref[...]
    xn = _rms(x, norm_w_ref[...]).astype(BF16)
    def proj(c0, c1):
        return _dot(xn, w_in_ref[:, c0:c1])

    def gate(idx):
        g = proj(COL_G + idx * D_MODEL, COL_G + (idx + 1) * D_MODEL)
        return _sigmoid(g + b_gate_ref[:, idx * D_MODEL:(idx + 1) * D_MODEL])

    ha = proj(COL_A, COL_B)
    hb = proj(COL_B, COL_C)
    q = _norm_heads(ha[:, 0:ATTN_WIDTH], qw_ref[...]) * (HEAD_DIM ** -0.5)
    k = _pair_rms(ha[:, ATTN_WIDTH:ATTN_WIDTH + KV_WIDTH], kw_ref[...])
    v = ha[:, ATTN_WIDTH + KV_WIDTH:ATTN_WIDTH + 2 * KV_WIDTH]
    z_a = ha[:, ATTN_WIDTH + 2 * KV_WIDTH:COL_B]

    pk_ref[0] = k[rows - WINDOW:rows, :]
    pv_ref[0] = v[rows - WINDOW:rows, :]
    ka_scr[WINDOW:WINDOW + rows, :] = k.astype(BF16)
    kb_scr[WINDOW:WINDOW + rows, :] = pltpu.roll(k, HEAD_DIM, axis=1).astype(BF16)
    for c in range(nsub):
        vt_scr[:, (c + 1) * WINDOW:(c + 2) * WINDOW] = jnp.transpose(
            v[c * WINDOW:(c + 1) * WINDOW, :]).astype(BF16)

    lo = _lo_lanes((WINDOW, LANES))
    scol = lax.broadcasted_iota(jnp.int32, (1, stack), 1) < WINDOW
    flag = first.astype(jnp.int32)
    carry = carry_scr[...]
    prev1 = carry[SUBLANES - 1:SUBLANES, :]
    prev2 = carry[SUBLANES - 2:SUBLANES - 1, :]

    def conv_shift(ci):
        rid = lax.broadcasted_iota(jnp.int32, ci.shape, 0)
        xm1 = jnp.where(rid == 0, prev1, pltpu.roll(ci, 1, axis=0))
        xm2 = jnp.where(rid == 0, prev2, jnp.where(rid == 1, prev1, pltpu.roll(ci, 2, axis=0)))
        return xm1, xm2

    mr = lax.broadcasted_iota(jnp.int32, (CHUNK, CHUNK), 0)
    mc = lax.broadcasted_iota(jnp.int32, (CHUNK, CHUNK), 1)
    done = {}
    stages = [
        lambda: done.update(hc=proj(COL_C, COL_G)),
        lambda: done.update(g0=gate(0)),
        lambda: done.update(g1=gate(1)),
        lambda: done.update(g2=gate(2)),
    ]
    slots = 2 * nsub
    order = list(range(0, slots, 2)) + list(range(1, slots, 2))
    per_slot = [0] * slots
    for n in range(len(stages)):
        per_slot[order[n % slots]] += 1

    def run_stages(slot):
        for _ in range(per_slot[slot]):
            stages.pop(0)()

    y_rows = []
    for c in range(nsub):
        r0 = c * WINDOW
        keys = slice(r0, r0 + 2 * WINDOW)
        k_nat, k_swp = ka_scr[keys, :], kb_scr[keys, :]
        vt = vt_scr[:, keys]
        scores = []
        for h in range(N_KV_HEADS):
            p0 = q[r0:r0 + WINDOW, (2 * h) * LANES:(2 * h + 1) * LANES]
            p1 = q[r0:r0 + WINDOW, (2 * h + 1) * LANES:(2 * h + 2) * LANES]
            for half in range(2):
                keep = lo if half == 0 else jnp.logical_not(lo)
                qs = jnp.concatenate([jnp.where(keep, p0, 0.0), jnp.where(keep, p1, 0.0)],
                                     axis=0).astype(BF16)
                kk = k_nat if h == half else k_swp
                tbl = h * 2 + half
                bias = bias_scr[flag * 4 + tbl] if c == 0 else bias_scr[tbl]
                scores.append(_dot_nt(kk, qs) + bias)
        run_stages(2 * c)
        pairs = []
        for h in range(N_KV_HEADS):
            outs = []
            for half in range(2):
                s = scores[h * 2 + half]
                sink = jnp.where(scol, sinks_ref[layer, h * Q_GROUP + half],
                                 sinks_ref[layer, h * Q_GROUP + 2 + half])
                mx = jnp.maximum(jnp.max(s, axis=0, keepdims=True), sink)
                e = jnp.exp(s - mx)
                den = jnp.sum(e, axis=0, keepdims=True) + jnp.exp(sink - mx)
                outs.append(_dot(vt[h * HEAD_DIM:(h + 1) * HEAD_DIM, :], e.astype(BF16)) * (1.0 / den))
            for p in range(2):
                cols = slice(p * WINDOW, (p + 1) * WINDOW)
                pairs.append(jnp.transpose(jnp.concatenate([outs[0][:, cols], outs[1][:, cols]], axis=0)))
        y_rows.append(jnp.concatenate(pairs, axis=1))
        run_stages(2 * c + 1)
    y_a = y_rows[0] if nsub == 1 else jnp.concatenate(y_rows, axis=0)

    ka_scr[0:WINDOW, :] = ka_scr[rows:rows + WINDOW, :]
    kb_scr[0:WINDOW, :] = kb_scr[rows:rows + WINDOW, :]
    vt_scr[:, 0:WINDOW] = vt_scr[:, rows:rows + WINDOW]

    a_out = _branch_a(y_a, z_a, woa_ref)
    b_out, conv_in = _branch_b(hb, conv_shift, conv_w_ref, wob_ref)
    c_out, _ = _branch_c(done["hc"], mc <= mr, lambda g: mix_ref[g], spb_ref[...], vnw_ref, woc_ref)
    y_ref[...] = _merge(x, a_out, b_out, c_out, [done["g0"], done["g1"], done["g2"]], wo_ref)
    last = conv_in[rows - SUBLANES:rows, :]
    carry_scr[...] = last
    pc_ref[0] = last


def _stage_bf16(src_hbm, dst_scr, stage, sem, layer):
    nslots, chunk_rows = stage.shape[0], stage.shape[1]
    nchunks = dst_scr.shape[0] // chunk_rows
    ahead = nslots - 1

    def copy(k):
        return pltpu.make_async_copy(src_hbm.at[layer, pl.ds(k * chunk_rows, chunk_rows), :],
                                     stage.at[k % nslots], sem.at[k % nslots])

    for k in range(min(ahead, nchunks)):
        copy(k).start()
    for k in range(nchunks):
        if k + ahead < nchunks:
            copy(k + ahead).start()
        copy(k).wait()
        dst_scr[k * chunk_rows:(k + 1) * chunk_rows, :] = stage[k % nslots].astype(BF16)


def _sample_kernel(sinks_ref, x_ref, ck_ref, cv_ref, st_ref, norm_w_ref, b_gate_ref, qw_ref,
                   kw_ref, conv_w_ref, vnw_ref, mix_ref, spb_ref,
                   w_in_hbm, woa_hbm, wob_hbm, woc_hbm, wo_hbm,
                   y_ref, sk_ref, sv_ref, ci_ref, scv_ref,
                   w_in_out, woa_out, wob_out, woc_out, wo_out,
                   x_scr, bold_scr, bnew_scr, q_scr, en_scr, oo_scr, inv_scr,
                   w_in_ref, woa_ref, wob_ref, woc_ref, wo_ref, stage_in, stage_sq,
                   sem_in, sem_sq, sem_out, *, dec_seq):
    layer = pl.program_id(0)
    j = pl.program_id(1)
    rows = SAMPLE_ROWS
    nseq = rows // dec_seq
    stack = N_HEADS * dec_seq
    w = WINDOW

    @pl.when((layer == 0) & (j == 0))
    def _():
        r = lax.broadcasted_iota(jnp.int32, (stack, LANES), 0)
        col = lax.broadcasted_iota(jnp.int32, (stack, LANES), 1)
        tok = lax.rem(r, dec_seq)
        head = r // dec_seq
        slope = jnp.zeros((stack, LANES), F32)
        for n in range(N_HEADS):
            slope = jnp.where(head == n, _slope(n), slope)
        dist = tok + w - col
        bold_scr[...] = jnp.where(dist < WINDOW, -(slope * dist.astype(F32)), NEG_INF)
        kseq = col // dec_seq
        dist = tok - lax.rem(col, dec_seq)
        pen = -(slope * dist.astype(F32))
        for b in range(nseq):
            bnew_scr[b] = jnp.where((kseq == b) & (dist >= 0), pen, NEG_INF)

    r0 = pl.multiple_of(j * rows, rows)

    resident = ((w_in_hbm, w_in_ref, w_in_out), (woa_hbm, woa_ref, woa_out), (wob_hbm, wob_ref, wob_out),
                (woc_hbm, woc_ref, woc_out), (wo_hbm, wo_ref, wo_out))

    def publish(n, lyr):
        return pltpu.make_async_copy(resident[n][1], resident[n][2].at[lyr], sem_out.at[n])

    @pl.when(j == 0)
    def _():
        @pl.when(layer > 0)
        def _():
            for n in range(len(resident)):
                publish(n, layer - 1).wait()

        _stage_bf16(w_in_hbm, w_in_ref, stage_in, sem_in, layer)
        for src, dst, _ in resident[1:]:
            _stage_bf16(src, dst, stage_sq, sem_sq, layer)
        for n in range(len(resident)):
            publish(n, layer).start()

    @pl.when(layer == 0)
    def _():
        x_scr[pl.ds(r0, rows), :] = x_ref[...]

    x = x_scr[pl.ds(r0, rows), :]
    xn = _rms(x, norm_w_ref[...]).astype(BF16)
    def proj(c0, c1):
        return _dot(xn, w_in_ref[:, c0:c1])

    def gate(idx):
        g = proj(COL_G + idx * D_MODEL, COL_G + (idx + 1) * D_MODEL)
        return _sigmoid(g + b_gate_ref[:, idx * D_MODEL:(idx + 1) * D_MODEL])

    ha = proj(COL_A, COL_B)
    hb = proj(COL_B, COL_C)
    q = _norm_heads(ha[:, 0:ATTN_WIDTH], qw_ref[...]) * (HEAD_DIM ** -0.5)
    k = _pair_rms(ha[:, ATTN_WIDTH:ATTN_WIDTH + KV_WIDTH], kw_ref[...])
    v = ha[:, ATTN_WIDTH + KV_WIDTH:ATTN_WIDTH + 2 * KV_WIDTH]
    z_a = ha[:, ATTN_WIDTH + 2 * KV_WIDTH:COL_B]

    lo = _lo_lanes((rows, LANES))
    for h in range(N_KV_HEADS):
        keep = lo if h == 0 else jnp.logical_not(lo)
        for g in range(Q_GROUP):
            pair = q[:, (h * 2 + g // 2) * LANES:(h * 2 + g // 2 + 1) * LANES]
            if g % 2 != h:
                pair = pltpu.roll(pair, HEAD_DIM, axis=1)
            piece = jnp.where(keep, pair, 0.0).reshape(nseq, dec_seq, LANES)
            s0 = (h * Q_GROUP + g) * dec_seq
            q_scr[:, s0:s0 + dec_seq, :] = piece

    kt_new = jnp.transpose(k)
    vt_new = jnp.transpose(v)
    qall = q_scr[...].reshape(nseq * stack, LANES).astype(BF16)
    s_new_all = _dot(qall, kt_new.astype(BF16)).reshape(nseq, stack, LANES)

    srow = lax.broadcasted_iota(jnp.int32, (stack, 1), 0) // dec_seq
    sink = jnp.zeros((stack, 1), F32)
    for n in range(N_HEADS):
        sink = jnp.where(srow == n, sinks_ref[layer, n], sink)
    bias_old = bold_scr[...]
    lane = lax.broadcasted_iota(jnp.int32, (KV_WIDTH, w), 1)
    s_olds = [_dot(q_scr[b].astype(BF16), ck_ref[b].reshape(KV_WIDTH, w).astype(BF16)) + bias_old
              for b in range(nseq)]
    hc = proj(COL_C, COL_G)
    gates = []
    for b in range(nseq):
        if b % (nseq // 4) == 0 and 0 < b:
            gates.append(gate(len(gates)))
        kt = ck_ref[b].reshape(KV_WIDTH, w)
        vt = cv_ref[b].reshape(KV_WIDTH, w)
        s_old = s_olds[b]
        s_new = s_new_all[b] + bnew_scr[b]
        mx = jnp.maximum(jnp.maximum(jnp.max(s_old, axis=-1, keepdims=True),
                                     jnp.max(s_new, axis=-1, keepdims=True)), sink)
        e_old = jnp.exp(s_old - mx)
        e_new = jnp.exp(s_new - mx)
        den = (jnp.sum(e_old, axis=-1, keepdims=True) + jnp.sum(e_new, axis=-1, keepdims=True)
               + jnp.exp(sink - mx))
        en_scr[b] = e_new
        oo_scr[b] = _dot_nt(e_old.astype(BF16), vt.astype(BF16))
        inv_scr[b] = jnp.broadcast_to(1.0 / den, (stack, LANES))
        shift = w - dec_seq - b * dec_seq
        newk = pltpu.roll(kt_new, shift, axis=1) if shift else kt_new
        newv = pltpu.roll(vt_new, shift, axis=1) if shift else vt_new
        keep_old = lane < w - dec_seq
        sk_ref[b] = jnp.where(keep_old, pltpu.roll(kt, w - dec_seq, axis=1), newk).reshape(
            N_KV_HEADS, HEAD_DIM, w)
        sv_ref[b] = jnp.where(keep_old, pltpu.roll(vt, w - dec_seq, axis=1), newv).reshape(
            N_KV_HEADS, HEAD_DIM, w)

    o_new = _dot(en_scr[...].reshape(nseq * stack, LANES).astype(BF16), v.astype(BF16))
    o = (oo_scr[...] + o_new.reshape(nseq, stack, LANES)) * inv_scr[...]
    pairs = []
    for h in range(N_KV_HEADS):
        heads = []
        for g in range(Q_GROUP):
            s0 = (h * Q_GROUP + g) * dec_seq
            og = o[:, s0:s0 + dec_seq, :].reshape(rows, LANES)
            if g % 2 != h:
                og = pltpu.roll(og, HEAD_DIM, axis=1)
            heads.append(og)
        pairs.append(jnp.where(lo, heads[0], heads[1]))
        pairs.append(jnp.where(lo, heads[2], heads[3]))
    y_a = jnp.concatenate(pairs, axis=1)

    st = st_ref[...]

    def conv_shift(ci):
        t = lax.rem(lax.broadcasted_iota(jnp.int32, ci.shape, 0), dec_seq)
        xm1 = jnp.where(t == 0, pltpu.roll(st, rows - 1, axis=0), pltpu.roll(ci, 1, axis=0))
        xm2 = jnp.where(t < CONV_K - 1, st, pltpu.roll(ci, 2, axis=0))
        return xm1, xm2

    mr = lax.broadcasted_iota(jnp.int32, (CHUNK, CHUNK), 0)
    mc = lax.broadcasted_iota(jnp.int32, (CHUNK, CHUNK), 1)
    mix_mask = (mr // dec_seq == mc // dec_seq) & (mc <= mr)
    b_out, conv_in = _branch_b(hb, conv_shift, conv_w_ref, wob_ref)
    reps = CHUNK // dec_seq

    def mix_rows(g):
        return jnp.broadcast_to(mix_ref[g][None], (reps, dec_seq, CHUNK)).reshape(CHUNK, CHUNK)

    spb = jnp.broadcast_to(spb_ref[...][None], (reps, dec_seq, MLP_WIDTH)).reshape(CHUNK, MLP_WIDTH)
    c_out, v_c = _branch_c(hc, mix_mask, mix_rows, spb, vnw_ref, woc_ref)
    y = _merge(x, _branch_a(y_a, z_a, woa_ref), b_out, c_out, gates, wo_ref)
    x_scr[pl.ds(r0, rows), :] = y

    @pl.when(layer == pl.num_programs(0) - 1)
    def _():
        y_ref[...] = y
    ci_ref[...] = conv_in
    scv_ref[...] = v_c

    @pl.when((layer == pl.num_programs(0) - 1) & (j == pl.num_programs(1) - 1))
    def _():
        for n in range(len(resident)):
            publish(n, layer).wait()


SMALL_WEIGHT_NAMES = ("norm_w", "b_gate", "qw", "kw", "conv_w", "vnw", "mix", "spb")
BIG_WEIGHT_NAMES = ("w_in", "woa", "wob", "woc", "wo")
STAGE_SLOTS = 4
STAGE_ROWS_W_IN = 32
STAGE_ROWS_SQUARE = 128
WEIGHT_NAMES = ("norm_w", "w_in", "b_gate", "qw", "kw", "conv_w", "vnw", "mix", "spb",
                "woa", "wob", "woc", "wo")


def _layer_spec(arr, layer_of):
    nd = arr.ndim - 1
    return pl.BlockSpec((None,) + arr.shape[1:], lambda *g, _nd=nd: (layer_of(*g),) + (0,) * _nd,
                        pipeline_mode=pl.Buffered(1))


def _prompt_layer(x, wts, layer, batch):
    n, d = x.shape
    rows = PROMPT_ROWS
    nblk = n // batch // rows
    smem = pl.BlockSpec(memory_space=pltpu.SMEM)
    in_specs = ([smem, pl.BlockSpec((rows, d), lambda i: (i, 0))]
                + [_layer_spec(wts[name], lambda i: layer) for name in WEIGHT_NAMES])
    out_shape = (jax.ShapeDtypeStruct((n, d), F32),
                 jax.ShapeDtypeStruct((batch, WINDOW, KV_WIDTH), F32),
                 jax.ShapeDtypeStruct((batch, WINDOW, KV_WIDTH), F32),
                 jax.ShapeDtypeStruct((batch, SUBLANES, CONV_WIDTH), F32))
    out_specs = (pl.BlockSpec((rows, d), lambda i: (i, 0)),
                 pl.BlockSpec((1, WINDOW, KV_WIDTH), lambda i: (i // nblk, 0, 0)),
                 pl.BlockSpec((1, WINDOW, KV_WIDTH), lambda i: (i // nblk, 0, 0)),
                 pl.BlockSpec((1, SUBLANES, CONV_WIDTH), lambda i: (i // nblk, 0, 0)))
    n_tables = 2 * N_KV_HEADS * 2
    scratch = [pltpu.VMEM((n_tables, 2 * WINDOW, 2 * WINDOW), F32)]
    scratch += [pltpu.VMEM((WINDOW + rows, KV_WIDTH), BF16) for _ in range(2)]
    scratch += [pltpu.VMEM((KV_WIDTH, WINDOW + rows), BF16)]
    scratch += [pltpu.VMEM((SUBLANES, CONV_WIDTH), F32)]
    return pl.pallas_call(
        functools.partial(_prompt_kernel, rows=rows, nblk=nblk, layer=layer),
        grid=(n // rows,),
        in_specs=in_specs, out_specs=out_specs, out_shape=out_shape, scratch_shapes=scratch,
        compiler_params=pltpu.CompilerParams(dimension_semantics=("arbitrary",),
                                             vmem_limit_bytes=VMEM_LIMIT_BYTES),
        name="prompt_layer",
    )(wts["sinks"], x, *[wts[name] for name in WEIGHT_NAMES])


def _sample_layers(x, ck, cv, st, wts, dec_seq):
    n, d = x.shape
    depth = ck.shape[0]
    rows = SAMPLE_ROWS
    nseq = rows // dec_seq
    stack = N_HEADS * dec_seq
    w = ck.shape[-1]
    smem = pl.BlockSpec(memory_space=pltpu.SMEM)
    cache_blk = pl.BlockSpec((None, nseq, N_KV_HEADS, HEAD_DIM, w), lambda l, j: (l, j, 0, 0, 0))
    rows_blk = lambda width: pl.BlockSpec((None, rows, width), lambda l, j: (l, j, 0))
    nblk = n // rows
    x_blk = pl.BlockSpec((rows, d), lambda l, j: (jnp.where(l == 0, j, nblk - 1), 0))
    y_blk = pl.BlockSpec((rows, d), lambda l, j: (jnp.where(l == depth - 1, j, 0), 0))
    hbm = pl.BlockSpec(memory_space=pl.ANY)
    in_specs = ([smem, x_blk, cache_blk, cache_blk, rows_blk(CONV_WIDTH)]
                + [_layer_spec(wts[name], lambda l, j: l) for name in SMALL_WEIGHT_NAMES]
                + [hbm] * len(BIG_WEIGHT_NAMES))
    out_shape = (jax.ShapeDtypeStruct((n, d), F32),
                 jax.ShapeDtypeStruct(ck.shape, F32),
                 jax.ShapeDtypeStruct(cv.shape, F32),
                 jax.ShapeDtypeStruct((depth, n, CONV_WIDTH), F32),
                 jax.ShapeDtypeStruct((depth, n, MLP_WIDTH), F32)
                 ) + tuple(jax.ShapeDtypeStruct(wts[name].shape, BF16) for name in BIG_WEIGHT_NAMES)
    out_specs = ((y_blk, cache_blk, cache_blk, rows_blk(CONV_WIDTH), rows_blk(MLP_WIDTH))
                 + (hbm,) * len(BIG_WEIGHT_NAMES))
    scratch = [pltpu.VMEM((n, d), F32),
               pltpu.VMEM((stack, LANES), F32),
               pltpu.VMEM((nseq, stack, LANES), F32),
               pltpu.VMEM((nseq, stack, LANES), F32),
               pltpu.VMEM((nseq, stack, LANES), F32),
               pltpu.VMEM((nseq, stack, LANES), F32),
               pltpu.VMEM((nseq, stack, LANES), F32)]
    scratch += [pltpu.VMEM(wts[name].shape[1:], BF16) for name in BIG_WEIGHT_NAMES]
    scratch += [pltpu.VMEM((STAGE_SLOTS, STAGE_ROWS_W_IN, IN_COLS), F32),
                pltpu.VMEM((STAGE_SLOTS, STAGE_ROWS_SQUARE, d), F32),
                pltpu.SemaphoreType.DMA((STAGE_SLOTS,)), pltpu.SemaphoreType.DMA((STAGE_SLOTS,)),
                pltpu.SemaphoreType.DMA((len(BIG_WEIGHT_NAMES),))]
    return pl.pallas_call(
        functools.partial(_sample_kernel, dec_seq=dec_seq),
        grid=(depth, n // rows),
        in_specs=in_specs, out_specs=out_specs, out_shape=out_shape, scratch_shapes=scratch,
        compiler_params=pltpu.CompilerParams(dimension_semantics=("arbitrary", "arbitrary"),
                                             vmem_limit_bytes=VMEM_LIMIT_BYTES),
        name="sample_layers",
    )(wts["sinks"], x, ck, cv, st, *[wts[name] for name in SMALL_WEIGHT_NAMES + BIG_WEIGHT_NAMES])


def kernel(x_prompt, x_sample, cache_k, cache_v, state_conv, norm_w, w_in, b_gate, q_norm_w, k_norm_w,
           sinks, conv_w, v_norm_w, w_spatial, b_spatial, w_out_a, w_out_b, w_out_c, w_o):
    batch, seq, d = x_prompt.shape
    dec_batch, dec_seq, _ = x_sample.shape
    depth = w_in.shape[0]
    w_buf = cache_k.shape[2]
    assert d == D_MODEL and seq % PROMPT_ROWS == 0 and (dec_batch * dec_seq) % SAMPLE_ROWS == 0
    assert w_buf == WINDOW and dec_seq == SUBLANES and w_in.shape[2] == IN_COLS

    gw = MLP_WIDTH // N_SPATIAL_GROUPS
    reps = CHUNK // dec_seq
    spb_p = jnp.repeat(jnp.swapaxes(b_spatial, 1, 2), gw, axis=2)
    common = {
        "sinks": sinks,
        "norm_w": norm_w[:, None, :], "w_in": w_in, "b_gate": b_gate[:, None, :],
        "qw": jnp.tile(q_norm_w, (1, LANES // HEAD_DIM))[:, None, :],
        "kw": jnp.tile(k_norm_w, (1, LANES // HEAD_DIM))[:, None, :],
        "conv_w": conv_w, "vnw": v_norm_w[:, None, :],
        "woa": w_out_a, "wob": w_out_b, "woc": w_out_c, "wo": w_o,
    }
    wts_s = dict(common,
                 mix=jnp.tile(w_spatial[:, :, :dec_seq, :dec_seq], (1, 1, 1, reps)),
                 spb=spb_p[:, :dec_seq, :])

    st_rows = jnp.pad(state_conv, ((0, 0), (0, 0), (0, dec_seq - (CONV_K - 1)), (0, 0)))
    st_rows = st_rows.reshape(depth, dec_batch * dec_seq, CONV_WIDTH)
    ck = jnp.transpose(cache_k, (0, 1, 3, 4, 2))
    cv = jnp.transpose(cache_v, (0, 1, 3, 4, 2))

    ys, sk, sv, ci, scv, *big_bf16 = _sample_layers(x_sample.reshape(dec_batch * dec_seq, d), ck, cv,
                                                    st_rows, wts_s, dec_seq)
    wts_p = dict(common, mix=w_spatial, spb=spb_p, **dict(zip(BIG_WEIGHT_NAMES, big_bf16)))
    sk = jnp.transpose(sk, (0, 1, 4, 2, 3))
    sv = jnp.transpose(sv, (0, 1, 4, 2, 3))
    sc = ci.reshape(depth, dec_batch, dec_seq, CONV_WIDTH)[:, :, dec_seq - (CONV_K - 1):, :]
    scv = scv.reshape(depth, dec_batch, dec_seq, MLP_WIDTH)

    yp = x_prompt.reshape(batch * seq, d)
    pk, pv, pc = [], [], []
    for l in range(depth):
        yp, k_l, v_l, c_l = _prompt_layer(yp, wts_p, l, batch)
        pk.append(k_l.reshape(batch, WINDOW, N_KV_HEADS, HEAD_DIM))
        pv.append(v_l.reshape(batch, WINDOW, N_KV_HEADS, HEAD_DIM))
        pc.append(c_l[:, SUBLANES - (CONV_K - 1):, :])

    return (yp.reshape(batch, seq, d), ys.reshape(dec_batch, dec_seq, d), jnp.stack(pk), jnp.stack(pv),
            jnp.stack(pc), sk, sv, sc, scv)
```

```python
import functools

import jax
import jax.numpy as jnp
from jax import lax
from jax.experimental import pallas as pl
from jax.experimental.pallas import tpu as pltpu

F32 = jnp.float32
BF16 = jnp.bfloat16

D_MODEL = 1024
N_HEADS = 8
N_KV_HEADS = 2
HEAD_DIM = 64
Q_GROUP = N_HEADS // N_KV_HEADS
ATTN_WIDTH = N_HEADS * HEAD_DIM
KV_WIDTH = N_KV_HEADS * HEAD_DIM
WINDOW = 128
CONV_WIDTH = 512
CONV_K = 3
CHUNK = 128
MLP_WIDTH = 512
N_SPATIAL_GROUPS = 4
EPS = 1e-6
NEG_INF = -1e30

COL_A = 0
COL_B = COL_A + 2 * ATTN_WIDTH + 2 * KV_WIDTH
COL_C = COL_B + 4 * CONV_WIDTH
COL_G = COL_C + 3 * MLP_WIDTH
IN_COLS = COL_G + 3 * D_MODEL

LANES = 128
SUBLANES = 8
PROMPT_ROWS = 512
SAMPLE_ROWS = 128
VMEM_LIMIT_BYTES = 56 * 1024 * 1024


def _dot(a, b):
    return jnp.dot(a, b, preferred_element_type=F32)


def _dot_nt(a, b):
    return lax.dot_general(a, b, (((1,), (1,)), ((), ())), preferred_element_type=F32)


def _rms(x, w):
    ms = jnp.mean(x * x, axis=-1, keepdims=True)
    return (x * lax.rsqrt(ms + EPS)) * w


def _sigmoid(x):
    return 0.5 * jnp.tanh(0.5 * x) + 0.5


def _silu(x):
    return x * _sigmoid(x)


def _lo_lanes(shape):
    return lax.broadcasted_iota(jnp.int32, shape, len(shape) - 1) < HEAD_DIM


def _pair_rms(x, w):
    lo = _lo_lanes(x.shape)
    sq = x * x
    s_lo = jnp.sum(jnp.where(lo, sq, 0.0), axis=-1, keepdims=True)
    s_hi = jnp.sum(jnp.where(lo, 0.0, sq), axis=-1, keepdims=True)
    ms = jnp.where(lo, s_lo, s_hi) * (1.0 / HEAD_DIM)
    return (x * lax.rsqrt(ms + EPS)) * w


def _norm_heads(x, w):
    groups = [_pair_rms(x[:, g * LANES:(g + 1) * LANES], w) for g in range(x.shape[1] // LANES)]
    return groups[0] if len(groups) == 1 else jnp.concatenate(groups, axis=1)


def _slope(head):
    return 2.0 ** (-(head + 1))


def _branch_b(hb, conv_shift, conv_w_ref, wob_ref):
    gate_b = hb[:, 0:CONV_WIDTH]
    gate_c = hb[:, CONV_WIDTH:2 * CONV_WIDTH]
    h_b = hb[:, 2 * CONV_WIDTH:3 * CONV_WIDTH]
    z_b = hb[:, 3 * CONV_WIDTH:4 * CONV_WIDTH]
    conv_in = gate_c * h_b
    xm1, xm2 = conv_shift(conv_in)
    cw = conv_w_ref[...]
    conv_out = cw[0:1] * xm2 + cw[1:2] * xm1 + cw[2:3] * conv_in
    y_b = gate_b * conv_out
    return _dot((_silu(z_b) * y_b).astype(BF16), wob_ref[...]), conv_in


def _branch_c(hc, mix_mask, mix_rows, spb, vnw_ref, woc_ref):
    rows = hc.shape[0]
    u = hc[:, 0:MLP_WIDTH]
    v_c = _rms(hc[:, MLP_WIDTH:2 * MLP_WIDTH], vnw_ref[...])
    z_c = hc[:, 2 * MLP_WIDTH:3 * MLP_WIDTH]
    vb = v_c.astype(BF16)
    gw = MLP_WIDTH // N_SPATIAL_GROUPS
    mixes = [jnp.where(mix_mask, mix_rows(g), 0.0).astype(BF16) for g in range(N_SPATIAL_GROUPS)]
    sp_chunks = []
    for c in range(rows // CHUNK):
        r0 = c * CHUNK
        parts = [_dot(mixes[g], vb[r0:r0 + CHUNK, g * gw:(g + 1) * gw])
                 for g in range(N_SPATIAL_GROUPS)]
        sp_chunks.append(jnp.concatenate(parts, axis=1) + spb)
    sp = sp_chunks[0] if len(sp_chunks) == 1 else jnp.concatenate(sp_chunks, axis=0)
    y_c = u * sp
    return _dot((_silu(z_c) * y_c).astype(BF16), woc_ref[...]), v_c


def _branch_a(y_a, z_a, woa_ref):
    return _dot((_silu(z_a) * y_a).astype(BF16), woa_ref[...])


def _merge(x, a, b, c_out, gates, wo_ref):
    m = gates[0] * a + gates[1] * b + gates[2] * c_out
    return x + _dot(m.astype(BF16), wo_ref[...])


def _prompt_kernel(sinks_ref, x_ref, norm_w_ref, w_in_ref, b_gate_ref, qw_ref, kw_ref, conv_w_ref,
                   vnw_ref, mix_ref, spb_ref, woa_ref, wob_ref, woc_ref, wo_ref,
                   y_ref, pk_ref, pv_ref, pc_ref,
                   bias_scr, ka_scr, kb_scr, vt_scr, carry_scr, *, rows, nblk, layer):
    i = pl.program_id(0)
    first = lax.rem(i, nblk) == 0
    nsub = rows // WINDOW
    stack = 2 * WINDOW

    @pl.when(i == 0)
    def _():
        key = lax.broadcasted_iota(jnp.int32, (2 * WINDOW, stack), 0)
        c = lax.broadcasted_iota(jnp.int32, (2 * WINDOW, stack), 1)
        left = c < WINDOW
        dist = jnp.where(left, c, c - WINDOW) + WINDOW - key
        band = (dist >= 0) & (dist < WINDOW)
        distf = dist.astype(F32)
        for flag in range(2):
            valid = band & (key >= WINDOW) if flag else band
            for h in range(N_KV_HEADS):
                for half in range(2):
                    slope = jnp.where(left, _slope(h * Q_GROUP + half), _slope(h * Q_GROUP + 2 + half))
                    bias_scr[flag * 4 + h * 2 + half] = jnp.where(valid, -(slope * distf), NEG_INF)

    @pl.when(first)
    def _():
        zeros = jnp.zeros((WINDOW, KV_WIDTH), BF16)
        ka_scr[0:WINDOW, :] = zeros
        kb_scr[0:WINDOW, :] = zeros
        vt_scr[:, 0:WINDOW] = zeros
        carry_scr[...] = jnp.zeros(carry_scr.shape, F32)

    x = x_ref[...]
    xn = _rms(x, norm_w_ref[...]).astype(BF16)
    def proj(c0, c1):
        return _dot(xn, w_in_ref[:, c0:c1])

    def gate(idx):
        g = proj(COL_G + idx * D_MODEL, COL_G + (idx + 1) * D_MODEL)
        return _sigmoid(g + b_gate_ref[:, idx * D_MODEL:(idx + 1) * D_MODEL])

    ha = proj(COL_A, COL_B)
    hb = proj(COL_B, COL_C)
    q = _norm_heads(ha[:, 0:ATTN_WIDTH], qw_ref[...]) * (HEAD_DIM ** -0.5)
    k = _pair_rms(ha[:, ATTN_WIDTH:ATTN_WIDTH + KV_WIDTH], kw_ref[...])
    v = ha[:, ATTN_WIDTH + KV_WIDTH:ATTN_WIDTH + 2 * KV_WIDTH]
    z_a = ha[:, ATTN_WIDTH + 2 * KV_WIDTH:COL_B]

    pk_ref[0] = k[rows - WINDOW:rows, :]
    pv_ref[0] = v[rows - WINDOW:rows, :]
    ka_scr[WINDOW:WINDOW + rows, :] = k.astype(BF16)
    kb_scr[WINDOW:WINDOW + rows, :] = pltpu.roll(k, HEAD_DIM, axis=1).astype(BF16)
    for c in range(nsub):
        vt_scr[:, (c + 1) * WINDOW:(c + 2) * WINDOW] = jnp.transpose(
            v[c * WINDOW:(c + 1) * WINDOW, :]).astype(BF16)

    lo = _lo_lanes((WINDOW, LANES))
    scol = lax.broadcasted_iota(jnp.int32, (1, stack), 1) < WINDOW
    flag = first.astype(jnp.int32)
    carry = carry_scr[...]
    prev1 = carry[SUBLANES - 1:SUBLANES, :]
    prev2 = carry[SUBLANES - 2:SUBLANES - 1, :]

    def conv_shift(ci):
        rid = lax.broadcasted_iota(jnp.int32, ci.shape, 0)
        xm1 = jnp.where(rid == 0, prev1, pltpu.roll(ci, 1, axis=0))
        xm2 = jnp.where(rid == 0, prev2, jnp.where(rid == 1, prev1, pltpu.roll(ci, 2, axis=0)))
        return xm1, xm2

    mr = lax.broadcasted_iota(jnp.int32, (CHUNK, CHUNK), 0)
    mc = lax.broadcasted_iota(jnp.int32, (CHUNK, CHUNK), 1)
    done = {}
    stages = [
        lambda: done.update(hc=proj(COL_C, COL_G)),
        lambda: done.update(g0=gate(0)),
        lambda: done.update(g1=gate(1)),
        lambda: done.update(g2=gate(2)),
    ]
    slots = 2 * nsub
    order = list(range(0, slots, 2)) + list(range(1, slots, 2))
    per_slot = [0] * slots
    for n in range(len(stages)):
        per_slot[order[n % slots]] += 1

    def run_stages(slot):
        for _ in range(per_slot[slot]):
            stages.pop(0)()

    y_rows = []
    for c in range(nsub):
        r0 = c * WINDOW
        keys = slice(r0, r0 + 2 * WINDOW)
        k_nat, k_swp = ka_scr[keys, :], kb_scr[keys, :]
        vt = vt_scr[:, keys]
        scores = []
        for h in range(N_KV_HEADS):
            p0 = q[r0:r0 + WINDOW, (2 * h) * LANES:(2 * h + 1) * LANES]
            p1 = q[r0:r0 + WINDOW, (2 * h + 1) * LANES:(2 * h + 2) * LANES]
            for half in range(2):
                keep = lo if half == 0 else jnp.logical_not(lo)
                qs = jnp.concatenate([jnp.where(keep, p0, 0.0), jnp.where(keep, p1, 0.0)],
                                     axis=0).astype(BF16)
                kk = k_nat if h == half else k_swp
                tbl = h * 2 + half
                bias = bias_scr[flag * 4 + tbl] if c == 0 else bias_scr[tbl]
                scores.append(_dot_nt(kk, qs) + bias)
        run_stages(2 * c)
        pairs = []
        for h in range(N_KV_HEADS):
            outs = []
            for half in range(2):
                s = scores[h * 2 + half]
                sink = jnp.where(scol, sinks_ref[layer, h * Q_GROUP + half],
                                 sinks_ref[layer, h * Q_GROUP + 2 + half])
                mx = jnp.maximum(jnp.max(s, axis=0, keepdims=True), sink)
                e = jnp.exp(s - mx)
                den = jnp.sum(e, axis=0, keepdims=True) + jnp.exp(sink - mx)
                outs.append(_dot(vt[h * HEAD_DIM:(h + 1) * HEAD_DIM, :], e.astype(BF16)) * (1.0 / den))
            for p in range(2):
                cols = slice(p * WINDOW, (p + 1) * WINDOW)
                pairs.append(jnp.transpose(jnp.concatenate([outs[0][:, cols], outs[1][:, cols]], axis=0)))
        y_rows.append(jnp.concatenate(pairs, axis=1))
        run_stages(2 * c + 1)
    y_a = y_rows[0] if nsub == 1 else jnp.concatenate(y_rows, axis=0)

    ka_scr[0:WINDOW, :] = ka_scr[rows:rows + WINDOW, :]
    kb_scr[0:WINDOW, :] = kb_scr[rows:rows + WINDOW, :]
    vt_scr[:, 0:WINDOW] = vt_scr[:, rows:rows + WINDOW]

    a_out = _branch_a(y_a, z_a, woa_ref)
    b_out, conv_in = _branch_b(hb, conv_shift, conv_w_ref, wob_ref)
    c_out, _ = _branch_c(done["hc"], mc <= mr, lambda g: mix_ref[g], spb_ref[...], vnw_ref, woc_ref)
    y_ref[...] = _merge(x, a_out, b_out, c_out, [done["g0"], done["g1"], done["g2"]], wo_ref)
    last = conv_in[rows - SUBLANES:rows, :]
    carry_scr[...] = last
    pc_ref[0] = last


def _stage_bf16(src_hbm, dst_scr, stage, sem, layer):
    nslots, chunk_rows = stage.shape[0], stage.shape[1]
    nchunks = dst_scr.shape[0] // chunk_rows
    ahead = nslots - 1

    def copy(k):
        return pltpu.make_async_copy(src_hbm.at[layer, pl.ds(k * chunk_rows, chunk_rows), :],
                                     stage.at[k % nslots], sem.at[k % nslots])

    for k in range(min(ahead, nchunks)):
        copy(k).start()
    for k in range(nchunks):
        if k + ahead < nchunks:
            copy(k + ahead).start()
        copy(k).wait()
        dst_scr[k * chunk_rows:(k + 1) * chunk_rows, :] = stage[k % nslots].astype(BF16)


def _sample_kernel(sinks_ref, x_ref, ck_ref, cv_ref, st_ref, norm_w_ref, b_gate_ref, qw_ref,
                   kw_ref, conv_w_ref, vnw_ref, mix_ref, spb_ref,
                   w_in_hbm, woa_hbm, wob_hbm, woc_hbm, wo_hbm,
                   y_ref, sk_ref, sv_ref, ci_ref, scv_ref,
                   w_in_out, woa_out, wob_out, woc_out, wo_out,
                   x_scr, bold_scr, bnew_scr, q_scr, en_scr, oo_scr, inv_scr,
                   w_in_ref, woa_ref, wob_ref, woc_ref, wo_ref, stage_in, stage_sq,
                   sem_in, sem_sq, sem_out, *, dec_seq):
    layer = pl.program_id(0)
    j = pl.program_id(1)
    rows = SAMPLE_ROWS
    nseq = rows // dec_seq
    stack = N_HEADS * dec_seq
    w = WINDOW

    @pl.when((layer == 0) & (j == 0))
    def _():
        r = lax.broadcasted_iota(jnp.int32, (stack, LANES), 0)
        col = lax.broadcasted_iota(jnp.int32, (stack, LANES), 1)
        tok = lax.rem(r, dec_seq)
        head = r // dec_seq
        slope = jnp.zeros((stack, LANES), F32)
        for n in range(N_HEADS):
            slope = jnp.where(head == n, _slope(n), slope)
        dist = tok + w - col
        bold_scr[...] = jnp.where(dist < WINDOW, -(slope * dist.astype(F32)), NEG_INF)
        kseq = col // dec_seq
        dist = tok - lax.rem(col, dec_seq)
        pen = -(slope * dist.astype(F32))
        for b in range(nseq):
            bnew_scr[b] = jnp.where((kseq == b) & (dist >= 0), pen, NEG_INF)

    r0 = pl.multiple_of(j * rows, rows)

    resident = ((w_in_hbm, w_in_ref, w_in_out), (woa_hbm, woa_ref, woa_out), (wob_hbm, wob_ref, wob_out),
                (woc_hbm, woc_ref, woc_out), (wo_hbm, wo_ref, wo_out))

    def publish(n, lyr):
        return pltpu.make_async_copy(resident[n][1], resident[n][2].at[lyr], sem_out.at[n])

    @pl.when(j == 0)
    def _():
        @pl.when(layer > 0)
        def _():
            for n in range(len(resident)):
                publish(n, layer - 1).wait()

        _stage_bf16(w_in_hbm, w_in_ref, stage_in, sem_in, layer)
        for src, dst, _ in resident[1:]:
            _stage_bf16(src, dst, stage_sq, sem_sq, layer)
        for n in range(len(resident)):
            publish(n, layer).start()

    @pl.when(layer == 0)
    def _():
        x_scr[pl.ds(r0, rows), :] = x_ref[...]

    x = x_scr[pl.ds(r0, rows), :]
    xn = _rms(x, norm_w_ref[...]).astype(BF16)
    def proj(c0, c1):
        return _dot(xn, w_in_ref[:, c0:c1])

    def gate(idx):
        g = proj(COL_G + idx * D_MODEL, COL_G + (idx + 1) * D_MODEL)
        return _sigmoid(g + b_gate_ref[:, idx * D_MODEL:(idx + 1) * D_MODEL])

    ha = proj(COL_A, COL_B)
    hb = proj(COL_B, COL_C)
    q = _norm_heads(ha[:, 0:ATTN_WIDTH], qw_ref[...]) * (HEAD_DIM ** -0.5)
    k = _pair_rms(ha[:, ATTN_WIDTH:ATTN_WIDTH + KV_WIDTH], kw_ref[...])
    v = ha[:, ATTN_WIDTH + KV_WIDTH:ATTN_WIDTH + 2 * KV_WIDTH]
    z_a = ha[:, ATTN_WIDTH + 2 * KV_WIDTH:COL_B]

    lo = _lo_lanes((rows, LANES))
    for h in range(N_KV_HEADS):
        keep = lo if h == 0 else jnp.logical_not(lo)
        for g in range(Q_GROUP):
            pair = q[:, (h * 2 + g // 2) * LANES:(h * 2 + g // 2 + 1) * LANES]
            if g % 2 != h:
                pair = pltpu.roll(pair, HEAD_DIM, axis=1)
            piece = jnp.where(keep, pair, 0.0).reshape(nseq, dec_seq, LANES)
            s0 = (h * Q_GROUP + g) * dec_seq
            q_scr[:, s0:s0 + dec_seq, :] = piece

    kt_new = jnp.transpose(k)
    vt_new = jnp.transpose(v)
    qall = q_scr[...].reshape(nseq * stack, LANES).astype(BF16)
    s_new_all = _dot(qall, kt_new.astype(BF16)).reshape(nseq, stack, LANES)

    srow = lax.broadcasted_iota(jnp.int32, (stack, 1), 0) // dec_seq
    sink = jnp.zeros((stack, 1), F32)
    for n in range(N_HEADS):
        sink = jnp.where(srow == n, sinks_ref[layer, n], sink)
    bias_old = bold_scr[...]
    lane = lax.broadcasted_iota(jnp.int32, (KV_WIDTH, w), 1)
    s_olds = [_dot(q_scr[b].astype(BF16), ck_ref[b].reshape(KV_WIDTH, w).astype(BF16)) + bias_old
              for b in range(nseq)]
    hc = proj(COL_C, COL_G)
    gates = []
    for b in range(nseq):
        if b % (nseq // 4) == 0 and 0 < b:
            gates.append(gate(len(gates)))
        kt = ck_ref[b].reshape(KV_WIDTH, w)
        vt = cv_ref[b].reshape(KV_WIDTH, w)
        s_old = s_olds[b]
        s_new = s_new_all[b] + bnew_scr[b]
        mx = jnp.maximum(jnp.maximum(jnp.max(s_old, axis=-1, keepdims=True),
                                     jnp.max(s_new, axis=-1, keepdims=True)), sink)
        e_old = jnp.exp(s_old - mx)
        e_new = jnp.exp(s_new - mx)
        den = (jnp.sum(e_old, axis=-1, keepdims=True) + jnp.sum(e_new, axis=-1, keepdims=True)
               + jnp.exp(sink - mx))
        en_scr[b] = e_new
        oo_scr[b] = _dot_nt(e_old.astype(BF16), vt.astype(BF16))
        inv_scr[b] = jnp.broadcast_to(1.0 / den, (stack, LANES))
        shift = w - dec_seq - b * dec_seq
        newk = pltpu.roll(kt_new, shift, axis=1) if shift else kt_new
        newv = pltpu.roll(vt_new, shift, axis=1) if shift else vt_new
        keep_old = lane < w - dec_seq
        sk_ref[b] = jnp.where(keep_old, pltpu.roll(kt, w - dec_seq, axis=1), newk).reshape(
            N_KV_HEADS, HEAD_DIM, w)
        sv_ref[b] = jnp.where(keep_old, pltpu.roll(vt, w - dec_seq, axis=1), newv).reshape(
            N_KV_HEADS, HEAD_DIM, w)

    o_new = _dot(en_scr[...].reshape(nseq * stack, LANES).astype(BF16), v.astype(BF16))
    o = (oo_scr[...] + o_new.reshape(nseq, stack, LANES)) * inv_scr[...]
    pairs = []
    for h in range(N_KV_HEADS):
        heads = []
        for g in range(Q_GROUP):
            s0 = (h * Q_GROUP + g) * dec_seq
            og = o[:, s0:s0 + dec_seq, :].reshape(rows, LANES)
            if g % 2 != h:
                og = pltpu.roll(og, HEAD_DIM, axis=1)
            heads.append(og)
        pairs.append(jnp.where(lo, heads[0], heads[1]))
        pairs.append(jnp.where(lo, heads[2], heads[3]))
    y_a = jnp.concatenate(pairs, axis=1)

    st = st_ref[...]

    def conv_shift(ci):
        t = lax.rem(lax.broadcasted_iota(jnp.int32, ci.shape, 0), dec_seq)
        xm1 = jnp.where(t == 0, pltpu.roll(st, rows - 1, axis=0), pltpu.roll(ci, 1, axis=0))
        xm2 = jnp.where(t < CONV_K - 1, st, pltpu.roll(ci, 2, axis=0))
        return xm1, xm2

    mr = lax.broadcasted_iota(jnp.int32, (CHUNK, CHUNK), 0)
    mc = lax.broadcasted_iota(jnp.int32, (CHUNK, CHUNK), 1)
    mix_mask = (mr // dec_seq == mc // dec_seq) & (mc <= mr)
    b_out, conv_in = _branch_b(hb, conv_shift, conv_w_ref, wob_ref)
    reps = CHUNK // dec_seq

    def mix_rows(g):
        return jnp.broadcast_to(mix_ref[g][None], (reps, dec_seq, CHUNK)).reshape(CHUNK, CHUNK)

    spb = jnp.broadcast_to(spb_ref[...][None], (reps, dec_seq, MLP_WIDTH)).reshape(CHUNK, MLP_WIDTH)
    c_out, v_c = _branch_c(hc, mix_mask, mix_rows, spb, vnw_ref, woc_ref)
    y = _merge(x, _branch_a(y_a, z_a, woa_ref), b_out, c_out, gates, wo_ref)
    x_scr[pl.ds(r0, rows), :] = y

    @pl.when(layer == pl.num_programs(0) - 1)
    def _():
        y_ref[...] = y
    ci_ref[...] = conv_in
    scv_ref[...] = v_c

    @pl.when((layer == pl.num_programs(0) - 1) & (j == pl.num_programs(1) - 1))
    def _():
        for n in range(len(resident)):
            publish(n, layer).wait()


SMALL_WEIGHT_NAMES = ("norm_w", "b_gate", "qw", "kw", "conv_w", "vnw", "mix", "spb")
BIG_WEIGHT_NAMES = ("w_in", "woa", "wob", "woc", "wo")
STAGE_SLOTS = 8
STAGE_ROWS_W_IN = 32
STAGE_ROWS_SQUARE = 128
WEIGHT_NAMES = ("norm_w", "w_in", "b_gate", "qw", "kw", "conv_w", "vnw", "mix", "spb",
                "woa", "wob", "woc", "wo")


def _layer_spec(arr, layer_of):
    nd = arr.ndim - 1
    return pl.BlockSpec((None,) + arr.shape[1:], lambda *g, _nd=nd: (layer_of(*g),) + (0,) * _nd,
                        pipeline_mode=pl.Buffered(1))


def _prompt_layer(x, wts, layer, batch):
    n, d = x.shape
    rows = PROMPT_ROWS
    nblk = n // batch // rows
    smem = pl.BlockSpec(memory_space=pltpu.SMEM)
    in_specs = ([smem, pl.BlockSpec((rows, d), lambda i: (i, 0))]
                + [_layer_spec(wts[name], lambda i: layer) for name in WEIGHT_NAMES])
    out_shape = (jax.ShapeDtypeStruct((n, d), F32),
                 jax.ShapeDtypeStruct((batch, WINDOW, KV_WIDTH), F32),
                 jax.ShapeDtypeStruct((batch, WINDOW, KV_WIDTH), F32),
                 jax.ShapeDtypeStruct((batch, SUBLANES, CONV_WIDTH), F32))
    out_specs = (pl.BlockSpec((rows, d), lambda i: (i, 0)),
                 pl.BlockSpec((1, WINDOW, KV_WIDTH), lambda i: (i // nblk, 0, 0)),
                 pl.BlockSpec((1, WINDOW, KV_WIDTH), lambda i: (i // nblk, 0, 0)),
                 pl.BlockSpec((1, SUBLANES, CONV_WIDTH), lambda i: (i // nblk, 0, 0)))
    n_tables = 2 * N_KV_HEADS * 2
    scratch = [pltpu.VMEM((n_tables, 2 * WINDOW, 2 * WINDOW), F32)]
    scratch += [pltpu.VMEM((WINDOW + rows, KV_WIDTH), BF16) for _ in range(2)]
    scratch += [pltpu.VMEM((KV_WIDTH, WINDOW + rows), BF16)]
    scratch += [pltpu.VMEM((SUBLANES, CONV_WIDTH), F32)]
    return pl.pallas_call(
        functools.partial(_prompt_kernel, rows=rows, nblk=nblk, layer=layer),
        grid=(n // rows,),
        in_specs=in_specs, out_specs=out_specs, out_shape=out_shape, scratch_shapes=scratch,
        compiler_params=pltpu.CompilerParams(dimension_semantics=("arbitrary",),
                                             vmem_limit_bytes=VMEM_LIMIT_BYTES),
        name="prompt_layer",
    )(wts["sinks"], x, *[wts[name] for name in WEIGHT_NAMES])


def _sample_layers(x, ck, cv, st, wts, dec_seq):
    n, d = x.shape
    depth = ck.shape[0]
    rows = SAMPLE_ROWS
    nseq = rows // dec_seq
    stack = N_HEADS * dec_seq
    w = ck.shape[-1]
    smem = pl.BlockSpec(memory_space=pltpu.SMEM)
    cache_blk = pl.BlockSpec((None, nseq, N_KV_HEADS, HEAD_DIM, w), lambda l, j: (l, j, 0, 0, 0))
    rows_blk = lambda width: pl.BlockSpec((None, rows, width), lambda l, j: (l, j, 0))
    nblk = n // rows
    x_blk = pl.BlockSpec((rows, d), lambda l, j: (jnp.where(l == 0, j, nblk - 1), 0))
    y_blk = pl.BlockSpec((rows, d), lambda l, j: (jnp.where(l == depth - 1, j, 0), 0))
    hbm = pl.BlockSpec(memory_space=pl.ANY)
    in_specs = ([smem, x_blk, cache_blk, cache_blk, rows_blk(CONV_WIDTH)]
                + [_layer_spec(wts[name], lambda l, j: l) for name in SMALL_WEIGHT_NAMES]
                + [hbm] * len(BIG_WEIGHT_NAMES))
    out_shape = (jax.ShapeDtypeStruct((n, d), F32),
                 jax.ShapeDtypeStruct(ck.shape, F32),
                 jax.ShapeDtypeStruct(cv.shape, F32),
                 jax.ShapeDtypeStruct((depth, n, CONV_WIDTH), F32),
                 jax.ShapeDtypeStruct((depth, n, MLP_WIDTH), F32)
                 ) + tuple(jax.ShapeDtypeStruct(wts[name].shape, BF16) for name in BIG_WEIGHT_NAMES)
    out_specs = ((y_blk, cache_blk, cache_blk, rows_blk(CONV_WIDTH), rows_blk(MLP_WIDTH))
                 + (hbm,) * len(BIG_WEIGHT_NAMES))
    scratch = [pltpu.VMEM((n, d), F32),
               pltpu.VMEM((stack, LANES), F32),
               pltpu.VMEM((nseq, stack, LANES), F32),
               pltpu.VMEM((nseq, stack, LANES), F32),
               pltpu.VMEM((nseq, stack, LANES), F32),
               pltpu.VMEM((nseq, stack, LANES), F32),
               pltpu.VMEM((nseq, stack, LANES), F32)]
    scratch += [pltpu.VMEM(wts[name].shape[1:], BF16) for name in BIG_WEIGHT_NAMES]
    scratch += [pltpu.VMEM((STAGE_SLOTS, STAGE_ROWS_W_IN, IN_COLS), F32),
                pltpu.VMEM((STAGE_SLOTS, STAGE_ROWS_SQUARE, d), F32),
                pltpu.SemaphoreType.DMA((STAGE_SLOTS,)), pltpu.SemaphoreType.DMA((STAGE_SLOTS,)),
                pltpu.SemaphoreType.DMA((len(BIG_WEIGHT_NAMES),))]
    return pl.pallas_call(
        functools.partial(_sample_kernel, dec_seq=dec_seq),
        grid=(depth, n // rows),
        in_specs=in_specs, out_specs=out_specs, out_shape=out_shape, scratch_shapes=scratch,
        compiler_params=pltpu.CompilerParams(dimension_semantics=("arbitrary", "arbitrary"),
                                             vmem_limit_bytes=VMEM_LIMIT_BYTES),
        name="sample_layers",
    )(wts["sinks"], x, ck, cv, st, *[wts[name] for name in SMALL_WEIGHT_NAMES + BIG_WEIGHT_NAMES])


def kernel(x_prompt, x_sample, cache_k, cache_v, state_conv, norm_w, w_in, b_gate, q_norm_w, k_norm_w,
           sinks, conv_w, v_norm_w, w_spatial, b_spatial, w_out_a, w_out_b, w_out_c, w_o):
    batch, seq, d = x_prompt.shape
    dec_batch, dec_seq, _ = x_sample.shape
    depth = w_in.shape[0]
    w_buf = cache_k.shape[2]
    assert d == D_MODEL and seq % PROMPT_ROWS == 0 and (dec_batch * dec_seq) % SAMPLE_ROWS == 0
    assert w_buf == WINDOW and dec_seq == SUBLANES and w_in.shape[2] == IN_COLS

    gw = MLP_WIDTH // N_SPATIAL_GROUPS
    reps = CHUNK // dec_seq
    spb_p = jnp.repeat(jnp.swapaxes(b_spatial, 1, 2), gw, axis=2)
    common = {
        "sinks": sinks,
        "norm_w": norm_w[:, None, :], "w_in": w_in, "b_gate": b_gate[:, None, :],
        "qw": jnp.tile(q_norm_w, (1, LANES // HEAD_DIM))[:, None, :],
        "kw": jnp.tile(k_norm_w, (1, LANES // HEAD_DIM))[:, None, :],
        "conv_w": conv_w, "vnw": v_norm_w[:, None, :],
        "woa": w_out_a, "wob": w_out_b, "woc": w_out_c, "wo": w_o,
    }
    wts_s = dict(common,
                 mix=jnp.tile(w_spatial[:, :, :dec_seq, :dec_seq], (1, 1, 1, reps)),
                 spb=spb_p[:, :dec_seq, :])

    st_rows = jnp.pad(state_conv, ((0, 0), (0, 0), (0, dec_seq - (CONV_K - 1)), (0, 0)))
    st_rows = st_rows.reshape(depth, dec_batch * dec_seq, CONV_WIDTH)
    ck = jnp.transpose(cache_k, (0, 1, 3, 4, 2))
    cv = jnp.transpose(cache_v, (0, 1, 3, 4, 2))

    ys, sk, sv, ci, scv, *big_bf16 = _sample_layers(x_sample.reshape(dec_batch * dec_seq, d), ck, cv,
                                                    st_rows, wts_s, dec_seq)
    wts_p = dict(common, mix=w_spatial, spb=spb_p, **dict(zip(BIG_WEIGHT_NAMES, big_bf16)))
    sk = jnp.transpose(sk, (0, 1, 4, 2, 3))
    sv = jnp.transpose(sv, (0, 1, 4, 2, 3))
    sc = ci.reshape(depth, dec_batch, dec_seq, CONV_WIDTH)[:, :, dec_seq - (CONV_K - 1):, :]
    scv = scv.reshape(depth, dec_batch, dec_seq, MLP_WIDTH)

    yp = x_prompt.reshape(batch * seq, d)
    pk, pv, pc = [], [], []
    for l in range(depth):
        yp, k_l, v_l, c_l = _prompt_layer(yp, wts_p, l, batch)
        pk.append(k_l.reshape(batch, WINDOW, N_KV_HEADS, HEAD_DIM))
        pv.append(v_l.reshape(batch, WINDOW, N_KV_HEADS, HEAD_DIM))
        pc.append(c_l[:, SUBLANES - (CONV_K - 1):, :])

    return (yp.reshape(batch, seq, d), ys.reshape(dec_batch, dec_seq, d), jnp.stack(pk), jnp.stack(pv),
            jnp.stack(pc), sk, sv, sc, scv)
```

```python
import functools

import jax
import jax.numpy as jnp
from jax import lax
from jax.experimental import pallas as pl
from jax.experimental.pallas import tpu as pltpu

F32 = jnp.float32
BF16 = jnp.bfloat16

D_MODEL = 1024
N_HEADS = 8
N_KV_HEADS = 2
HEAD_DIM = 64
Q_GROUP = N_HEADS // N_KV_HEADS
ATTN_WIDTH = N_HEADS * HEAD_DIM
KV_WIDTH = N_KV_HEADS * HEAD_DIM
WINDOW = 128
CONV_WIDTH = 512
CONV_K = 3
CHUNK = 128
MLP_WIDTH = 512
N_SPATIAL_GROUPS = 4
EPS = 1e-6
NEG_INF = -1e30

COL_A = 0
COL_B = COL_A + 2 * ATTN_WIDTH + 2 * KV_WIDTH
COL_C = COL_B + 4 * CONV_WIDTH
COL_G = COL_C + 3 * MLP_WIDTH
IN_COLS = COL_G + 3 * D_MODEL

LANES = 128
SUBLANES = 8
PROMPT_ROWS = 512
SAMPLE_ROWS = 128
VMEM_LIMIT_BYTES = 56 * 1024 * 1024


def _dot(a, b):
    return jnp.dot(a, b, preferred_element_type=F32)


def _dot_nt(a, b):
    return lax.dot_general(a, b, (((1,), (1,)), ((), ())), preferred_element_type=F32)


def _rms(x, w):
    ms = jnp.mean(x * x, axis=-1, keepdims=True)
    return (x * lax.rsqrt(ms + EPS)) * w


def _sigmoid(x):
    return 0.5 * jnp.tanh(0.5 * x) + 0.5


def _silu(x):
    return x * _sigmoid(x)


def _lo_lanes(shape):
    return lax.broadcasted_iota(jnp.int32, shape, len(shape) - 1) < HEAD_DIM


def _pair_rms(x, w):
    lo = _lo_lanes(x.shape)
    sq = x * x
    s_lo = jnp.sum(jnp.where(lo, sq, 0.0), axis=-1, keepdims=True)
    s_hi = jnp.sum(jnp.where(lo, 0.0, sq), axis=-1, keepdims=True)
    ms = jnp.where(lo, s_lo, s_hi) * (1.0 / HEAD_DIM)
    return (x * lax.rsqrt(ms + EPS)) * w


def _norm_heads(x, w):
    groups = [_pair_rms(x[:, g * LANES:(g + 1) * LANES], w) for g in range(x.shape[1] // LANES)]
    return groups[0] if len(groups) == 1 else jnp.concatenate(groups, axis=1)


def _slope(head):
    return 2.0 ** (-(head + 1))


def _branch_b(hb, conv_shift, conv_w_ref, wob_ref):
    gate_b = hb[:, 0:CONV_WIDTH]
    gate_c = hb[:, CONV_WIDTH:2 * CONV_WIDTH]
    h_b = hb[:, 2 * CONV_WIDTH:3 * CONV_WIDTH]
    z_b = hb[:, 3 * CONV_WIDTH:4 * CONV_WIDTH]
    conv_in = gate_c * h_b
    xm1, xm2 = conv_shift(conv_in)
    cw = conv_w_ref[...]
    conv_out = cw[0:1] * xm2 + cw[1:2] * xm1 + cw[2:3] * conv_in
    y_b = gate_b * conv_out
    return _dot((_silu(z_b) * y_b).astype(BF16), wob_ref[...]), conv_in


def _branch_c(hc, mix_mask, mix_rows, spb, vnw_ref, woc_ref):
    rows = hc.shape[0]
    u = hc[:, 0:MLP_WIDTH]
    v_c = _rms(hc[:, MLP_WIDTH:2 * MLP_WIDTH], vnw_ref[...])
    z_c = hc[:, 2 * MLP_WIDTH:3 * MLP_WIDTH]
    vb = v_c.astype(BF16)
    gw = MLP_WIDTH // N_SPATIAL_GROUPS
    mixes = [jnp.where(mix_mask, mix_rows(g), 0.0).astype(BF16) for g in range(N_SPATIAL_GROUPS)]
    sp_chunks = []
    for c in range(rows // CHUNK):
        r0 = c * CHUNK
        parts = [_dot(mixes[g], vb[r0:r0 + CHUNK, g * gw:(g + 1) * gw])
                 for g in range(N_SPATIAL_GROUPS)]
        sp_chunks.append(jnp.concatenate(parts, axis=1) + spb)
    sp = sp_chunks[0] if len(sp_chunks) == 1 else jnp.concatenate(sp_chunks, axis=0)
    y_c = u * sp
    return _dot((_silu(z_c) * y_c).astype(BF16), woc_ref[...]), v_c


def _branch_a(y_a, z_a, woa_ref):
    return _dot((_silu(z_a) * y_a).astype(BF16), woa_ref[...])


def _merge(x, gated_a, gated_b, gated_c, wo_ref):
    m = gated_a + gated_b + gated_c
    return x + _dot(m.astype(BF16), wo_ref[...])


def _prompt_kernel(sinks_ref, x_ref, norm_w_ref, w_in_ref, b_gate_ref, qw_ref, kw_ref, conv_w_ref,
                   vnw_ref, mix_ref, spb_ref, woa_ref, wob_ref, woc_ref, wo_ref,
                   y_ref, pk_ref, pv_ref, pc_ref,
                   bias_scr, ka_scr, kb_scr, vt_scr, carry_scr, *, rows, nblk, layer):
    i = pl.program_id(0)
    first = lax.rem(i, nblk) == 0
    nsub = rows // WINDOW
    stack = 2 * WINDOW

    @pl.when(i == 0)
    def _():
        key = lax.broadcasted_iota(jnp.int32, (2 * WINDOW, stack), 0)
        c = lax.broadcasted_iota(jnp.int32, (2 * WINDOW, stack), 1)
        left = c < WINDOW
        dist = jnp.where(left, c, c - WINDOW) + WINDOW - key
        band = (dist >= 0) & (dist < WINDOW)
        distf = dist.astype(F32)
        for flag in range(2):
            valid = band & (key >= WINDOW) if flag else band
            for h in range(N_KV_HEADS):
                for half in range(2):
                    slope = jnp.where(left, _slope(h * Q_GROUP + half), _slope(h * Q_GROUP + 2 + half))
                    bias_scr[flag * 4 + h * 2 + half] = jnp.where(valid, -(slope * distf), NEG_INF)

    @pl.when(first)
    def _():
        zeros = jnp.zeros((WINDOW, KV_WIDTH), BF16)
        ka_scr[0:WINDOW, :] = zeros
        kb_scr[0:WINDOW, :] = zeros
        vt_scr[:, 0:WINDOW] = zeros
        carry_scr[...] = jnp.zeros(carry_scr.shape, F32)

    x = x_ref[...]
    xn = _rms(x, norm_w_ref[...]).astype(BF16)
    def proj(c0, c1):
        return _dot(xn, w_in_ref[:, c0:c1])

    def gate(idx):
        g = proj(COL_G + idx * D_MODEL, COL_G + (idx + 1) * D_MODEL)
        return _sigmoid(g + b_gate_ref[:, idx * D_MODEL:(idx + 1) * D_MODEL])

    ha = proj(COL_A, COL_B)
    hb = proj(COL_B, COL_C)
    q = _norm_heads(ha[:, 0:ATTN_WIDTH], qw_ref[...]) * (HEAD_DIM ** -0.5)
    k = _pair_rms(ha[:, ATTN_WIDTH:ATTN_WIDTH + KV_WIDTH], kw_ref[...])
    v = ha[:, ATTN_WIDTH + KV_WIDTH:ATTN_WIDTH + 2 * KV_WIDTH]
    z_a = ha[:, ATTN_WIDTH + 2 * KV_WIDTH:COL_B]

    pk_ref[0] = k[rows - WINDOW:rows, :]
    pv_ref[0] = v[rows - WINDOW:rows, :]
    ka_scr[WINDOW:WINDOW + rows, :] = k.astype(BF16)
    kb_scr[WINDOW:WINDOW + rows, :] = pltpu.roll(k, HEAD_DIM, axis=1).astype(BF16)
    for c in range(nsub):
        vt_scr[:, (c + 1) * WINDOW:(c + 2) * WINDOW] = jnp.transpose(
            v[c * WINDOW:(c + 1) * WINDOW, :]).astype(BF16)

    lo = _lo_lanes((WINDOW, LANES))
    scol = lax.broadcasted_iota(jnp.int32, (1, stack), 1) < WINDOW
    flag = first.astype(jnp.int32)
    carry = carry_scr[...]
    prev1 = carry[SUBLANES - 1:SUBLANES, :]
    prev2 = carry[SUBLANES - 2:SUBLANES - 1, :]

    def conv_shift(ci):
        rid = lax.broadcasted_iota(jnp.int32, ci.shape, 0)
        xm1 = jnp.where(rid == 0, prev1, pltpu.roll(ci, 1, axis=0))
        xm2 = jnp.where(rid == 0, prev2, jnp.where(rid == 1, prev1, pltpu.roll(ci, 2, axis=0)))
        return xm1, xm2

    mr = lax.broadcasted_iota(jnp.int32, (CHUNK, CHUNK), 0)
    mc = lax.broadcasted_iota(jnp.int32, (CHUNK, CHUNK), 1)
    done = {}
    stages = [
        lambda: done.update(hc=proj(COL_C, COL_G)),
        lambda: done.update(b=_branch_b(hb, conv_shift, conv_w_ref, wob_ref)),
        lambda: done.update(gated_b=gate(1) * done["b"][0]),
        lambda: done.update(c=_branch_c(done["hc"], mc <= mr, lambda g: mix_ref[g], spb_ref[...],
                                        vnw_ref, woc_ref)),
        lambda: done.update(gated_c=gate(2) * done["c"][0]),
        lambda: done.update(g0=gate(0)),
    ]
    slots = 2 * nsub
    order = list(range(0, slots, 2)) + list(range(1, slots, 2))
    per_slot = [0] * slots
    for n in range(len(stages)):
        per_slot[order[n % slots]] += 1

    def run_stages(slot):
        for _ in range(per_slot[slot]):
            stages.pop(0)()

    y_rows = []
    for c in range(nsub):
        r0 = c * WINDOW
        keys = slice(r0, r0 + 2 * WINDOW)
        k_nat, k_swp = ka_scr[keys, :], kb_scr[keys, :]
        vt = vt_scr[:, keys]
        scores = []
        for h in range(N_KV_HEADS):
            p0 = q[r0:r0 + WINDOW, (2 * h) * LANES:(2 * h + 1) * LANES]
            p1 = q[r0:r0 + WINDOW, (2 * h + 1) * LANES:(2 * h + 2) * LANES]
            for half in range(2):
                keep = lo if half == 0 else jnp.logical_not(lo)
                qs = jnp.concatenate([jnp.where(keep, p0, 0.0), jnp.where(keep, p1, 0.0)],
                                     axis=0).astype(BF16)
                kk = k_nat if h == half else k_swp
                tbl = h * 2 + half
                bias = bias_scr[flag * 4 + tbl] if c == 0 else bias_scr[tbl]
                scores.append(_dot_nt(kk, qs) + bias)
        run_stages(2 * c)
        pairs = []
        for h in range(N_KV_HEADS):
            outs = []
            for half in range(2):
                s = scores[h * 2 + half]
                sink = jnp.where(scol, sinks_ref[layer, h * Q_GROUP + half],
                                 sinks_ref[layer, h * Q_GROUP + 2 + half])
                mx = jnp.maximum(jnp.max(s, axis=0, keepdims=True), sink)
                e = jnp.exp(s - mx)
                den = jnp.sum(e, axis=0, keepdims=True) + jnp.exp(sink - mx)
                outs.append(_dot(vt[h * HEAD_DIM:(h + 1) * HEAD_DIM, :], e.astype(BF16)) * (1.0 / den))
            for p in range(2):
                cols = slice(p * WINDOW, (p + 1) * WINDOW)
                pairs.append(jnp.transpose(jnp.concatenate([outs[0][:, cols], outs[1][:, cols]], axis=0)))
        y_rows.append(jnp.concatenate(pairs, axis=1))
        run_stages(2 * c + 1)
    y_a = y_rows[0] if nsub == 1 else jnp.concatenate(y_rows, axis=0)

    ka_scr[0:WINDOW, :] = ka_scr[rows:rows + WINDOW, :]
    kb_scr[0:WINDOW, :] = kb_scr[rows:rows + WINDOW, :]
    vt_scr[:, 0:WINDOW] = vt_scr[:, rows:rows + WINDOW]

    gated_a = done["g0"] * _branch_a(y_a, z_a, woa_ref)
    y_ref[...] = _merge(x, gated_a, done["gated_b"], done["gated_c"], wo_ref)
    conv_in = done["b"][1]
    last = conv_in[rows - SUBLANES:rows, :]
    carry_scr[...] = last
    pc_ref[0] = last


def _stage_bf16(src_hbm, dst_scr, stage, sem, layer):
    nslots, chunk_rows = stage.shape[0], stage.shape[1]
    nchunks = dst_scr.shape[0] // chunk_rows
    ahead = nslots - 1

    def copy(k):
        return pltpu.make_async_copy(src_hbm.at[layer, pl.ds(k * chunk_rows, chunk_rows), :],
                                     stage.at[k % nslots], sem.at[k % nslots])

    for k in range(min(ahead, nchunks)):
        copy(k).start()
    for k in range(nchunks):
        if k + ahead < nchunks:
            copy(k + ahead).start()
        copy(k).wait()
        dst_scr[k * chunk_rows:(k + 1) * chunk_rows, :] = stage[k % nslots].astype(BF16)


def _sample_kernel(sinks_ref, x_ref, ck_ref, cv_ref, st_ref, norm_w_ref, b_gate_ref, qw_ref,
                   kw_ref, conv_w_ref, vnw_ref, mix_ref, spb_ref,
                   w_in_hbm, woa_hbm, wob_hbm, woc_hbm, wo_hbm,
                   y_ref, sk_ref, sv_ref, ci_ref, scv_ref,
                   w_in_out, woa_out, wob_out, woc_out, wo_out,
                   x_scr, bold_scr, bnew_scr, q_scr, en_scr, oo_scr, inv_scr,
                   w_in_ref, woa_ref, wob_ref, woc_ref, wo_ref, stage_in, stage_sq,
                   sem_in, sem_sq, sem_out, *, dec_seq):
    layer = pl.program_id(0)
    j = pl.program_id(1)
    rows = SAMPLE_ROWS
    nseq = rows // dec_seq
    stack = N_HEADS * dec_seq
    w = WINDOW

    @pl.when((layer == 0) & (j == 0))
    def _():
        r = lax.broadcasted_iota(jnp.int32, (stack, LANES), 0)
        col = lax.broadcasted_iota(jnp.int32, (stack, LANES), 1)
        tok = lax.rem(r, dec_seq)
        head = r // dec_seq
        slope = jnp.zeros((stack, LANES), F32)
        for n in range(N_HEADS):
            slope = jnp.where(head == n, _slope(n), slope)
        dist = tok + w - col
        bold_scr[...] = jnp.where(dist < WINDOW, -(slope * dist.astype(F32)), NEG_INF)
        kseq = col // dec_seq
        dist = tok - lax.rem(col, dec_seq)
        pen = -(slope * dist.astype(F32))
        for b in range(nseq):
            bnew_scr[b] = jnp.where((kseq == b) & (dist >= 0), pen, NEG_INF)

    r0 = pl.multiple_of(j * rows, rows)

    resident = ((w_in_hbm, w_in_ref, w_in_out), (woa_hbm, woa_ref, woa_out), (wob_hbm, wob_ref, wob_out),
                (woc_hbm, woc_ref, woc_out), (wo_hbm, wo_ref, wo_out))

    def publish(n, lyr):
        return pltpu.make_async_copy(resident[n][1], resident[n][2].at[lyr], sem_out.at[n])

    @pl.when(j == 0)
    def _():
        @pl.when(layer > 0)
        def _():
            for n in range(len(resident)):
                publish(n, layer - 1).wait()

        _stage_bf16(w_in_hbm, w_in_ref, stage_in, sem_in, layer)
        for src, dst, _ in resident[1:]:
            _stage_bf16(src, dst, stage_sq, sem_sq, layer)
        for n in range(len(resident)):
            publish(n, layer).start()

    @pl.when(layer == 0)
    def _():
        x_scr[pl.ds(r0, rows), :] = x_ref[...]

    x = x_scr[pl.ds(r0, rows), :]
    xn = _rms(x, norm_w_ref[...]).astype(BF16)
    def proj(c0, c1):
        return _dot(xn, w_in_ref[:, c0:c1])

    def gate(idx):
        g = proj(COL_G + idx * D_MODEL, COL_G + (idx + 1) * D_MODEL)
        return _sigmoid(g + b_gate_ref[:, idx * D_MODEL:(idx + 1) * D_MODEL])

    ha = proj(COL_A, COL_B)
    hb = proj(COL_B, COL_C)
    q = _norm_heads(ha[:, 0:ATTN_WIDTH], qw_ref[...]) * (HEAD_DIM ** -0.5)
    k = _pair_rms(ha[:, ATTN_WIDTH:ATTN_WIDTH + KV_WIDTH], kw_ref[...])
    v = ha[:, ATTN_WIDTH + KV_WIDTH:ATTN_WIDTH + 2 * KV_WIDTH]
    z_a = ha[:, ATTN_WIDTH + 2 * KV_WIDTH:COL_B]

    lo = _lo_lanes((rows, LANES))
    for h in range(N_KV_HEADS):
        keep = lo if h == 0 else jnp.logical_not(lo)
        for g in range(Q_GROUP):
            pair = q[:, (h * 2 + g // 2) * LANES:(h * 2 + g // 2 + 1) * LANES]
            if g % 2 != h:
                pair = pltpu.roll(pair, HEAD_DIM, axis=1)
            piece = jnp.where(keep, pair, 0.0).reshape(nseq, dec_seq, LANES)
            s0 = (h * Q_GROUP + g) * dec_seq
            q_scr[:, s0:s0 + dec_seq, :] = piece

    kt_new = jnp.transpose(k)
    vt_new = jnp.transpose(v)
    qall = q_scr[...].reshape(nseq * stack, LANES).astype(BF16)
    s_new_all = _dot(qall, kt_new.astype(BF16)).reshape(nseq, stack, LANES)

    srow = lax.broadcasted_iota(jnp.int32, (stack, 1), 0) // dec_seq
    sink = jnp.zeros((stack, 1), F32)
    for n in range(N_HEADS):
        sink = jnp.where(srow == n, sinks_ref[layer, n], sink)
    bias_old = bold_scr[...]
    lane = lax.broadcasted_iota(jnp.int32, (KV_WIDTH, w), 1)
    s_olds = [_dot(q_scr[b].astype(BF16), ck_ref[b].reshape(KV_WIDTH, w).astype(BF16)) + bias_old
              for b in range(nseq)]
    hc = proj(COL_C, COL_G)
    gates = []
    for b in range(nseq):
        if b % (nseq // 4) == 0 and 0 < b:
            gates.append(gate(len(gates)))
        kt = ck_ref[b].reshape(KV_WIDTH, w)
        vt = cv_ref[b].reshape(KV_WIDTH, w)
        s_old = s_olds[b]
        s_new = s_new_all[b] + bnew_scr[b]
        mx = jnp.maximum(jnp.maximum(jnp.max(s_old, axis=-1, keepdims=True),
                                     jnp.max(s_new, axis=-1, keepdims=True)), sink)
        e_old = jnp.exp(s_old - mx)
        e_new = jnp.exp(s_new - mx)
        den = (jnp.sum(e_old, axis=-1, keepdims=True) + jnp.sum(e_new, axis=-1, keepdims=True)
               + jnp.exp(sink - mx))
        en_scr[b] = e_new
        oo_scr[b] = _dot_nt(e_old.astype(BF16), vt.astype(BF16))
        inv_scr[b] = jnp.broadcast_to(1.0 / den, (stack, LANES))
        shift = w - dec_seq - b * dec_seq
        newk = pltpu.roll(kt_new, shift, axis=1) if shift else kt_new
        newv = pltpu.roll(vt_new, shift, axis=1) if shift else vt_new
        keep_old = lane < w - dec_seq
        sk_ref[b] = jnp.where(keep_old, pltpu.roll(kt, w - dec_seq, axis=1), newk).reshape(
            N_KV_HEADS, HEAD_DIM, w)
        sv_ref[b] = jnp.where(keep_old, pltpu.roll(vt, w - dec_seq, axis=1), newv).reshape(
            N_KV_HEADS, HEAD_DIM, w)

    o_new = _dot(en_scr[...].reshape(nseq * stack, LANES).astype(BF16), v.astype(BF16))
    o = (oo_scr[...] + o_new.reshape(nseq, stack, LANES)) * inv_scr[...]
    pairs = []
    for h in range(N_KV_HEADS):
        heads = []
        for g in range(Q_GROUP):
            s0 = (h * Q_GROUP + g) * dec_seq
            og = o[:, s0:s0 + dec_seq, :].reshape(rows, LANES)
            if g % 2 != h:
                og = pltpu.roll(og, HEAD_DIM, axis=1)
            heads.append(og)
        pairs.append(jnp.where(lo, heads[0], heads[1]))
        pairs.append(jnp.where(lo, heads[2], heads[3]))
    y_a = jnp.concatenate(pairs, axis=1)

    st = st_ref[...]

    def conv_shift(ci):
        t = lax.rem(lax.broadcasted_iota(jnp.int32, ci.shape, 0), dec_seq)
        xm1 = jnp.where(t == 0, pltpu.roll(st, rows - 1, axis=0), pltpu.roll(ci, 1, axis=0))
        xm2 = jnp.where(t < CONV_K - 1, st, pltpu.roll(ci, 2, axis=0))
        return xm1, xm2

    mr = lax.broadcasted_iota(jnp.int32, (CHUNK, CHUNK), 0)
    mc = lax.broadcasted_iota(jnp.int32, (CHUNK, CHUNK), 1)
    mix_mask = (mr // dec_seq == mc // dec_seq) & (mc <= mr)
    b_out, conv_in = _branch_b(hb, conv_shift, conv_w_ref, wob_ref)
    reps = CHUNK // dec_seq

    def mix_rows(g):
        return jnp.broadcast_to(mix_ref[g][None], (reps, dec_seq, CHUNK)).reshape(CHUNK, CHUNK)

    spb = jnp.broadcast_to(spb_ref[...][None], (reps, dec_seq, MLP_WIDTH)).reshape(CHUNK, MLP_WIDTH)
    c_out, v_c = _branch_c(hc, mix_mask, mix_rows, spb, vnw_ref, woc_ref)
    y = _merge(x, gates[0] * _branch_a(y_a, z_a, woa_ref), gates[1] * b_out, gates[2] * c_out, wo_ref)
    x_scr[pl.ds(r0, rows), :] = y

    @pl.when(layer == pl.num_programs(0) - 1)
    def _():
        y_ref[...] = y
    ci_ref[...] = conv_in
    scv_ref[...] = v_c

    @pl.when((layer == pl.num_programs(0) - 1) & (j == pl.num_programs(1) - 1))
    def _():
        for n in range(len(resident)):
            publish(n, layer).wait()


SMALL_WEIGHT_NAMES = ("norm_w", "b_gate", "qw", "kw", "conv_w", "vnw", "mix", "spb")
BIG_WEIGHT_NAMES = ("w_in", "woa", "wob", "woc", "wo")
STAGE_SLOTS = 8
STAGE_ROWS_W_IN = 32
STAGE_ROWS_SQUARE = 128
WEIGHT_NAMES = ("norm_w", "w_in", "b_gate", "qw", "kw", "conv_w", "vnw", "mix", "spb",
                "woa", "wob", "woc", "wo")


def _layer_spec(arr, layer_of):
    nd = arr.ndim - 1
    return pl.BlockSpec((None,) + arr.shape[1:], lambda *g, _nd=nd: (layer_of(*g),) + (0,) * _nd,
                        pipeline_mode=pl.Buffered(1))


def _prompt_layer(x, wts, layer, batch):
    n, d = x.shape
    rows = PROMPT_ROWS
    nblk = n // batch // rows
    smem = pl.BlockSpec(memory_space=pltpu.SMEM)
    in_specs = ([smem, pl.BlockSpec((rows, d), lambda i: (i, 0))]
                + [_layer_spec(wts[name], lambda i: layer) for name in WEIGHT_NAMES])
    out_shape = (jax.ShapeDtypeStruct((n, d), F32),
                 jax.ShapeDtypeStruct((batch, WINDOW, KV_WIDTH), F32),
                 jax.ShapeDtypeStruct((batch, WINDOW, KV_WIDTH), F32),
                 jax.ShapeDtypeStruct((batch, SUBLANES, CONV_WIDTH), F32))
    out_specs = (pl.BlockSpec((rows, d), lambda i: (i, 0)),
                 pl.BlockSpec((1, WINDOW, KV_WIDTH), lambda i: (i // nblk, 0, 0)),
                 pl.BlockSpec((1, WINDOW, KV_WIDTH), lambda i: (i // nblk, 0, 0)),
                 pl.BlockSpec((1, SUBLANES, CONV_WIDTH), lambda i: (i // nblk, 0, 0)))
    n_tables = 2 * N_KV_HEADS * 2
    scratch = [pltpu.VMEM((n_tables, 2 * WINDOW, 2 * WINDOW), F32)]
    scratch += [pltpu.VMEM((WINDOW + rows, KV_WIDTH), BF16) for _ in range(2)]
    scratch += [pltpu.VMEM((KV_WIDTH, WINDOW + rows), BF16)]
    scratch += [pltpu.VMEM((SUBLANES, CONV_WIDTH), F32)]
    return pl.pallas_call(
        functools.partial(_prompt_kernel, rows=rows, nblk=nblk, layer=layer),
        grid=(n // rows,),
        in_specs=in_specs, out_specs=out_specs, out_shape=out_shape, scratch_shapes=scratch,
        compiler_params=pltpu.CompilerParams(dimension_semantics=("arbitrary",),
                                             vmem_limit_bytes=VMEM_LIMIT_BYTES),
        name="prompt_layer",
    )(wts["sinks"], x, *[wts[name] for name in WEIGHT_NAMES])


def _sample_layers(x, ck, cv, st, wts, dec_seq):
    n, d = x.shape
    depth = ck.shape[0]
    rows = SAMPLE_ROWS
    nseq = rows // dec_seq
    stack = N_HEADS * dec_seq
    w = ck.shape[-1]
    smem = pl.BlockSpec(memory_space=pltpu.SMEM)
    cache_blk = pl.BlockSpec((None, nseq, N_KV_HEADS, HEAD_DIM, w), lambda l, j: (l, j, 0, 0, 0))
    rows_blk = lambda width: pl.BlockSpec((None, rows, width), lambda l, j: (l, j, 0))
    nblk = n // rows
    x_blk = pl.BlockSpec((rows, d), lambda l, j: (jnp.where(l == 0, j, nblk - 1), 0))
    y_blk = pl.BlockSpec((rows, d), lambda l, j: (jnp.where(l == depth - 1, j, 0), 0))
    hbm = pl.BlockSpec(memory_space=pl.ANY)
    in_specs = ([smem, x_blk, cache_blk, cache_blk, rows_blk(CONV_WIDTH)]
                + [_layer_spec(wts[name], lambda l, j: l) for name in SMALL_WEIGHT_NAMES]
                + [hbm] * len(BIG_WEIGHT_NAMES))
    out_shape = (jax.ShapeDtypeStruct((n, d), F32),
                 jax.ShapeDtypeStruct(ck.shape, F32),
                 jax.ShapeDtypeStruct(cv.shape, F32),
                 jax.ShapeDtypeStruct((depth, n, CONV_WIDTH), F32),
                 jax.ShapeDtypeStruct((depth, n, MLP_WIDTH), F32)
                 ) + tuple(jax.ShapeDtypeStruct(wts[name].shape, BF16) for name in BIG_WEIGHT_NAMES)
    out_specs = ((y_blk, cache_blk, cache_blk, rows_blk(CONV_WIDTH), rows_blk(MLP_WIDTH))
                 + (hbm,) * len(BIG_WEIGHT_NAMES))
    scratch = [pltpu.VMEM((n, d), F32),
               pltpu.VMEM((stack, LANES), F32),
               pltpu.VMEM((nseq, stack, LANES), F32),
               pltpu.VMEM((nseq, stack, LANES), F32),
               pltpu.VMEM((nseq, stack, LANES), F32),
               pltpu.VMEM((nseq, stack, LANES), F32),
               pltpu.VMEM((nseq, stack, LANES), F32)]
    scratch += [pltpu.VMEM(wts[name].shape[1:], BF16) for name in BIG_WEIGHT_NAMES]
    scratch += [pltpu.VMEM((STAGE_SLOTS, STAGE_ROWS_W_IN, IN_COLS), F32),
                pltpu.VMEM((STAGE_SLOTS, STAGE_ROWS_SQUARE, d), F32),
                pltpu.SemaphoreType.DMA((STAGE_SLOTS,)), pltpu.SemaphoreType.DMA((STAGE_SLOTS,)),
                pltpu.SemaphoreType.DMA((len(BIG_WEIGHT_NAMES),))]
    return pl.pallas_call(
        functools.partial(_sample_kernel, dec_seq=dec_seq),
        grid=(depth, n // rows),
        in_specs=in_specs, out_specs=out_specs, out_shape=out_shape, scratch_shapes=scratch,
        compiler_params=pltpu.CompilerParams(dimension_semantics=("arbitrary", "arbitrary"),
                                             vmem_limit_bytes=VMEM_LIMIT_BYTES),
        name="sample_layers",
    )(wts["sinks"], x, ck, cv, st, *[wts[name] for name in SMALL_WEIGHT_NAMES + BIG_WEIGHT_NAMES])


def kernel(x_prompt, x_sample, cache_k, cache_v, state_conv, norm_w, w_in, b_gate, q_norm_w, k_norm_w,
           sinks, conv_w, v_norm_w, w_spatial, b_spatial, w_out_a, w_out_b, w_out_c, w_o):
    batch, seq, d = x_prompt.shape
    dec_batch, dec_seq, _ = x_sample.shape
    depth = w_in.shape[0]
    w_buf = cache_k.shape[2]
    assert d == D_MODEL and seq % PROMPT_ROWS == 0 and (dec_batch * dec_seq) % SAMPLE_ROWS == 0
    assert w_buf == WINDOW and dec_seq == SUBLANES and w_in.shape[2] == IN_COLS

    gw = MLP_WIDTH // N_SPATIAL_GROUPS
    reps = CHUNK // dec_seq
    spb_p = jnp.repeat(jnp.swapaxes(b_spatial, 1, 2), gw, axis=2)
    common = {
        "sinks": sinks,
        "norm_w": norm_w[:, None, :], "w_in": w_in, "b_gate": b_gate[:, None, :],
        "qw": jnp.tile(q_norm_w, (1, LANES // HEAD_DIM))[:, None, :],
        "kw": jnp.tile(k_norm_w, (1, LANES // HEAD_DIM))[:, None, :],
        "conv_w": conv_w, "vnw": v_norm_w[:, None, :],
        "woa": w_out_a, "wob": w_out_b, "woc": w_out_c, "wo": w_o,
    }
    wts_s = dict(common,
                 mix=jnp.tile(w_spatial[:, :, :dec_seq, :dec_seq], (1, 1, 1, reps)),
                 spb=spb_p[:, :dec_seq, :])

    st_rows = jnp.pad(state_conv, ((0, 0), (0, 0), (0, dec_seq - (CONV_K - 1)), (0, 0)))
    st_rows = st_rows.reshape(depth, dec_batch * dec_seq, CONV_WIDTH)
    ck = jnp.transpose(cache_k, (0, 1, 3, 4, 2))
    cv = jnp.transpose(cache_v, (0, 1, 3, 4, 2))

    ys, sk, sv, ci, scv, *big_bf16 = _sample_layers(x_sample.reshape(dec_batch * dec_seq, d), ck, cv,
                                                    st_rows, wts_s, dec_seq)
    wts_p = dict(common, mix=w_spatial, spb=spb_p, **dict(zip(BIG_WEIGHT_NAMES, big_bf16)))
    sk = jnp.transpose(sk, (0, 1, 4, 2, 3))
    sv = jnp.transpose(sv, (0, 1, 4, 2, 3))
    sc = ci.reshape(depth, dec_batch, dec_seq, CONV_WIDTH)[:, :, dec_seq - (CONV_K - 1):, :]
    scv = scv.reshape(depth, dec_batch, dec_seq, MLP_WIDTH)

    yp = x_prompt.reshape(batch * seq, d)
    pk, pv, pc = [], [], []
    for l in range(depth):
        yp, k_l, v_l, c_l = _prompt_layer(yp, wts_p, l, batch)
        pk.append(k_l.reshape(batch, WINDOW, N_KV_HEADS, HEAD_DIM))
        pv.append(v_l.reshape(batch, WINDOW, N_KV_HEADS, HEAD_DIM))
        pc.append(c_l[:, SUBLANES - (CONV_K - 1):, :])

    return (yp.reshape(batch, seq, d), ys.reshape(dec_batch, dec_seq, d), jnp.stack(pk), jnp.stack(pv),
            jnp.stack(pc), sk, sv, sc, scv)
```

```python
import functools

import jax
import jax.numpy as jnp
from jax import lax
from jax.experimental import pallas as pl
from jax.experimental.pallas import tpu as pltpu

F32 = jnp.float32
BF16 = jnp.bfloat16

D_MODEL = 1024
N_HEADS = 8
N_KV_HEADS = 2
HEAD_DIM = 64
Q_GROUP = N_HEADS // N_KV_HEADS
ATTN_WIDTH = N_HEADS * HEAD_DIM
KV_WIDTH = N_KV_HEADS * HEAD_DIM
WINDOW = 128
CONV_WIDTH = 512
CONV_K = 3
CHUNK = 128
MLP_WIDTH = 512
N_SPATIAL_GROUPS = 4
EPS = 1e-6
NEG_INF = -1e30

COL_A = 0
COL_B = COL_A + 2 * ATTN_WIDTH + 2 * KV_WIDTH
COL_C = COL_B + 4 * CONV_WIDTH
COL_G = COL_C + 3 * MLP_WIDTH
IN_COLS = COL_G + 3 * D_MODEL

LANES = 128
SUBLANES = 8
PROMPT_ROWS = 512
SAMPLE_ROWS = 128
VMEM_LIMIT_BYTES = 56 * 1024 * 1024


def _dot(a, b):
    return jnp.dot(a, b, preferred_element_type=F32)


def _dot_nt(a, b):
    return lax.dot_general(a, b, (((1,), (1,)), ((), ())), preferred_element_type=F32)


def _rms(x, w):
    ms = jnp.mean(x * x, axis=-1, keepdims=True)
    return (x * lax.rsqrt(ms + EPS)) * w


def _sigmoid(x):
    return 0.5 * jnp.tanh(0.5 * x) + 0.5


def _silu(x):
    return x * _sigmoid(x)


def _lo_lanes(shape):
    return lax.broadcasted_iota(jnp.int32, shape, len(shape) - 1) < HEAD_DIM


def _pair_rms(x, w):
    lo = _lo_lanes(x.shape)
    sq = x * x
    s_lo = jnp.sum(jnp.where(lo, sq, 0.0), axis=-1, keepdims=True)
    s_hi = jnp.sum(jnp.where(lo, 0.0, sq), axis=-1, keepdims=True)
    ms = jnp.where(lo, s_lo, s_hi) * (1.0 / HEAD_DIM)
    return (x * lax.rsqrt(ms + EPS)) * w


def _norm_heads(x, w):
    groups = [_pair_rms(x[:, g * LANES:(g + 1) * LANES], w) for g in range(x.shape[1] // LANES)]
    return groups[0] if len(groups) == 1 else jnp.concatenate(groups, axis=1)


def _slope(head):
    return 2.0 ** (-(head + 1))


def _branch_b(hb, conv_shift, conv_w_ref, wob_ref):
    gate_b = hb[:, 0:CONV_WIDTH]
    gate_c = hb[:, CONV_WIDTH:2 * CONV_WIDTH]
    h_b = hb[:, 2 * CONV_WIDTH:3 * CONV_WIDTH]
    z_b = hb[:, 3 * CONV_WIDTH:4 * CONV_WIDTH]
    conv_in = gate_c * h_b
    xm1, xm2 = conv_shift(conv_in)
    cw = conv_w_ref[...]
    conv_out = cw[0:1] * xm2 + cw[1:2] * xm1 + cw[2:3] * conv_in
    y_b = gate_b * conv_out
    return _dot((_silu(z_b) * y_b).astype(BF16), wob_ref[...]), conv_in


def _branch_c(hc, mix_mask, mix_rows, spb, vnw_ref, woc_ref):
    rows = hc.shape[0]
    u = hc[:, 0:MLP_WIDTH]
    v_c = _rms(hc[:, MLP_WIDTH:2 * MLP_WIDTH], vnw_ref[...])
    z_c = hc[:, 2 * MLP_WIDTH:3 * MLP_WIDTH]
    vb = v_c.astype(BF16)
    gw = MLP_WIDTH // N_SPATIAL_GROUPS
    mixes = [jnp.where(mix_mask, mix_rows(g), 0.0).astype(BF16) for g in range(N_SPATIAL_GROUPS)]
    sp_chunks = []
    for c in range(rows // CHUNK):
        r0 = c * CHUNK
        parts = [_dot(mixes[g], vb[r0:r0 + CHUNK, g * gw:(g + 1) * gw])
                 for g in range(N_SPATIAL_GROUPS)]
        sp_chunks.append(jnp.concatenate(parts, axis=1) + spb)
    sp = sp_chunks[0] if len(sp_chunks) == 1 else jnp.concatenate(sp_chunks, axis=0)
    y_c = u * sp
    return _dot((_silu(z_c) * y_c).astype(BF16), woc_ref[...]), v_c


def _branch_a(y_a, z_a, woa_ref):
    return _dot((_silu(z_a) * y_a).astype(BF16), woa_ref[...])


def _merge(x, gated_a, gated_b, gated_c, wo_ref):
    m = gated_a + gated_b + gated_c
    return x + _dot(m.astype(BF16), wo_ref[...])


def _prompt_kernel(sinks_ref, x_ref, norm_w_ref, w_in_ref, b_gate_ref, qw_ref, kw_ref, conv_w_ref,
                   vnw_ref, mix_ref, spb_ref, woa_ref, wob_ref, woc_ref, wo_ref,
                   y_ref, pk_ref, pv_ref, pc_ref,
                   bias_scr, ka_scr, kb_scr, vt_scr, carry_scr, *, rows, nblk, layer):
    i = pl.program_id(0)
    first = lax.rem(i, nblk) == 0
    nsub = rows // WINDOW
    stack = 2 * WINDOW

    @pl.when(i == 0)
    def _():
        key = lax.broadcasted_iota(jnp.int32, (2 * WINDOW, stack), 0)
        c = lax.broadcasted_iota(jnp.int32, (2 * WINDOW, stack), 1)
        left = c < WINDOW
        dist = jnp.where(left, c, c - WINDOW) + WINDOW - key
        band = (dist >= 0) & (dist < WINDOW)
        distf = dist.astype(F32)
        for flag in range(2):
            valid = band & (key >= WINDOW) if flag else band
            for h in range(N_KV_HEADS):
                for half in range(2):
                    slope = jnp.where(left, _slope(h * Q_GROUP + half), _slope(h * Q_GROUP + 2 + half))
                    bias_scr[flag * 4 + h * 2 + half] = jnp.where(valid, -(slope * distf), NEG_INF)

    @pl.when(first)
    def _():
        zeros = jnp.zeros((WINDOW, KV_WIDTH), BF16)
        ka_scr[0:WINDOW, :] = zeros
        kb_scr[0:WINDOW, :] = zeros
        vt_scr[:, 0:WINDOW] = zeros
        carry_scr[...] = jnp.zeros(carry_scr.shape, F32)

    x = x_ref[...]
    xn = _rms(x, norm_w_ref[...]).astype(BF16)
    def proj(c0, c1):
        return _dot(xn, w_in_ref[:, c0:c1])

    def gate(idx):
        g = proj(COL_G + idx * D_MODEL, COL_G + (idx + 1) * D_MODEL)
        return _sigmoid(g + b_gate_ref[:, idx * D_MODEL:(idx + 1) * D_MODEL])

    ha = proj(COL_A, COL_B)
    hb = proj(COL_B, COL_C)
    q = _norm_heads(ha[:, 0:ATTN_WIDTH], qw_ref[...]) * (HEAD_DIM ** -0.5)
    k = _pair_rms(ha[:, ATTN_WIDTH:ATTN_WIDTH + KV_WIDTH], kw_ref[...])
    v = ha[:, ATTN_WIDTH + KV_WIDTH:ATTN_WIDTH + 2 * KV_WIDTH]
    z_a = ha[:, ATTN_WIDTH + 2 * KV_WIDTH:COL_B]

    pk_ref[0] = k[rows - WINDOW:rows, :]
    pv_ref[0] = v[rows - WINDOW:rows, :]
    ka_scr[WINDOW:WINDOW + rows, :] = k.astype(BF16)
    kb_scr[WINDOW:WINDOW + rows, :] = pltpu.roll(k, HEAD_DIM, axis=1).astype(BF16)
    for c in range(nsub):
        vt_scr[:, (c + 1) * WINDOW:(c + 2) * WINDOW] = jnp.transpose(
            v[c * WINDOW:(c + 1) * WINDOW, :]).astype(BF16)

    lo = _lo_lanes((WINDOW, LANES))
    scol = lax.broadcasted_iota(jnp.int32, (1, stack), 1) < WINDOW
    flag = first.astype(jnp.int32)
    carry = carry_scr[...]
    prev1 = carry[SUBLANES - 1:SUBLANES, :]
    prev2 = carry[SUBLANES - 2:SUBLANES - 1, :]

    def conv_shift(ci):
        rid = lax.broadcasted_iota(jnp.int32, ci.shape, 0)
        xm1 = jnp.where(rid == 0, prev1, pltpu.roll(ci, 1, axis=0))
        xm2 = jnp.where(rid == 0, prev2, jnp.where(rid == 1, prev1, pltpu.roll(ci, 2, axis=0)))
        return xm1, xm2

    mr = lax.broadcasted_iota(jnp.int32, (CHUNK, CHUNK), 0)
    mc = lax.broadcasted_iota(jnp.int32, (CHUNK, CHUNK), 1)
    done = {}
    stages = [
        lambda: done.update(hc=proj(COL_C, COL_G)),
        lambda: done.update(b=_branch_b(hb, conv_shift, conv_w_ref, wob_ref)),
        lambda: done.update(gated_b=gate(1) * done["b"][0]),
        lambda: done.update(c=_branch_c(done["hc"], mc <= mr, lambda g: mix_ref[g], spb_ref[...],
                                        vnw_ref, woc_ref)),
        lambda: done.update(gated_c=gate(2) * done["c"][0]),
        lambda: done.update(g0=gate(0)),
    ]
    slots = 2 * nsub
    order = list(range(0, slots, 2)) + list(range(1, slots, 2))
    per_slot = [0] * slots
    for n in range(len(stages)):
        per_slot[order[n % slots]] += 1

    def run_stages(slot):
        for _ in range(per_slot[slot]):
            stages.pop(0)()

    y_rows = []
    for c in range(nsub):
        r0 = c * WINDOW
        keys = slice(r0, r0 + 2 * WINDOW)
        k_nat, k_swp = ka_scr[keys, :], kb_scr[keys, :]
        vt = vt_scr[:, keys]
        scores = []
        for h in range(N_KV_HEADS):
            p0 = q[r0:r0 + WINDOW, (2 * h) * LANES:(2 * h + 1) * LANES]
            p1 = q[r0:r0 + WINDOW, (2 * h + 1) * LANES:(2 * h + 2) * LANES]
            for half in range(2):
                keep = lo if half == 0 else jnp.logical_not(lo)
                qs = jnp.concatenate([jnp.where(keep, p0, 0.0), jnp.where(keep, p1, 0.0)],
                                     axis=0).astype(BF16)
                kk = k_nat if h == half else k_swp
                tbl = h * 2 + half
                bias = bias_scr[flag * 4 + tbl] if c == 0 else bias_scr[tbl]
                scores.append(_dot_nt(kk, qs) + bias)
        run_stages(2 * c)
        pairs = []
        for h in range(N_KV_HEADS):
            outs = []
            for half in range(2):
                s = scores[h * 2 + half]
                sink = jnp.where(scol, sinks_ref[layer, h * Q_GROUP + half],
                                 sinks_ref[layer, h * Q_GROUP + 2 + half])
                mx = jnp.maximum(jnp.max(s, axis=0, keepdims=True), sink)
                e = jnp.exp(s - mx)
                den = jnp.sum(e, axis=0, keepdims=True) + jnp.exp(sink - mx)
                outs.append(_dot(vt[h * HEAD_DIM:(h + 1) * HEAD_DIM, :], e.astype(BF16)) * (1.0 / den))
            for p in range(2):
                cols = slice(p * WINDOW, (p + 1) * WINDOW)
                pairs.append(jnp.transpose(jnp.concatenate([outs[0][:, cols], outs[1][:, cols]], axis=0)))
        y_rows.append(jnp.concatenate(pairs, axis=1))
        run_stages(2 * c + 1)
    y_a = y_rows[0] if nsub == 1 else jnp.concatenate(y_rows, axis=0)

    ka_scr[0:WINDOW, :] = ka_scr[rows:rows + WINDOW, :]
    kb_scr[0:WINDOW, :] = kb_scr[rows:rows + WINDOW, :]
    vt_scr[:, 0:WINDOW] = vt_scr[:, rows:rows + WINDOW]

    gated_a = done["g0"] * _branch_a(y_a, z_a, woa_ref)
    y_ref[...] = _merge(x, gated_a, done["gated_b"], done["gated_c"], wo_ref)
    conv_in = done["b"][1]
    last = conv_in[rows - SUBLANES:rows, :]
    carry_scr[...] = last
    pc_ref[0] = last


def _stage_bf16(streams, layer):
    def copy(s, k):
        jobs, stage, sem = streams[s]
        src, _, r0 = jobs[k]
        slot = k % stage.shape[0]
        return pltpu.make_async_copy(src.at[layer, pl.ds(r0, stage.shape[1]), :], stage.at[slot], sem.at[slot])

    for s, (jobs, stage, _) in enumerate(streams):
        for k in range(min(stage.shape[0] - 1, len(jobs))):
            copy(s, k).start()
    for k in range(max(len(jobs) for jobs, _, _ in streams)):
        for s, (jobs, stage, _) in enumerate(streams):
            if k >= len(jobs):
                continue
            nslots, chunk_rows = stage.shape[0], stage.shape[1]
            if k + nslots - 1 < len(jobs):
                copy(s, k + nslots - 1).start()
            copy(s, k).wait()
            _, dst, r0 = jobs[k]
            dst[r0:r0 + chunk_rows, :] = stage[k % nslots].astype(BF16)


def _sample_kernel(sinks_ref, x_ref, ck_ref, cv_ref, st_ref, norm_w_ref, b_gate_ref, qw_ref,
                   kw_ref, conv_w_ref, vnw_ref, mix_ref, spb_ref,
                   w_in_hbm, woa_hbm, wob_hbm, woc_hbm, wo_hbm,
                   y_ref, sk_ref, sv_ref, ci_ref, scv_ref,
                   w_in_out, woa_out, wob_out, woc_out, wo_out,
                   x_scr, bold_scr, bnew_scr, q_scr, en_scr, oo_scr, inv_scr,
                   w_in_ref, woa_ref, wob_ref, woc_ref, wo_ref, stage_in, stage_sq,
                   sem_in, sem_sq, sem_out, st_scr, *, dec_seq):
    layer = pl.program_id(0)
    j = pl.program_id(1)
    rows = SAMPLE_ROWS
    nseq = rows // dec_seq
    stack = N_HEADS * dec_seq
    w = WINDOW

    @pl.when((layer == 0) & (j == 0))
    def _():
        st_scr[...] = jnp.zeros(st_scr.shape, F32)
        r = lax.broadcasted_iota(jnp.int32, (stack, LANES), 0)
        col = lax.broadcasted_iota(jnp.int32, (stack, LANES), 1)
        tok = lax.rem(r, dec_seq)
        head = r // dec_seq
        slope = jnp.zeros((stack, LANES), F32)
        for n in range(N_HEADS):
            slope = jnp.where(head == n, _slope(n), slope)
        dist = tok + w - col
        bold_scr[...] = jnp.where(dist < WINDOW, -(slope * dist.astype(F32)), NEG_INF)
        kseq = col // dec_seq
        dist = tok - lax.rem(col, dec_seq)
        pen = -(slope * dist.astype(F32))
        for b in range(nseq):
            bnew_scr[b] = jnp.where((kseq == b) & (dist >= 0), pen, NEG_INF)

    r0 = pl.multiple_of(j * rows, rows)

    resident = ((w_in_hbm, w_in_ref, w_in_out), (woa_hbm, woa_ref, woa_out), (wob_hbm, wob_ref, wob_out),
                (woc_hbm, woc_ref, woc_out), (wo_hbm, wo_ref, wo_out))

    def publish(n, lyr):
        return pltpu.make_async_copy(resident[n][1], resident[n][2].at[lyr], sem_out.at[n])

    @pl.when(j == 0)
    def _():
        @pl.when(layer > 0)
        def _():
            for n in range(len(resident)):
                publish(n, layer - 1).wait()

        wide = [(w_in_hbm, w_in_ref, r) for r in range(0, w_in_ref.shape[0], stage_in.shape[1])]
        square = [(src, dst, r) for src, dst, _ in resident[1:]
                  for r in range(0, dst.shape[0], stage_sq.shape[1])]
        _stage_bf16([(wide, stage_in, sem_in), (square, stage_sq, sem_sq)], layer)
        for n in range(len(resident)):
            publish(n, layer).start()

    @pl.when(layer == 0)
    def _():
        x_scr[pl.ds(r0, rows), :] = x_ref[...]

    x = x_scr[pl.ds(r0, rows), :]
    xn = _rms(x, norm_w_ref[...]).astype(BF16)
    def proj(c0, c1):
        return _dot(xn, w_in_ref[:, c0:c1])

    def gate(idx):
        g = proj(COL_G + idx * D_MODEL, COL_G + (idx + 1) * D_MODEL)
        return _sigmoid(g + b_gate_ref[:, idx * D_MODEL:(idx + 1) * D_MODEL])

    ha = proj(COL_A, COL_B)
    hb = proj(COL_B, COL_C)
    q = _norm_heads(ha[:, 0:ATTN_WIDTH], qw_ref[...]) * (HEAD_DIM ** -0.5)
    k = _pair_rms(ha[:, ATTN_WIDTH:ATTN_WIDTH + KV_WIDTH], kw_ref[...])
    v = ha[:, ATTN_WIDTH + KV_WIDTH:ATTN_WIDTH + 2 * KV_WIDTH]
    z_a = ha[:, ATTN_WIDTH + 2 * KV_WIDTH:COL_B]

    lo = _lo_lanes((rows, LANES))
    for h in range(N_KV_HEADS):
        keep = lo if h == 0 else jnp.logical_not(lo)
        for g in range(Q_GROUP):
            pair = q[:, (h * 2 + g // 2) * LANES:(h * 2 + g // 2 + 1) * LANES]
            if g % 2 != h:
                pair = pltpu.roll(pair, HEAD_DIM, axis=1)
            piece = jnp.where(keep, pair, 0.0).reshape(nseq, dec_seq, LANES)
            s0 = (h * Q_GROUP + g) * dec_seq
            q_scr[:, s0:s0 + dec_seq, :] = piece

    kt_new = jnp.transpose(k)
    vt_new = jnp.transpose(v)
    qall = q_scr[...].reshape(nseq * stack, LANES).astype(BF16)
    s_new_all = _dot(qall, kt_new.astype(BF16)).reshape(nseq, stack, LANES)

    srow = lax.broadcasted_iota(jnp.int32, (stack, 1), 0) // dec_seq
    sink = jnp.zeros((stack, 1), F32)
    for n in range(N_HEADS):
        sink = jnp.where(srow == n, sinks_ref[layer, n], sink)
    bias_old = bold_scr[...]
    lane = lax.broadcasted_iota(jnp.int32, (KV_WIDTH, w), 1)
    s_olds = [_dot(q_scr[b].astype(BF16), ck_ref[b].reshape(KV_WIDTH, w).astype(BF16)) + bias_old
              for b in range(nseq)]
    hc = proj(COL_C, COL_G)
    gates = []
    for b in range(nseq):
        if b % (nseq // 4) == 0 and 0 < b:
            gates.append(gate(len(gates)))
        kt = ck_ref[b].reshape(KV_WIDTH, w)
        vt = cv_ref[b].reshape(KV_WIDTH, w)
        s_old = s_olds[b]
        s_new = s_new_all[b] + bnew_scr[b]
        mx = jnp.maximum(jnp.maximum(jnp.max(s_old, axis=-1, keepdims=True),
                                     jnp.max(s_new, axis=-1, keepdims=True)), sink)
        e_old = jnp.exp(s_old - mx)
        e_new = jnp.exp(s_new - mx)
        den = (jnp.sum(e_old, axis=-1, keepdims=True) + jnp.sum(e_new, axis=-1, keepdims=True)
               + jnp.exp(sink - mx))
        en_scr[b] = e_new
        oo_scr[b] = _dot_nt(e_old.astype(BF16), vt.astype(BF16))
        inv_scr[b] = jnp.broadcast_to(1.0 / den, (stack, LANES))
        shift = w - dec_seq - b * dec_seq
        newk = pltpu.roll(kt_new, shift, axis=1) if shift else kt_new
        newv = pltpu.roll(vt_new, shift, axis=1) if shift else vt_new
        keep_old = lane < w - dec_seq
        sk_ref[b] = jnp.where(keep_old, pltpu.roll(kt, w - dec_seq, axis=1), newk).reshape(
            N_KV_HEADS, HEAD_DIM, w)
        sv_ref[b] = jnp.where(keep_old, pltpu.roll(vt, w - dec_seq, axis=1), newv).reshape(
            N_KV_HEADS, HEAD_DIM, w)

    o_new = _dot(en_scr[...].reshape(nseq * stack, LANES).astype(BF16), v.astype(BF16))
    o = (oo_scr[...] + o_new.reshape(nseq, stack, LANES)) * inv_scr[...]
    pairs = []
    for h in range(N_KV_HEADS):
        heads = []
        for g in range(Q_GROUP):
            s0 = (h * Q_GROUP + g) * dec_seq
            og = o[:, s0:s0 + dec_seq, :].reshape(rows, LANES)
            if g % 2 != h:
                og = pltpu.roll(og, HEAD_DIM, axis=1)
            heads.append(og)
        pairs.append(jnp.where(lo, heads[0], heads[1]))
        pairs.append(jnp.where(lo, heads[2], heads[3]))
    y_a = jnp.concatenate(pairs, axis=1)

    st_scr[:, 0:CONV_K - 1, :] = st_ref[...]
    st = st_scr[...].reshape(rows, CONV_WIDTH)

    def conv_shift(ci):
        t = lax.rem(lax.broadcasted_iota(jnp.int32, ci.shape, 0), dec_seq)
        xm1 = jnp.where(t == 0, pltpu.roll(st, rows - 1, axis=0), pltpu.roll(ci, 1, axis=0))
        xm2 = jnp.where(t < CONV_K - 1, st, pltpu.roll(ci, 2, axis=0))
        return xm1, xm2

    mr = lax.broadcasted_iota(jnp.int32, (CHUNK, CHUNK), 0)
    mc = lax.broadcasted_iota(jnp.int32, (CHUNK, CHUNK), 1)
    mix_mask = (mr // dec_seq == mc // dec_seq) & (mc <= mr)
    b_out, conv_in = _branch_b(hb, conv_shift, conv_w_ref, wob_ref)
    reps = CHUNK // dec_seq

    def mix_rows(g):
        return jnp.broadcast_to(mix_ref[g][None], (reps, dec_seq, CHUNK)).reshape(CHUNK, CHUNK)

    spb = jnp.broadcast_to(spb_ref[...][None], (reps, dec_seq, MLP_WIDTH)).reshape(CHUNK, MLP_WIDTH)
    c_out, v_c = _branch_c(hc, mix_mask, mix_rows, spb, vnw_ref, woc_ref)
    y = _merge(x, gates[0] * _branch_a(y_a, z_a, woa_ref), gates[1] * b_out, gates[2] * c_out, wo_ref)
    x_scr[pl.ds(r0, rows), :] = y

    @pl.when(layer == pl.num_programs(0) - 1)
    def _():
        y_ref[...] = y
    ci_ref[...] = conv_in.reshape(nseq, dec_seq, CONV_WIDTH)[:, dec_seq - (CONV_K - 1):, :]
    scv_ref[...] = v_c

    @pl.when((layer == pl.num_programs(0) - 1) & (j == pl.num_programs(1) - 1))
    def _():
        for n in range(len(resident)):
            publish(n, layer).wait()


SMALL_WEIGHT_NAMES = ("norm_w", "b_gate", "qw", "kw", "conv_w", "vnw", "mix", "spb")
BIG_WEIGHT_NAMES = ("w_in", "woa", "wob", "woc", "wo")
STAGE_SLOTS = 8
STAGE_ROWS_W_IN = 32
STAGE_ROWS_SQUARE = 128
WEIGHT_NAMES = ("norm_w", "w_in", "b_gate", "qw", "kw", "conv_w", "vnw", "mix", "spb",
                "woa", "wob", "woc", "wo")


def _layer_spec(arr, layer_of):
    nd = arr.ndim - 1
    return pl.BlockSpec((None,) + arr.shape[1:], lambda *g, _nd=nd: (layer_of(*g),) + (0,) * _nd,
                        pipeline_mode=pl.Buffered(1))


def _prompt_layer(x, wts, layer, batch):
    n, d = x.shape
    rows = PROMPT_ROWS
    nblk = n // batch // rows
    smem = pl.BlockSpec(memory_space=pltpu.SMEM)
    in_specs = ([smem, pl.BlockSpec((rows, d), lambda i: (i, 0))]
                + [_layer_spec(wts[name], lambda i: layer) for name in WEIGHT_NAMES])
    out_shape = (jax.ShapeDtypeStruct((n, d), F32),
                 jax.ShapeDtypeStruct((batch, WINDOW, KV_WIDTH), F32),
                 jax.ShapeDtypeStruct((batch, WINDOW, KV_WIDTH), F32),
                 jax.ShapeDtypeStruct((batch, SUBLANES, CONV_WIDTH), F32))
    out_specs = (pl.BlockSpec((rows, d), lambda i: (i, 0)),
                 pl.BlockSpec((1, WINDOW, KV_WIDTH), lambda i: (i // nblk, 0, 0)),
                 pl.BlockSpec((1, WINDOW, KV_WIDTH), lambda i: (i // nblk, 0, 0)),
                 pl.BlockSpec((1, SUBLANES, CONV_WIDTH), lambda i: (i // nblk, 0, 0)))
    n_tables = 2 * N_KV_HEADS * 2
    scratch = [pltpu.VMEM((n_tables, 2 * WINDOW, 2 * WINDOW), F32)]
    scratch += [pltpu.VMEM((WINDOW + rows, KV_WIDTH), BF16) for _ in range(2)]
    scratch += [pltpu.VMEM((KV_WIDTH, WINDOW + rows), BF16)]
    scratch += [pltpu.VMEM((SUBLANES, CONV_WIDTH), F32)]
    return pl.pallas_call(
        functools.partial(_prompt_kernel, rows=rows, nblk=nblk, layer=layer),
        grid=(n // rows,),
        in_specs=in_specs, out_specs=out_specs, out_shape=out_shape, scratch_shapes=scratch,
        compiler_params=pltpu.CompilerParams(dimension_semantics=("arbitrary",),
                                             vmem_limit_bytes=VMEM_LIMIT_BYTES),
        name="prompt_layer",
    )(wts["sinks"], x, *[wts[name] for name in WEIGHT_NAMES])


def _sample_layers(x, ck, cv, st, wts, dec_seq):
    n, d = x.shape
    depth = ck.shape[0]
    rows = SAMPLE_ROWS
    nseq = rows // dec_seq
    stack = N_HEADS * dec_seq
    w = ck.shape[-1]
    smem = pl.BlockSpec(memory_space=pltpu.SMEM)
    cache_blk = pl.BlockSpec((None, nseq, N_KV_HEADS, HEAD_DIM, w), lambda l, j: (l, j, 0, 0, 0))
    rows_blk = lambda width: pl.BlockSpec((None, rows, width), lambda l, j: (l, j, 0))
    nblk = n // rows
    x_blk = pl.BlockSpec((rows, d), lambda l, j: (jnp.where(l == 0, j, nblk - 1), 0))
    y_blk = pl.BlockSpec((rows, d), lambda l, j: (jnp.where(l == depth - 1, j, 0), 0))
    hbm = pl.BlockSpec(memory_space=pl.ANY)
    state_blk = pl.BlockSpec((None, nseq, CONV_K - 1, CONV_WIDTH), lambda l, j: (l, j, 0, 0))
    in_specs = ([smem, x_blk, cache_blk, cache_blk, state_blk]
                + [_layer_spec(wts[name], lambda l, j: l) for name in SMALL_WEIGHT_NAMES]
                + [hbm] * len(BIG_WEIGHT_NAMES))
    out_shape = (jax.ShapeDtypeStruct((n, d), F32),
                 jax.ShapeDtypeStruct(ck.shape, F32),
                 jax.ShapeDtypeStruct(cv.shape, F32),
                 jax.ShapeDtypeStruct(st.shape, F32),
                 jax.ShapeDtypeStruct((depth, n, MLP_WIDTH), F32)
                 ) + tuple(jax.ShapeDtypeStruct(wts[name].shape, BF16) for name in BIG_WEIGHT_NAMES)
    out_specs = ((y_blk, cache_blk, cache_blk, state_blk, rows_blk(MLP_WIDTH))
                 + (hbm,) * len(BIG_WEIGHT_NAMES))
    scratch = [pltpu.VMEM((n, d), F32),
               pltpu.VMEM((stack, LANES), F32),
               pltpu.VMEM((nseq, stack, LANES), F32),
               pltpu.VMEM((nseq, stack, LANES), F32),
               pltpu.VMEM((nseq, stack, LANES), F32),
               pltpu.VMEM((nseq, stack, LANES), F32),
               pltpu.VMEM((nseq, stack, LANES), F32)]
    scratch += [pltpu.VMEM(wts[name].shape[1:], BF16) for name in BIG_WEIGHT_NAMES]
    scratch += [pltpu.VMEM((STAGE_SLOTS, STAGE_ROWS_W_IN, IN_COLS), F32),
                pltpu.VMEM((STAGE_SLOTS, STAGE_ROWS_SQUARE, d), F32),
                pltpu.SemaphoreType.DMA((STAGE_SLOTS,)), pltpu.SemaphoreType.DMA((STAGE_SLOTS,)),
                pltpu.SemaphoreType.DMA((len(BIG_WEIGHT_NAMES),)),
                pltpu.VMEM((nseq, dec_seq, CONV_WIDTH), F32)]
    return pl.pallas_call(
        functools.partial(_sample_kernel, dec_seq=dec_seq),
        grid=(depth, n // rows),
        in_specs=in_specs, out_specs=out_specs, out_shape=out_shape, scratch_shapes=scratch,
        compiler_params=pltpu.CompilerParams(dimension_semantics=("arbitrary", "arbitrary"),
                                             vmem_limit_bytes=VMEM_LIMIT_BYTES),
        name="sample_layers",
    )(wts["sinks"], x, ck, cv, st, *[wts[name] for name in SMALL_WEIGHT_NAMES + BIG_WEIGHT_NAMES])


def kernel(x_prompt, x_sample, cache_k, cache_v, state_conv, norm_w, w_in, b_gate, q_norm_w, k_norm_w,
           sinks, conv_w, v_norm_w, w_spatial, b_spatial, w_out_a, w_out_b, w_out_c, w_o):
    batch, seq, d = x_prompt.shape
    dec_batch, dec_seq, _ = x_sample.shape
    depth = w_in.shape[0]
    w_buf = cache_k.shape[2]
    assert d == D_MODEL and seq % PROMPT_ROWS == 0 and (dec_batch * dec_seq) % SAMPLE_ROWS == 0
    assert w_buf == WINDOW and dec_seq == SUBLANES and w_in.shape[2] == IN_COLS

    gw = MLP_WIDTH // N_SPATIAL_GROUPS
    reps = CHUNK // dec_seq
    spb_p = jnp.repeat(jnp.swapaxes(b_spatial, 1, 2), gw, axis=2)
    common = {
        "sinks": sinks,
        "norm_w": norm_w[:, None, :], "w_in": w_in, "b_gate": b_gate[:, None, :],
        "qw": jnp.tile(q_norm_w, (1, LANES // HEAD_DIM))[:, None, :],
        "kw": jnp.tile(k_norm_w, (1, LANES // HEAD_DIM))[:, None, :],
        "conv_w": conv_w, "vnw": v_norm_w[:, None, :],
        "woa": w_out_a, "wob": w_out_b, "woc": w_out_c, "wo": w_o,
    }
    wts_s = dict(common,
                 mix=jnp.tile(w_spatial[:, :, :dec_seq, :dec_seq], (1, 1, 1, reps)),
                 spb=spb_p[:, :dec_seq, :])

    ck = jnp.transpose(cache_k, (0, 1, 3, 4, 2))
    cv = jnp.transpose(cache_v, (0, 1, 3, 4, 2))

    ys, sk, sv, sc, scv, *big_bf16 = _sample_layers(x_sample.reshape(dec_batch * dec_seq, d), ck, cv,
                                                    state_conv, wts_s, dec_seq)
    wts_p = dict(common, mix=w_spatial, spb=spb_p, **dict(zip(BIG_WEIGHT_NAMES, big_bf16)))
    sk = jnp.transpose(sk, (0, 1, 4, 2, 3))
    sv = jnp.transpose(sv, (0, 1, 4, 2, 3))
    scv = scv.reshape(depth, dec_batch, dec_seq, MLP_WIDTH)

    yp = x_prompt.reshape(batch * seq, d)
    pk, pv, pc = [], [], []
    for l in range(depth):
        yp, k_l, v_l, c_l = _prompt_layer(yp, wts_p, l, batch)
        pk.append(k_l.reshape(batch, WINDOW, N_KV_HEADS, HEAD_DIM))
        pv.append(v_l.reshape(batch, WINDOW, N_KV_HEADS, HEAD_DIM))
        pc.append(c_l[:, SUBLANES - (CONV_K - 1):, :])

    return (yp.reshape(batch, seq, d), ys.reshape(dec_batch, dec_seq, d), jnp.stack(pk), jnp.stack(pv),
            jnp.stack(pc), sk, sv, sc, scv)
```

```python
import functools

import jax
import jax.numpy as jnp
from jax import lax
from jax.experimental import pallas as pl
from jax.experimental.pallas import tpu as pltpu

F32 = jnp.float32
BF16 = jnp.bfloat16

D_MODEL = 1024
N_HEADS = 8
N_KV_HEADS = 2
HEAD_DIM = 64
Q_GROUP = N_HEADS // N_KV_HEADS
ATTN_WIDTH = N_HEADS * HEAD_DIM
KV_WIDTH = N_KV_HEADS * HEAD_DIM
WINDOW = 128
CONV_WIDTH = 512
CONV_K = 3
CHUNK = 128
MLP_WIDTH = 512
N_SPATIAL_GROUPS = 4
EPS = 1e-6
NEG_INF = -1e30
LOG2E = 1.4426950408889634

COL_A = 0
COL_B = COL_A + 2 * ATTN_WIDTH + 2 * KV_WIDTH
COL_C = COL_B + 4 * CONV_WIDTH
COL_G = COL_C + 3 * MLP_WIDTH
IN_COLS = COL_G + 3 * D_MODEL

LANES = 128
SUBLANES = 8
PROMPT_ROWS = 512
SAMPLE_ROWS = 128
VMEM_LIMIT_BYTES = 56 * 1024 * 1024


def _dot(a, b):
    return jnp.dot(a, b, preferred_element_type=F32)


def _dot_nt(a, b):
    return lax.dot_general(a, b, (((1,), (1,)), ((), ())), preferred_element_type=F32)


def _rms(x, w):
    ms = jnp.mean(x * x, axis=-1, keepdims=True)
    return (x * lax.rsqrt(ms + EPS)) * w


def _sigmoid(x):
    return 0.5 * jnp.tanh(0.5 * x) + 0.5


def _silu(x):
    return x * _sigmoid(x)


def _lo_lanes(shape):
    return lax.broadcasted_iota(jnp.int32, shape, len(shape) - 1) < HEAD_DIM


def _pair_rms(x, w):
    lo = _lo_lanes(x.shape)
    sq = x * x
    s_lo = jnp.sum(jnp.where(lo, sq, 0.0), axis=-1, keepdims=True)
    s_hi = jnp.sum(jnp.where(lo, 0.0, sq), axis=-1, keepdims=True)
    ms = jnp.where(lo, s_lo, s_hi) * (1.0 / HEAD_DIM)
    return (x * lax.rsqrt(ms + EPS)) * w


def _norm_heads(x, w):
    groups = [_pair_rms(x[:, g * LANES:(g + 1) * LANES], w) for g in range(x.shape[1] // LANES)]
    return groups[0] if len(groups) == 1 else jnp.concatenate(groups, axis=1)


def _slope(head):
    return 2.0 ** (-(head + 1))


def _branch_b(hb, conv_shift, conv_w_ref, wob_ref):
    gate_b = hb[:, 0:CONV_WIDTH]
    gate_c = hb[:, CONV_WIDTH:2 * CONV_WIDTH]
    h_b = hb[:, 2 * CONV_WIDTH:3 * CONV_WIDTH]
    z_b = hb[:, 3 * CONV_WIDTH:4 * CONV_WIDTH]
    conv_in = gate_c * h_b
    xm1, xm2 = conv_shift(conv_in)
    cw = conv_w_ref[...]
    conv_out = cw[0:1] * xm2 + cw[1:2] * xm1 + cw[2:3] * conv_in
    y_b = gate_b * conv_out
    return _dot((_silu(z_b) * y_b).astype(BF16), wob_ref[...]), conv_in


def _branch_c(hc, mix_mask, mix_rows, spb, vnw_ref, woc_ref):
    rows = hc.shape[0]
    u = hc[:, 0:MLP_WIDTH]
    v_c = _rms(hc[:, MLP_WIDTH:2 * MLP_WIDTH], vnw_ref[...])
    z_c = hc[:, 2 * MLP_WIDTH:3 * MLP_WIDTH]
    vb = v_c.astype(BF16)
    gw = MLP_WIDTH // N_SPATIAL_GROUPS
    mixes = [jnp.where(mix_mask, mix_rows(g), 0.0).astype(BF16) for g in range(N_SPATIAL_GROUPS)]
    sp_chunks = []
    for c in range(rows // CHUNK):
        r0 = c * CHUNK
        parts = [_dot(mixes[g], vb[r0:r0 + CHUNK, g * gw:(g + 1) * gw])
                 for g in range(N_SPATIAL_GROUPS)]
        sp_chunks.append(jnp.concatenate(parts, axis=1) + spb)
    sp = sp_chunks[0] if len(sp_chunks) == 1 else jnp.concatenate(sp_chunks, axis=0)
    y_c = u * sp
    return _dot((_silu(z_c) * y_c).astype(BF16), woc_ref[...]), v_c


def _branch_a(y_a, z_a, woa_ref):
    return _dot((_silu(z_a) * y_a).astype(BF16), woa_ref[...])


def _merge(x, gated_a, gated_b, gated_c, wo_ref):
    m = gated_a + gated_b + gated_c
    return x + _dot(m.astype(BF16), wo_ref[...])


def _prompt_kernel(sinks_ref, x_ref, norm_w_ref, w_in_ref, b_gate_ref, qw_ref, kw_ref, conv_w_ref,
                   vnw_ref, mix_ref, spb_ref, woa_ref, wob_ref, woc_ref, wo_ref,
                   y_ref, pk_ref, pv_ref, pc_ref,
                   bias_scr, ka_scr, kb_scr, vt_scr, carry_scr, *, rows, nblk, layer):
    i = pl.program_id(0)
    first = lax.rem(i, nblk) == 0
    nsub = rows // WINDOW
    stack = 2 * WINDOW

    @pl.when(i == 0)
    def _():
        key = lax.broadcasted_iota(jnp.int32, (2 * WINDOW, stack), 0)
        c = lax.broadcasted_iota(jnp.int32, (2 * WINDOW, stack), 1)
        left = c < WINDOW
        dist = jnp.where(left, c, c - WINDOW) + WINDOW - key
        band = (dist >= 0) & (dist < WINDOW)
        distf = dist.astype(F32)
        for flag in range(2):
            valid = band & (key >= WINDOW) if flag else band
            for h in range(N_KV_HEADS):
                for half in range(2):
                    slope = jnp.where(left, _slope(h * Q_GROUP + half), _slope(h * Q_GROUP + 2 + half))
                    bias_scr[flag * 4 + h * 2 + half] = jnp.where(valid, -(slope * distf) * LOG2E, NEG_INF)

    @pl.when(first)
    def _():
        zeros = jnp.zeros((WINDOW, KV_WIDTH), BF16)
        ka_scr[0:WINDOW, :] = zeros
        kb_scr[0:WINDOW, :] = zeros
        vt_scr[:, 0:WINDOW] = zeros
        carry_scr[...] = jnp.zeros(carry_scr.shape, F32)

    x = x_ref[...]
    xn = _rms(x, norm_w_ref[...]).astype(BF16)
    def proj(c0, c1):
        return _dot(xn, w_in_ref[:, c0:c1])

    def gate(idx):
        g = proj(COL_G + idx * D_MODEL, COL_G + (idx + 1) * D_MODEL)
        return _sigmoid(g + b_gate_ref[:, idx * D_MODEL:(idx + 1) * D_MODEL])

    ha = proj(COL_A, COL_B)
    hb = proj(COL_B, COL_C)
    q = _norm_heads(ha[:, 0:ATTN_WIDTH], qw_ref[...]) * (HEAD_DIM ** -0.5 * LOG2E)
    k = _pair_rms(ha[:, ATTN_WIDTH:ATTN_WIDTH + KV_WIDTH], kw_ref[...])
    v = ha[:, ATTN_WIDTH + KV_WIDTH:ATTN_WIDTH + 2 * KV_WIDTH]
    z_a = ha[:, ATTN_WIDTH + 2 * KV_WIDTH:COL_B]

    pk_ref[0] = k[rows - WINDOW:rows, :]
    pv_ref[0] = v[rows - WINDOW:rows, :]
    ka_scr[WINDOW:WINDOW + rows, :] = k.astype(BF16)
    kb_scr[WINDOW:WINDOW + rows, :] = pltpu.roll(k, HEAD_DIM, axis=1).astype(BF16)
    for c in range(nsub):
        vt_scr[:, (c + 1) * WINDOW:(c + 2) * WINDOW] = jnp.transpose(
            v[c * WINDOW:(c + 1) * WINDOW, :]).astype(BF16)

    lo = _lo_lanes((WINDOW, LANES))
    scol = lax.broadcasted_iota(jnp.int32, (1, stack), 1) < WINDOW
    flag = first.astype(jnp.int32)
    carry = carry_scr[...]
    prev1 = carry[SUBLANES - 1:SUBLANES, :]
    prev2 = carry[SUBLANES - 2:SUBLANES - 1, :]

    def conv_shift(ci):
        rid = lax.broadcasted_iota(jnp.int32, ci.shape, 0)
        xm1 = jnp.where(rid == 0, prev1, pltpu.roll(ci, 1, axis=0))
        xm2 = jnp.where(rid == 0, prev2, jnp.where(rid == 1, prev1, pltpu.roll(ci, 2, axis=0)))
        return xm1, xm2

    mr = lax.broadcasted_iota(jnp.int32, (CHUNK, CHUNK), 0)
    mc = lax.broadcasted_iota(jnp.int32, (CHUNK, CHUNK), 1)
    done = {}
    stages = [
        lambda: done.update(hc=proj(COL_C, COL_G)),
        lambda: done.update(b=_branch_b(hb, conv_shift, conv_w_ref, wob_ref)),
        lambda: done.update(gated_b=gate(1) * done["b"][0]),
        lambda: done.update(c=_branch_c(done["hc"], mc <= mr, lambda g: mix_ref[g], spb_ref[...],
                                        vnw_ref, woc_ref)),
        lambda: done.update(gated_c=gate(2) * done["c"][0]),
        lambda: done.update(g0=gate(0)),
    ]
    slots = 2 * nsub
    order = list(range(0, slots, 2)) + list(range(1, slots, 2))
    per_slot = [0] * slots
    for n in range(len(stages)):
        per_slot[order[n % slots]] += 1

    def run_stages(slot):
        for _ in range(per_slot[slot]):
            stages.pop(0)()

    y_rows = []
    for c in range(nsub):
        r0 = c * WINDOW
        keys = slice(r0, r0 + 2 * WINDOW)
        k_nat, k_swp = ka_scr[keys, :], kb_scr[keys, :]
        vt = vt_scr[:, keys]
        scores = []
        for h in range(N_KV_HEADS):
            p0 = q[r0:r0 + WINDOW, (2 * h) * LANES:(2 * h + 1) * LANES]
            p1 = q[r0:r0 + WINDOW, (2 * h + 1) * LANES:(2 * h + 2) * LANES]
            for half in range(2):
                keep = lo if half == 0 else jnp.logical_not(lo)
                qs = jnp.concatenate([jnp.where(keep, p0, 0.0), jnp.where(keep, p1, 0.0)],
                                     axis=0).astype(BF16)
                kk = k_nat if h == half else k_swp
                tbl = h * 2 + half
                bias = bias_scr[flag * 4 + tbl] if c == 0 else bias_scr[tbl]
                scores.append(_dot_nt(kk, qs) + bias)
        run_stages(2 * c)
        pairs = []
        for h in range(N_KV_HEADS):
            outs = []
            for half in range(2):
                s = scores[h * 2 + half]
                sink = jnp.where(scol, sinks_ref[layer, h * Q_GROUP + half],
                                 sinks_ref[layer, h * Q_GROUP + 2 + half]) * LOG2E
                mx = jnp.maximum(jnp.max(s, axis=0, keepdims=True), sink)
                e = jnp.exp2(s - mx)
                den = jnp.sum(e, axis=0, keepdims=True) + jnp.exp2(sink - mx)
                outs.append(_dot(vt[h * HEAD_DIM:(h + 1) * HEAD_DIM, :], e.astype(BF16)) * (1.0 / den))
            for p in range(2):
                cols = slice(p * WINDOW, (p + 1) * WINDOW)
                pairs.append(jnp.transpose(jnp.concatenate([outs[0][:, cols], outs[1][:, cols]], axis=0)))
        y_rows.append(jnp.concatenate(pairs, axis=1))
        run_stages(2 * c + 1)
    y_a = y_rows[0] if nsub == 1 else jnp.concatenate(y_rows, axis=0)

    ka_scr[0:WINDOW, :] = ka_scr[rows:rows + WINDOW, :]
    kb_scr[0:WINDOW, :] = kb_scr[rows:rows + WINDOW, :]
    vt_scr[:, 0:WINDOW] = vt_scr[:, rows:rows + WINDOW]

    gated_a = done["g0"] * _branch_a(y_a, z_a, woa_ref)
    y_ref[...] = _merge(x, gated_a, done["gated_b"], done["gated_c"], wo_ref)
    conv_in = done["b"][1]
    last = conv_in[rows - SUBLANES:rows, :]
    carry_scr[...] = last
    pc_ref[0] = last


def _stage_bf16(streams, layer):
    def copy(s, k):
        jobs, stage, sem = streams[s]
        src, _, r0 = jobs[k]
        slot = k % stage.shape[0]
        return pltpu.make_async_copy(src.at[layer, pl.ds(r0, stage.shape[1]), :], stage.at[slot], sem.at[slot])

    for s, (jobs, stage, _) in enumerate(streams):
        for k in range(min(stage.shape[0] - 1, len(jobs))):
            copy(s, k).start()
    for k in range(max(len(jobs) for jobs, _, _ in streams)):
        for s, (jobs, stage, _) in enumerate(streams):
            if k >= len(jobs):
                continue
            nslots, chunk_rows = stage.shape[0], stage.shape[1]
            if k + nslots - 1 < len(jobs):
                copy(s, k + nslots - 1).start()
            copy(s, k).wait()
            _, dst, r0 = jobs[k]
            dst[r0:r0 + chunk_rows, :] = stage[k % nslots].astype(BF16)


def _sample_kernel(sinks_ref, x_ref, ck_ref, cv_ref, st_ref, norm_w_ref, b_gate_ref, qw_ref,
                   kw_ref, conv_w_ref, vnw_ref, mix_ref, spb_ref,
                   w_in_hbm, woa_hbm, wob_hbm, woc_hbm, wo_hbm,
                   y_ref, sk_ref, sv_ref, ci_ref, scv_ref,
                   w_in_out, woa_out, wob_out, woc_out, wo_out,
                   x_scr, bold_scr, bnew_scr, q_scr, en_scr, oo_scr, inv_scr,
                   w_in_ref, woa_ref, wob_ref, woc_ref, wo_ref, stage_in, stage_sq,
                   sem_in, sem_sq, sem_out, st_scr, *, dec_seq):
    layer = pl.program_id(0)
    j = pl.program_id(1)
    rows = SAMPLE_ROWS
    nseq = rows // dec_seq
    stack = N_HEADS * dec_seq
    w = WINDOW

    @pl.when((layer == 0) & (j == 0))
    def _():
        st_scr[...] = jnp.zeros(st_scr.shape, F32)
        r = lax.broadcasted_iota(jnp.int32, (stack, LANES), 0)
        col = lax.broadcasted_iota(jnp.int32, (stack, LANES), 1)
        tok = lax.rem(r, dec_seq)
        head = r // dec_seq
        slope = jnp.zeros((stack, LANES), F32)
        for n in range(N_HEADS):
            slope = jnp.where(head == n, _slope(n), slope)
        dist = tok + w - col
        bold_scr[...] = jnp.where(dist < WINDOW, -(slope * dist.astype(F32)), NEG_INF)
        kseq = col // dec_seq
        dist = tok - lax.rem(col, dec_seq)
        pen = -(slope * dist.astype(F32))
        for b in range(nseq):
            bnew_scr[b] = jnp.where((kseq == b) & (dist >= 0), pen, NEG_INF)

    r0 = pl.multiple_of(j * rows, rows)

    resident = ((w_in_hbm, w_in_ref, w_in_out), (woa_hbm, woa_ref, woa_out), (wob_hbm, wob_ref, wob_out),
                (woc_hbm, woc_ref, woc_out), (wo_hbm, wo_ref, wo_out))

    def publish(n, lyr):
        return pltpu.make_async_copy(resident[n][1], resident[n][2].at[lyr], sem_out.at[n])

    @pl.when(j == 0)
    def _():
        @pl.when(layer > 0)
        def _():
            for n in range(len(resident)):
                publish(n, layer - 1).wait()

        wide = [(w_in_hbm, w_in_ref, r) for r in range(0, w_in_ref.shape[0], stage_in.shape[1])]
        square = [(src, dst, r) for src, dst, _ in resident[1:]
                  for r in range(0, dst.shape[0], stage_sq.shape[1])]
        _stage_bf16([(wide, stage_in, sem_in), (square, stage_sq, sem_sq)], layer)
        for n in range(len(resident)):
            publish(n, layer).start()

    @pl.when(layer == 0)
    def _():
        x_scr[pl.ds(r0, rows), :] = x_ref[...]

    x = x_scr[pl.ds(r0, rows), :]
    xn = _rms(x, norm_w_ref[...]).astype(BF16)
    def proj(c0, c1):
        return _dot(xn, w_in_ref[:, c0:c1])

    def gate(idx):
        g = proj(COL_G + idx * D_MODEL, COL_G + (idx + 1) * D_MODEL)
        return _sigmoid(g + b_gate_ref[:, idx * D_MODEL:(idx + 1) * D_MODEL])

    ha = proj(COL_A, COL_B)
    hb = proj(COL_B, COL_C)
    q = _norm_heads(ha[:, 0:ATTN_WIDTH], qw_ref[...]) * (HEAD_DIM ** -0.5)
    k = _pair_rms(ha[:, ATTN_WIDTH:ATTN_WIDTH + KV_WIDTH], kw_ref[...])
    v = ha[:, ATTN_WIDTH + KV_WIDTH:ATTN_WIDTH + 2 * KV_WIDTH]
    z_a = ha[:, ATTN_WIDTH + 2 * KV_WIDTH:COL_B]

    lo = _lo_lanes((rows, LANES))
    for h in range(N_KV_HEADS):
        keep = lo if h == 0 else jnp.logical_not(lo)
        for g in range(Q_GROUP):
            pair = q[:, (h * 2 + g // 2) * LANES:(h * 2 + g // 2 + 1) * LANES]
            if g % 2 != h:
                pair = pltpu.roll(pair, HEAD_DIM, axis=1)
            piece = jnp.where(keep, pair, 0.0).reshape(nseq, dec_seq, LANES)
            s0 = (h * Q_GROUP + g) * dec_seq
            q_scr[:, s0:s0 + dec_seq, :] = piece

    kt_new = jnp.transpose(k)
    vt_new = jnp.transpose(v)
    qall = q_scr[...].reshape(nseq * stack, LANES).astype(BF16)
    s_new_all = _dot(qall, kt_new.astype(BF16)).reshape(nseq, stack, LANES)

    srow = lax.broadcasted_iota(jnp.int32, (stack, 1), 0) // dec_seq
    sink = jnp.zeros((stack, 1), F32)
    for n in range(N_HEADS):
        sink = jnp.where(srow == n, sinks_ref[layer, n], sink)
    bias_old = bold_scr[...]
    lane = lax.broadcasted_iota(jnp.int32, (KV_WIDTH, w), 1)
    s_olds = [_dot(q_scr[b].astype(BF16), ck_ref[b].reshape(KV_WIDTH, w).astype(BF16)) + bias_old
              for b in range(nseq)]
    hc = proj(COL_C, COL_G)
    gates = []
    for b in range(nseq):
        if b % (nseq // 4) == 0 and 0 < b:
            gates.append(gate(len(gates)))
        kt = ck_ref[b].reshape(KV_WIDTH, w)
        vt = cv_ref[b].reshape(KV_WIDTH, w)
        s_old = s_olds[b]
        s_new = s_new_all[b] + bnew_scr[b]
        mx = jnp.maximum(jnp.maximum(jnp.max(s_old, axis=-1, keepdims=True),
                                     jnp.max(s_new, axis=-1, keepdims=True)), sink)
        e_old = jnp.exp(s_old - mx)
        e_new = jnp.exp(s_new - mx)
        den = (jnp.sum(e_old, axis=-1, keepdims=True) + jnp.sum(e_new, axis=-1, keepdims=True)
               + jnp.exp(sink - mx))
        en_scr[b] = e_new
        oo_scr[b] = _dot_nt(e_old.astype(BF16), vt.astype(BF16))
        inv_scr[b] = jnp.broadcast_to(1.0 / den, (stack, LANES))
        shift = w - dec_seq - b * dec_seq
        newk = pltpu.roll(kt_new, shift, axis=1) if shift else kt_new
        newv = pltpu.roll(vt_new, shift, axis=1) if shift else vt_new
        keep_old = lane < w - dec_seq
        sk_ref[b] = jnp.where(keep_old, pltpu.roll(kt, w - dec_seq, axis=1), newk).reshape(
            N_KV_HEADS, HEAD_DIM, w)
        sv_ref[b] = jnp.where(keep_old, pltpu.roll(vt, w - dec_seq, axis=1), newv).reshape(
            N_KV_HEADS, HEAD_DIM, w)

    o_new = _dot(en_scr[...].reshape(nseq * stack, LANES).astype(BF16), v.astype(BF16))
    o = (oo_scr[...] + o_new.reshape(nseq, stack, LANES)) * inv_scr[...]
    pairs = []
    for h in range(N_KV_HEADS):
        heads = []
        for g in range(Q_GROUP):
            s0 = (h * Q_GROUP + g) * dec_seq
            og = o[:, s0:s0 + dec_seq, :].reshape(rows, LANES)
            if g % 2 != h:
                og = pltpu.roll(og, HEAD_DIM, axis=1)
            heads.append(og)
        pairs.append(jnp.where(lo, heads[0], heads[1]))
        pairs.append(jnp.where(lo, heads[2], heads[3]))
    y_a = jnp.concatenate(pairs, axis=1)

    st_scr[:, 0:CONV_K - 1, :] = st_ref[...]
    st = st_scr[...].reshape(rows, CONV_WIDTH)

    def conv_shift(ci):
        t = lax.rem(lax.broadcasted_iota(jnp.int32, ci.shape, 0), dec_seq)
        xm1 = jnp.where(t == 0, pltpu.roll(st, rows - 1, axis=0), pltpu.roll(ci, 1, axis=0))
        xm2 = jnp.where(t < CONV_K - 1, st, pltpu.roll(ci, 2, axis=0))
        return xm1, xm2

    mr = lax.broadcasted_iota(jnp.int32, (CHUNK, CHUNK), 0)
    mc = lax.broadcasted_iota(jnp.int32, (CHUNK, CHUNK), 1)
    mix_mask = (mr // dec_seq == mc // dec_seq) & (mc <= mr)
    b_out, conv_in = _branch_b(hb, conv_shift, conv_w_ref, wob_ref)
    reps = CHUNK // dec_seq

    def mix_rows(g):
        return jnp.broadcast_to(mix_ref[g][None], (reps, dec_seq, CHUNK)).reshape(CHUNK, CHUNK)

    spb = jnp.broadcast_to(spb_ref[...][None], (reps, dec_seq, MLP_WIDTH)).reshape(CHUNK, MLP_WIDTH)
    c_out, v_c = _branch_c(hc, mix_mask, mix_rows, spb, vnw_ref, woc_ref)
    y = _merge(x, gates[0] * _branch_a(y_a, z_a, woa_ref), gates[1] * b_out, gates[2] * c_out, wo_ref)
    x_scr[pl.ds(r0, rows), :] = y

    @pl.when(layer == pl.num_programs(0) - 1)
    def _():
        y_ref[...] = y
    ci_ref[...] = conv_in.reshape(nseq, dec_seq, CONV_WIDTH)[:, dec_seq - (CONV_K - 1):, :]
    scv_ref[...] = v_c

    @pl.when((layer == pl.num_programs(0) - 1) & (j == pl.num_programs(1) - 1))
    def _():
        for n in range(len(resident)):
            publish(n, layer).wait()


SMALL_WEIGHT_NAMES = ("norm_w", "b_gate", "qw", "kw", "conv_w", "vnw", "mix", "spb")
BIG_WEIGHT_NAMES = ("w_in", "woa", "wob", "woc", "wo")
STAGE_SLOTS = 8
STAGE_ROWS_W_IN = 32
STAGE_ROWS_SQUARE = 128
WEIGHT_NAMES = ("norm_w", "w_in", "b_gate", "qw", "kw", "conv_w", "vnw", "mix", "spb",
                "woa", "wob", "woc", "wo")


def _layer_spec(arr, layer_of):
    nd = arr.ndim - 1
    return pl.BlockSpec((None,) + arr.shape[1:], lambda *g, _nd=nd: (layer_of(*g),) + (0,) * _nd,
                        pipeline_mode=pl.Buffered(1))


def _prompt_layer(x, wts, layer, batch):
    n, d = x.shape
    rows = PROMPT_ROWS
    nblk = n // batch // rows
    smem = pl.BlockSpec(memory_space=pltpu.SMEM)
    in_specs = ([smem, pl.BlockSpec((rows, d), lambda i: (i, 0))]
                + [_layer_spec(wts[name], lambda i: layer) for name in WEIGHT_NAMES])
    out_shape = (jax.ShapeDtypeStruct((n, d), F32),
                 jax.ShapeDtypeStruct((batch, WINDOW, KV_WIDTH), F32),
                 jax.ShapeDtypeStruct((batch, WINDOW, KV_WIDTH), F32),
                 jax.ShapeDtypeStruct((batch, SUBLANES, CONV_WIDTH), F32))
    out_specs = (pl.BlockSpec((rows, d), lambda i: (i, 0)),
                 pl.BlockSpec((1, WINDOW, KV_WIDTH), lambda i: (i // nblk, 0, 0)),
                 pl.BlockSpec((1, WINDOW, KV_WIDTH), lambda i: (i // nblk, 0, 0)),
                 pl.BlockSpec((1, SUBLANES, CONV_WIDTH), lambda i: (i // nblk, 0, 0)))
    n_tables = 2 * N_KV_HEADS * 2
    scratch = [pltpu.VMEM((n_tables, 2 * WINDOW, 2 * WINDOW), F32)]
    scratch += [pltpu.VMEM((WINDOW + rows, KV_WIDTH), BF16) for _ in range(2)]
    scratch += [pltpu.VMEM((KV_WIDTH, WINDOW + rows), BF16)]
    scratch += [pltpu.VMEM((SUBLANES, CONV_WIDTH), F32)]
    return pl.pallas_call(
        functools.partial(_prompt_kernel, rows=rows, nblk=nblk, layer=layer),
        grid=(n // rows,),
        in_specs=in_specs, out_specs=out_specs, out_shape=out_shape, scratch_shapes=scratch,
        compiler_params=pltpu.CompilerParams(dimension_semantics=("arbitrary",),
                                             vmem_limit_bytes=VMEM_LIMIT_BYTES),
        name="prompt_layer",
    )(wts["sinks"], x, *[wts[name] for name in WEIGHT_NAMES])


def _sample_layers(x, ck, cv, st, wts, dec_seq):
    n, d = x.shape
    depth = ck.shape[0]
    rows = SAMPLE_ROWS
    nseq = rows // dec_seq
    stack = N_HEADS * dec_seq
    w = ck.shape[-1]
    smem = pl.BlockSpec(memory_space=pltpu.SMEM)
    cache_blk = pl.BlockSpec((None, nseq, N_KV_HEADS, HEAD_DIM, w), lambda l, j: (l, j, 0, 0, 0))
    rows_blk = lambda width: pl.BlockSpec((None, rows, width), lambda l, j: (l, j, 0))
    nblk = n // rows
    x_blk = pl.BlockSpec((rows, d), lambda l, j: (jnp.where(l == 0, j, nblk - 1), 0))
    y_blk = pl.BlockSpec((rows, d), lambda l, j: (jnp.where(l == depth - 1, j, 0), 0))
    hbm = pl.BlockSpec(memory_space=pl.ANY)
    state_blk = pl.BlockSpec((None, nseq, CONV_K - 1, CONV_WIDTH), lambda l, j: (l, j, 0, 0))
    in_specs = ([smem, x_blk, cache_blk, cache_blk, state_blk]
                + [_layer_spec(wts[name], lambda l, j: l) for name in SMALL_WEIGHT_NAMES]
                + [hbm] * len(BIG_WEIGHT_NAMES))
    out_shape = (jax.ShapeDtypeStruct((n, d), F32),
                 jax.ShapeDtypeStruct(ck.shape, F32),
                 jax.ShapeDtypeStruct(cv.shape, F32),
                 jax.ShapeDtypeStruct(st.shape, F32),
                 jax.ShapeDtypeStruct((depth, n, MLP_WIDTH), F32)
                 ) + tuple(jax.ShapeDtypeStruct(wts[name].shape, BF16) for name in BIG_WEIGHT_NAMES)
    out_specs = ((y_blk, cache_blk, cache_blk, state_blk, rows_blk(MLP_WIDTH))
                 + (hbm,) * len(BIG_WEIGHT_NAMES))
    scratch = [pltpu.VMEM((n, d), F32),
               pltpu.VMEM((stack, LANES), F32),
               pltpu.VMEM((nseq, stack, LANES), F32),
               pltpu.VMEM((nseq, stack, LANES), F32),
               pltpu.VMEM((nseq, stack, LANES), F32),
               pltpu.VMEM((nseq, stack, LANES), F32),
               pltpu.VMEM((nseq, stack, LANES), F32)]
    scratch += [pltpu.VMEM(wts[name].shape[1:], BF16) for name in BIG_WEIGHT_NAMES]
    scratch += [pltpu.VMEM((STAGE_SLOTS, STAGE_ROWS_W_IN, IN_COLS), F32),
                pltpu.VMEM((STAGE_SLOTS, STAGE_ROWS_SQUARE, d), F32),
                pltpu.SemaphoreType.DMA((STAGE_SLOTS,)), pltpu.SemaphoreType.DMA((STAGE_SLOTS,)),
                pltpu.SemaphoreType.DMA((len(BIG_WEIGHT_NAMES),)),
                pltpu.VMEM((nseq, dec_seq, CONV_WIDTH), F32)]
    return pl.pallas_call(
        functools.partial(_sample_kernel, dec_seq=dec_seq),
        grid=(depth, n // rows),
        in_specs=in_specs, out_specs=out_specs, out_shape=out_shape, scratch_shapes=scratch,
        compiler_params=pltpu.CompilerParams(dimension_semantics=("arbitrary", "arbitrary"),
                                             vmem_limit_bytes=VMEM_LIMIT_BYTES),
        name="sample_layers",
    )(wts["sinks"], x, ck, cv, st, *[wts[name] for name in SMALL_WEIGHT_NAMES + BIG_WEIGHT_NAMES])


def kernel(x_prompt, x_sample, cache_k, cache_v, state_conv, norm_w, w_in, b_gate, q_norm_w, k_norm_w,
           sinks, conv_w, v_norm_w, w_spatial, b_spatial, w_out_a, w_out_b, w_out_c, w_o):
    batch, seq, d = x_prompt.shape
    dec_batch, dec_seq, _ = x_sample.shape
    depth = w_in.shape[0]
    w_buf = cache_k.shape[2]
    assert d == D_MODEL and seq % PROMPT_ROWS == 0 and (dec_batch * dec_seq) % SAMPLE_ROWS == 0
    assert w_buf == WINDOW and dec_seq == SUBLANES and w_in.shape[2] == IN_COLS

    gw = MLP_WIDTH // N_SPATIAL_GROUPS
    reps = CHUNK // dec_seq
    spb_p = jnp.repeat(jnp.swapaxes(b_spatial, 1, 2), gw, axis=2)
    common = {
        "sinks": sinks,
        "norm_w": norm_w[:, None, :], "w_in": w_in, "b_gate": b_gate[:, None, :],
        "qw": jnp.tile(q_norm_w, (1, LANES // HEAD_DIM))[:, None, :],
        "kw": jnp.tile(k_norm_w, (1, LANES // HEAD_DIM))[:, None, :],
        "conv_w": conv_w, "vnw": v_norm_w[:, None, :],
        "woa": w_out_a, "wob": w_out_b, "woc": w_out_c, "wo": w_o,
    }
    wts_s = dict(common,
                 mix=jnp.tile(w_spatial[:, :, :dec_seq, :dec_seq], (1, 1, 1, reps)),
                 spb=spb_p[:, :dec_seq, :])

    ck = jnp.transpose(cache_k, (0, 1, 3, 4, 2))
    cv = jnp.transpose(cache_v, (0, 1, 3, 4, 2))

    ys, sk, sv, sc, scv, *big_bf16 = _sample_layers(x_sample.reshape(dec_batch * dec_seq, d), ck, cv,
                                                    state_conv, wts_s, dec_seq)
    wts_p = dict(common, mix=w_spatial, spb=spb_p, **dict(zip(BIG_WEIGHT_NAMES, big_bf16)))
    sk = jnp.transpose(sk, (0, 1, 4, 2, 3))
    sv = jnp.transpose(sv, (0, 1, 4, 2, 3))
    scv = scv.reshape(depth, dec_batch, dec_seq, MLP_WIDTH)

    yp = x_prompt.reshape(batch * seq, d)
    pk, pv, pc = [], [], []
    for l in range(depth):
        yp, k_l, v_l, c_l = _prompt_layer(yp, wts_p, l, batch)
        pk.append(k_l.reshape(batch, WINDOW, N_KV_HEADS, HEAD_DIM))
        pv.append(v_l.reshape(batch, WINDOW, N_KV_HEADS, HEAD_DIM))
        pc.append(c_l[:, SUBLANES - (CONV_K - 1):, :])

    return (yp.reshape(batch, seq, d), ys.reshape(dec_batch, dec_seq, d), jnp.stack(pk), jnp.stack(pv),
            jnp.stack(pc), sk, sv, sc, scv)
```

```python
import functools

import jax
import jax.numpy as jnp
from jax import lax
from jax.experimental import pallas as pl
from jax.experimental.pallas import tpu as pltpu

F32 = jnp.float32
BF16 = jnp.bfloat16

D_MODEL = 1024
N_HEADS = 8
N_KV_HEADS = 2
HEAD_DIM = 64
Q_GROUP = N_HEADS // N_KV_HEADS
ATTN_WIDTH = N_HEADS * HEAD_DIM
KV_WIDTH = N_KV_HEADS * HEAD_DIM
WINDOW = 128
CONV_WIDTH = 512
CONV_K = 3
CHUNK = 128
MLP_WIDTH = 512
N_SPATIAL_GROUPS = 4
EPS = 1e-6
NEG_INF = -1e30
LOG2E = 1.4426950408889634

COL_A = 0
COL_B = COL_A + 2 * ATTN_WIDTH + 2 * KV_WIDTH
COL_C = COL_B + 4 * CONV_WIDTH
COL_G = COL_C + 3 * MLP_WIDTH
IN_COLS = COL_G + 3 * D_MODEL

LANES = 128
SUBLANES = 8
PROMPT_ROWS = 512
SAMPLE_ROWS = 128
VMEM_LIMIT_BYTES = 56 * 1024 * 1024


def _dot(a, b):
    return jnp.dot(a, b, preferred_element_type=F32)


def _dot_nt(a, b):
    return lax.dot_general(a, b, (((1,), (1,)), ((), ())), preferred_element_type=F32)


def _rms(x, w):
    ms = jnp.mean(x * x, axis=-1, keepdims=True)
    return (x * lax.rsqrt(ms + EPS)) * w


def _twice_sigmoid(x):
    return jnp.tanh(0.5 * x) + 1.0


def _silu(x):
    h = 0.5 * x
    return h * (jnp.tanh(h) + 1.0)


def _lo_lanes(shape):
    return lax.broadcasted_iota(jnp.int32, shape, len(shape) - 1) < HEAD_DIM


def _pair_rms(x, w):
    lo = _lo_lanes(x.shape)
    sq = x * x
    s_lo = jnp.sum(jnp.where(lo, sq, 0.0), axis=-1, keepdims=True)
    s_hi = jnp.sum(jnp.where(lo, 0.0, sq), axis=-1, keepdims=True)
    r = lax.rsqrt(jnp.where(lo, s_lo, s_hi) + HEAD_DIM * EPS)
    return (x * r) * (w * HEAD_DIM ** 0.5)


def _norm_heads(x, w):
    groups = [_pair_rms(x[:, g * LANES:(g + 1) * LANES], w) for g in range(x.shape[1] // LANES)]
    return groups[0] if len(groups) == 1 else jnp.concatenate(groups, axis=1)


def _slope(head):
    return 2.0 ** (-(head + 1))


def _branch_b(hb, conv_shift, conv_w_ref, wob_ref):
    gate_b = hb[:, 0:CONV_WIDTH]
    gate_c = hb[:, CONV_WIDTH:2 * CONV_WIDTH]
    h_b = hb[:, 2 * CONV_WIDTH:3 * CONV_WIDTH]
    z_b = hb[:, 3 * CONV_WIDTH:4 * CONV_WIDTH]
    conv_in = gate_c * h_b
    xm1, xm2 = conv_shift(conv_in)
    cw = conv_w_ref[...]
    conv_out = cw[0:1] * xm2 + cw[1:2] * xm1 + cw[2:3] * conv_in
    y_b = gate_b * conv_out
    return _dot((_silu(z_b) * y_b).astype(BF16), wob_ref[...]), conv_in


def _branch_c(hc, mix_mask, mix_rows, spb, vnw_ref, woc_ref):
    rows = hc.shape[0]
    u = hc[:, 0:MLP_WIDTH]
    v_c = _rms(hc[:, MLP_WIDTH:2 * MLP_WIDTH], vnw_ref[...])
    z_c = hc[:, 2 * MLP_WIDTH:3 * MLP_WIDTH]
    vb = v_c.astype(BF16)
    gw = MLP_WIDTH // N_SPATIAL_GROUPS
    mixes = [jnp.where(mix_mask, mix_rows(g), 0.0).astype(BF16) for g in range(N_SPATIAL_GROUPS)]
    sp_chunks = []
    for c in range(rows // CHUNK):
        r0 = c * CHUNK
        parts = [_dot(mixes[g], vb[r0:r0 + CHUNK, g * gw:(g + 1) * gw])
                 for g in range(N_SPATIAL_GROUPS)]
        sp_chunks.append(jnp.concatenate(parts, axis=1) + spb)
    sp = sp_chunks[0] if len(sp_chunks) == 1 else jnp.concatenate(sp_chunks, axis=0)
    y_c = u * sp
    return _dot((_silu(z_c) * y_c).astype(BF16), woc_ref[...]), v_c


def _branch_a(y_a, z_a, woa_ref):
    return _dot((_silu(z_a) * y_a).astype(BF16), woa_ref[...])


def _merge(x, gated_a, gated_b, gated_c, wo_ref):
    m = gated_a + gated_b + gated_c
    return x + 0.5 * _dot(m.astype(BF16), wo_ref[...])


def _prompt_kernel(sinks_ref, x_ref, norm_w_ref, w_in_ref, b_gate_ref, qw_ref, kw_ref, conv_w_ref,
                   vnw_ref, mix_ref, spb_ref, woa_ref, wob_ref, woc_ref, wo_ref,
                   y_ref, pk_ref, pv_ref, pc_ref,
                   bias_scr, ka_scr, kb_scr, vt_scr, carry_scr, *, rows, nblk, layer):
    i = pl.program_id(0)
    first = lax.rem(i, nblk) == 0
    nsub = rows // WINDOW
    stack = 2 * WINDOW

    @pl.when(i == 0)
    def _():
        key = lax.broadcasted_iota(jnp.int32, (2 * WINDOW, stack), 0)
        c = lax.broadcasted_iota(jnp.int32, (2 * WINDOW, stack), 1)
        left = c < WINDOW
        dist = jnp.where(left, c, c - WINDOW) + WINDOW - key
        band = (dist >= 0) & (dist < WINDOW)
        distf = dist.astype(F32)
        for flag in range(2):
            valid = band & (key >= WINDOW) if flag else band
            for h in range(N_KV_HEADS):
                for half in range(2):
                    slope = jnp.where(left, _slope(h * Q_GROUP + half), _slope(h * Q_GROUP + 2 + half))
                    bias_scr[flag * 4 + h * 2 + half] = jnp.where(valid, -(slope * distf) * LOG2E, NEG_INF)

    @pl.when(first)
    def _():
        zeros = jnp.zeros((WINDOW, KV_WIDTH), BF16)
        ka_scr[0:WINDOW, :] = zeros
        kb_scr[0:WINDOW, :] = zeros
        vt_scr[:, 0:WINDOW] = zeros
        carry_scr[...] = jnp.zeros(carry_scr.shape, F32)

    x = x_ref[...]
    xn = _rms(x, norm_w_ref[...]).astype(BF16)
    def proj(c0, c1):
        return _dot(xn, w_in_ref[:, c0:c1])

    def gate(idx):
        g = proj(COL_G + idx * D_MODEL, COL_G + (idx + 1) * D_MODEL)
        return _twice_sigmoid(g + b_gate_ref[:, idx * D_MODEL:(idx + 1) * D_MODEL])

    ha = proj(COL_A, COL_B)
    hb = proj(COL_B, COL_C)
    q = _norm_heads(ha[:, 0:ATTN_WIDTH], qw_ref[...] * (HEAD_DIM ** -0.5 * LOG2E))
    k = _pair_rms(ha[:, ATTN_WIDTH:ATTN_WIDTH + KV_WIDTH], kw_ref[...])
    v = ha[:, ATTN_WIDTH + KV_WIDTH:ATTN_WIDTH + 2 * KV_WIDTH]
    z_a = ha[:, ATTN_WIDTH + 2 * KV_WIDTH:COL_B]

    pk_ref[0] = k[rows - WINDOW:rows, :]
    pv_ref[0] = v[rows - WINDOW:rows, :]
    ka_scr[WINDOW:WINDOW + rows, :] = k.astype(BF16)
    kb_scr[WINDOW:WINDOW + rows, :] = pltpu.roll(k, HEAD_DIM, axis=1).astype(BF16)
    for c in range(nsub):
        vt_scr[:, (c + 1) * WINDOW:(c + 2) * WINDOW] = jnp.transpose(
            v[c * WINDOW:(c + 1) * WINDOW, :]).astype(BF16)

    lo = _lo_lanes((WINDOW, LANES))
    scol = lax.broadcasted_iota(jnp.int32, (1, stack), 1) < WINDOW
    flag = first.astype(jnp.int32)
    carry = carry_scr[...]
    prev1 = carry[SUBLANES - 1:SUBLANES, :]
    prev2 = carry[SUBLANES - 2:SUBLANES - 1, :]

    def conv_shift(ci):
        rid = lax.broadcasted_iota(jnp.int32, ci.shape, 0)
        xm1 = jnp.where(rid == 0, prev1, pltpu.roll(ci, 1, axis=0))
        xm2 = jnp.where(rid == 0, prev2, jnp.where(rid == 1, prev1, pltpu.roll(ci, 2, axis=0)))
        return xm1, xm2

    mr = lax.broadcasted_iota(jnp.int32, (CHUNK, CHUNK), 0)
    mc = lax.broadcasted_iota(jnp.int32, (CHUNK, CHUNK), 1)
    done = {}
    stages = [
        lambda: done.update(hc=proj(COL_C, COL_G)),
        lambda: done.update(b=_branch_b(hb, conv_shift, conv_w_ref, wob_ref)),
        lambda: done.update(gated_b=gate(1) * done["b"][0]),
        lambda: done.update(c=_branch_c(done["hc"], mc <= mr, lambda g: mix_ref[g], spb_ref[...],
                                        vnw_ref, woc_ref)),
        lambda: done.update(gated_c=gate(2) * done["c"][0]),
        lambda: done.update(g0=gate(0)),
    ]
    slots = 2 * nsub
    order = list(range(0, slots, 2)) + list(range(1, slots, 2))
    per_slot = [0] * slots
    for n in range(len(stages)):
        per_slot[order[n % slots]] += 1

    def run_stages(slot):
        for _ in range(per_slot[slot]):
            stages.pop(0)()

    y_rows = []
    for c in range(nsub):
        r0 = c * WINDOW
        keys = slice(r0, r0 + 2 * WINDOW)
        k_nat, k_swp = ka_scr[keys, :], kb_scr[keys, :]
        vt = vt_scr[:, keys]
        scores = []
        for h in range(N_KV_HEADS):
            p0 = q[r0:r0 + WINDOW, (2 * h) * LANES:(2 * h + 1) * LANES]
            p1 = q[r0:r0 + WINDOW, (2 * h + 1) * LANES:(2 * h + 2) * LANES]
            for half in range(2):
                keep = lo if half == 0 else jnp.logical_not(lo)
                qs = jnp.concatenate([jnp.where(keep, p0, 0.0), jnp.where(keep, p1, 0.0)],
                                     axis=0).astype(BF16)
                kk = k_nat if h == half else k_swp
                tbl = h * 2 + half
                bias = bias_scr[flag * 4 + tbl] if c == 0 else bias_scr[tbl]
                scores.append(_dot_nt(kk, qs) + bias)
        run_stages(2 * c)
        pairs = []
        for h in range(N_KV_HEADS):
            outs = []
            for half in range(2):
                s = scores[h * 2 + half]
                sink = jnp.where(scol, sinks_ref[layer, h * Q_GROUP + half],
                                 sinks_ref[layer, h * Q_GROUP + 2 + half]) * LOG2E
                mx = jnp.maximum(jnp.max(s, axis=0, keepdims=True), sink)
                e = jnp.exp2(s - mx)
                den = jnp.sum(e, axis=0, keepdims=True) + jnp.exp2(sink - mx)
                outs.append(_dot(vt[h * HEAD_DIM:(h + 1) * HEAD_DIM, :], e.astype(BF16)) * (1.0 / den))
            for p in range(2):
                cols = slice(p * WINDOW, (p + 1) * WINDOW)
                pairs.append(jnp.transpose(jnp.concatenate([outs[0][:, cols], outs[1][:, cols]], axis=0)))
        y_rows.append(jnp.concatenate(pairs, axis=1))
        run_stages(2 * c + 1)
    y_a = y_rows[0] if nsub == 1 else jnp.concatenate(y_rows, axis=0)

    ka_scr[0:WINDOW, :] = ka_scr[rows:rows + WINDOW, :]
    kb_scr[0:WINDOW, :] = kb_scr[rows:rows + WINDOW, :]
    vt_scr[:, 0:WINDOW] = vt_scr[:, rows:rows + WINDOW]

    gated_a = done["g0"] * _branch_a(y_a, z_a, woa_ref)
    y_ref[...] = _merge(x, gated_a, done["gated_b"], done["gated_c"], wo_ref)
    conv_in = done["b"][1]
    last = conv_in[rows - SUBLANES:rows, :]
    carry_scr[...] = last
    pc_ref[0] = last


def _stage_bf16(streams, layer):
    def copy(s, k):
        jobs, stage, sem = streams[s]
        src, _, r0 = jobs[k]
        slot = k % stage.shape[0]
        return pltpu.make_async_copy(src.at[layer, pl.ds(r0, stage.shape[1]), :], stage.at[slot], sem.at[slot])

    for s, (jobs, stage, _) in enumerate(streams):
        for k in range(min(stage.shape[0] - 1, len(jobs))):
            copy(s, k).start()
    for k in range(max(len(jobs) for jobs, _, _ in streams)):
        for s, (jobs, stage, _) in enumerate(streams):
            if k >= len(jobs):
                continue
            nslots, chunk_rows = stage.shape[0], stage.shape[1]
            if k + nslots - 1 < len(jobs):
                copy(s, k + nslots - 1).start()
            copy(s, k).wait()
            _, dst, r0 = jobs[k]
            dst[r0:r0 + chunk_rows, :] = stage[k % nslots].astype(BF16)


def _sample_kernel(sinks_ref, x_ref, ck_ref, cv_ref, st_ref, norm_w_ref, b_gate_ref, qw_ref,
                   kw_ref, conv_w_ref, vnw_ref, mix_ref, spb_ref,
                   w_in_hbm, woa_hbm, wob_hbm, woc_hbm, wo_hbm,
                   y_ref, sk_ref, sv_ref, ci_ref, scv_ref,
                   w_in_out, woa_out, wob_out, woc_out, wo_out,
                   x_scr, bold_scr, bnew_scr, q_scr, en_scr, oo_scr, inv_scr,
                   w_in_ref, woa_ref, wob_ref, woc_ref, wo_ref, stage_in, stage_sq,
                   sem_in, sem_sq, sem_out, st_scr, *, dec_seq):
    layer = pl.program_id(0)
    j = pl.program_id(1)
    rows = SAMPLE_ROWS
    nseq = rows // dec_seq
    stack = N_HEADS * dec_seq
    w = WINDOW

    @pl.when((layer == 0) & (j == 0))
    def _():
        st_scr[...] = jnp.zeros(st_scr.shape, F32)
        r = lax.broadcasted_iota(jnp.int32, (stack, LANES), 0)
        col = lax.broadcasted_iota(jnp.int32, (stack, LANES), 1)
        tok = lax.rem(r, dec_seq)
        head = r // dec_seq
        slope = jnp.zeros((stack, LANES), F32)
        for n in range(N_HEADS):
            slope = jnp.where(head == n, _slope(n), slope)
        dist = tok + w - col
        bold_scr[...] = jnp.where(dist < WINDOW, -(slope * dist.astype(F32)), NEG_INF)
        kseq = col // dec_seq
        dist = tok - lax.rem(col, dec_seq)
        pen = -(slope * dist.astype(F32))
        for b in range(nseq):
            bnew_scr[b] = jnp.where((kseq == b) & (dist >= 0), pen, NEG_INF)

    r0 = pl.multiple_of(j * rows, rows)

    resident = ((w_in_hbm, w_in_ref, w_in_out), (woa_hbm, woa_ref, woa_out), (wob_hbm, wob_ref, wob_out),
                (woc_hbm, woc_ref, woc_out), (wo_hbm, wo_ref, wo_out))

    def publish(n, lyr):
        return pltpu.make_async_copy(resident[n][1], resident[n][2].at[lyr], sem_out.at[n])

    @pl.when(j == 0)
    def _():
        @pl.when(layer > 0)
        def _():
            for n in range(len(resident)):
                publish(n, layer - 1).wait()

        wide = [(w_in_hbm, w_in_ref, r) for r in range(0, w_in_ref.shape[0], stage_in.shape[1])]
        square = [(src, dst, r) for src, dst, _ in resident[1:]
                  for r in range(0, dst.shape[0], stage_sq.shape[1])]
        _stage_bf16([(wide, stage_in, sem_in), (square, stage_sq, sem_sq)], layer)
        for n in range(len(resident)):
            publish(n, layer).start()

    @pl.when(layer == 0)
    def _():
        x_scr[pl.ds(r0, rows), :] = x_ref[...]

    x = x_scr[pl.ds(r0, rows), :]
    xn = _rms(x, norm_w_ref[...]).astype(BF16)
    def proj(c0, c1):
        return _dot(xn, w_in_ref[:, c0:c1])

    def gate(idx):
        g = proj(COL_G + idx * D_MODEL, COL_G + (idx + 1) * D_MODEL)
        return _twice_sigmoid(g + b_gate_ref[:, idx * D_MODEL:(idx + 1) * D_MODEL])

    ha = proj(COL_A, COL_B)
    hb = proj(COL_B, COL_C)
    q = _norm_heads(ha[:, 0:ATTN_WIDTH], qw_ref[...] * (HEAD_DIM ** -0.5))
    k = _pair_rms(ha[:, ATTN_WIDTH:ATTN_WIDTH + KV_WIDTH], kw_ref[...])
    v = ha[:, ATTN_WIDTH + KV_WIDTH:ATTN_WIDTH + 2 * KV_WIDTH]
    z_a = ha[:, ATTN_WIDTH + 2 * KV_WIDTH:COL_B]

    lo = _lo_lanes((rows, LANES))
    for h in range(N_KV_HEADS):
        keep = lo if h == 0 else jnp.logical_not(lo)
        for g in range(Q_GROUP):
            pair = q[:, (h * 2 + g // 2) * LANES:(h * 2 + g // 2 + 1) * LANES]
            if g % 2 != h:
                pair = pltpu.roll(pair, HEAD_DIM, axis=1)
            piece = jnp.where(keep, pair, 0.0).reshape(nseq, dec_seq, LANES)
            s0 = (h * Q_GROUP + g) * dec_seq
            q_scr[:, s0:s0 + dec_seq, :] = piece

    kt_new = jnp.transpose(k)
    vt_new = jnp.transpose(v)
    qall = q_scr[...].reshape(nseq * stack, LANES).astype(BF16)
    s_new_all = _dot(qall, kt_new.astype(BF16)).reshape(nseq, stack, LANES)

    srow = lax.broadcasted_iota(jnp.int32, (stack, 1), 0) // dec_seq
    sink = jnp.zeros((stack, 1), F32)
    for n in range(N_HEADS):
        sink = jnp.where(srow == n, sinks_ref[layer, n], sink)
    bias_old = bold_scr[...]
    lane = lax.broadcasted_iota(jnp.int32, (KV_WIDTH, w), 1)
    s_olds = [_dot(q_scr[b].astype(BF16), ck_ref[b].reshape(KV_WIDTH, w).astype(BF16)) + bias_old
              for b in range(nseq)]
    hc = proj(COL_C, COL_G)
    gates = []
    for b in range(nseq):
        if b % (nseq // 4) == 0 and 0 < b:
            gates.append(gate(len(gates)))
        kt = ck_ref[b].reshape(KV_WIDTH, w)
        vt = cv_ref[b].reshape(KV_WIDTH, w)
        s_old = s_olds[b]
        s_new = s_new_all[b] + bnew_scr[b]
        mx = jnp.maximum(jnp.maximum(jnp.max(s_old, axis=-1, keepdims=True),
                                     jnp.max(s_new, axis=-1, keepdims=True)), sink)
        e_old = jnp.exp(s_old - mx)
        e_new = jnp.exp(s_new - mx)
        den = (jnp.sum(e_old, axis=-1, keepdims=True) + jnp.sum(e_new, axis=-1, keepdims=True)
               + jnp.exp(sink - mx))
        en_scr[b] = e_new
        oo_scr[b] = _dot_nt(e_old.astype(BF16), vt.astype(BF16))
        inv_scr[b] = jnp.broadcast_to(1.0 / den, (stack, LANES))
        shift = w - dec_seq - b * dec_seq
        newk = pltpu.roll(kt_new, shift, axis=1) if shift else kt_new
        newv = pltpu.roll(vt_new, shift, axis=1) if shift else vt_new
        keep_old = lane < w - dec_seq
        sk_ref[b] = jnp.where(keep_old, pltpu.roll(kt, w - dec_seq, axis=1), newk).reshape(
            N_KV_HEADS, HEAD_DIM, w)
        sv_ref[b] = jnp.where(keep_old, pltpu.roll(vt, w - dec_seq, axis=1), newv).reshape(
            N_KV_HEADS, HEAD_DIM, w)

    o_new = _dot(en_scr[...].reshape(nseq * stack, LANES).astype(BF16), v.astype(BF16))
    o = (oo_scr[...] + o_new.reshape(nseq, stack, LANES)) * inv_scr[...]
    pairs = []
    for h in range(N_KV_HEADS):
        heads = []
        for g in range(Q_GROUP):
            s0 = (h * Q_GROUP + g) * dec_seq
            og = o[:, s0:s0 + dec_seq, :].reshape(rows, LANES)
            if g % 2 != h:
                og = pltpu.roll(og, HEAD_DIM, axis=1)
            heads.append(og)
        pairs.append(jnp.where(lo, heads[0], heads[1]))
        pairs.append(jnp.where(lo, heads[2], heads[3]))
    y_a = jnp.concatenate(pairs, axis=1)

    st_scr[:, 0:CONV_K - 1, :] = st_ref[...]
    st = st_scr[...].reshape(rows, CONV_WIDTH)

    def conv_shift(ci):
        t = lax.rem(lax.broadcasted_iota(jnp.int32, ci.shape, 0), dec_seq)
        xm1 = jnp.where(t == 0, pltpu.roll(st, rows - 1, axis=0), pltpu.roll(ci, 1, axis=0))
        xm2 = jnp.where(t < CONV_K - 1, st, pltpu.roll(ci, 2, axis=0))
        return xm1, xm2

    mr = lax.broadcasted_iota(jnp.int32, (CHUNK, CHUNK), 0)
    mc = lax.broadcasted_iota(jnp.int32, (CHUNK, CHUNK), 1)
    mix_mask = (mr // dec_seq == mc // dec_seq) & (mc <= mr)
    b_out, conv_in = _branch_b(hb, conv_shift, conv_w_ref, wob_ref)
    reps = CHUNK // dec_seq

    def mix_rows(g):
        return jnp.broadcast_to(mix_ref[g][None], (reps, dec_seq, CHUNK)).reshape(CHUNK, CHUNK)

    spb = jnp.broadcast_to(spb_ref[...][None], (reps, dec_seq, MLP_WIDTH)).reshape(CHUNK, MLP_WIDTH)
    c_out, v_c = _branch_c(hc, mix_mask, mix_rows, spb, vnw_ref, woc_ref)
    y = _merge(x, gates[0] * _branch_a(y_a, z_a, woa_ref), gates[1] * b_out, gates[2] * c_out, wo_ref)
    x_scr[pl.ds(r0, rows), :] = y

    @pl.when(layer == pl.num_programs(0) - 1)
    def _():
        y_ref[...] = y
    ci_ref[...] = conv_in.reshape(nseq, dec_seq, CONV_WIDTH)[:, dec_seq - (CONV_K - 1):, :]
    scv_ref[...] = v_c

    @pl.when((layer == pl.num_programs(0) - 1) & (j == pl.num_programs(1) - 1))
    def _():
        for n in range(len(resident)):
            publish(n, layer).wait()


SMALL_WEIGHT_NAMES = ("norm_w", "b_gate", "qw", "kw", "conv_w", "vnw", "mix", "spb")
BIG_WEIGHT_NAMES = ("w_in", "woa", "wob", "woc", "wo")
STAGE_SLOTS = 8
STAGE_ROWS_W_IN = 32
STAGE_ROWS_SQUARE = 128
WEIGHT_NAMES = ("norm_w", "w_in", "b_gate", "qw", "kw", "conv_w", "vnw", "mix", "spb",
                "woa", "wob", "woc", "wo")


def _layer_spec(arr, layer_of):
    nd = arr.ndim - 1
    return pl.BlockSpec((None,) + arr.shape[1:], lambda *g, _nd=nd: (layer_of(*g),) + (0,) * _nd,
                        pipeline_mode=pl.Buffered(1))


def _prompt_layer(x, wts, layer, batch):
    n, d = x.shape
    rows = PROMPT_ROWS
    nblk = n // batch // rows
    smem = pl.BlockSpec(memory_space=pltpu.SMEM)
    in_specs = ([smem, pl.BlockSpec((rows, d), lambda i: (i, 0))]
                + [_layer_spec(wts[name], lambda i: layer) for name in WEIGHT_NAMES])
    out_shape = (jax.ShapeDtypeStruct((n, d), F32),
                 jax.ShapeDtypeStruct((batch, WINDOW, KV_WIDTH), F32),
                 jax.ShapeDtypeStruct((batch, WINDOW, KV_WIDTH), F32),
                 jax.ShapeDtypeStruct((batch, SUBLANES, CONV_WIDTH), F32))
    out_specs = (pl.BlockSpec((rows, d), lambda i: (i, 0)),
                 pl.BlockSpec((1, WINDOW, KV_WIDTH), lambda i: (i // nblk, 0, 0)),
                 pl.BlockSpec((1, WINDOW, KV_WIDTH), lambda i: (i // nblk, 0, 0)),
                 pl.BlockSpec((1, SUBLANES, CONV_WIDTH), lambda i: (i // nblk, 0, 0)))
    n_tables = 2 * N_KV_HEADS * 2
    scratch = [pltpu.VMEM((n_tables, 2 * WINDOW, 2 * WINDOW), F32)]
    scratch += [pltpu.VMEM((WINDOW + rows, KV_WIDTH), BF16) for _ in range(2)]
    scratch += [pltpu.VMEM((KV_WIDTH, WINDOW + rows), BF16)]
    scratch += [pltpu.VMEM((SUBLANES, CONV_WIDTH), F32)]
    return pl.pallas_call(
        functools.partial(_prompt_kernel, rows=rows, nblk=nblk, layer=layer),
        grid=(n // rows,),
        in_specs=in_specs, out_specs=out_specs, out_shape=out_shape, scratch_shapes=scratch,
        compiler_params=pltpu.CompilerParams(dimension_semantics=("arbitrary",),
                                             vmem_limit_bytes=VMEM_LIMIT_BYTES),
        name="prompt_layer",
    )(wts["sinks"], x, *[wts[name] for name in WEIGHT_NAMES])


def _sample_layers(x, ck, cv, st, wts, dec_seq):
    n, d = x.shape
    depth = ck.shape[0]
    rows = SAMPLE_ROWS
    nseq = rows // dec_seq
    stack = N_HEADS * dec_seq
    w = ck.shape[-1]
    smem = pl.BlockSpec(memory_space=pltpu.SMEM)
    cache_blk = pl.BlockSpec((None, nseq, N_KV_HEADS, HEAD_DIM, w), lambda l, j: (l, j, 0, 0, 0))
    rows_blk = lambda width: pl.BlockSpec((None, rows, width), lambda l, j: (l, j, 0))
    nblk = n // rows
    x_blk = pl.BlockSpec((rows, d), lambda l, j: (jnp.where(l == 0, j, nblk - 1), 0))
    y_blk = pl.BlockSpec((rows, d), lambda l, j: (jnp.where(l == depth - 1, j, 0), 0))
    hbm = pl.BlockSpec(memory_space=pl.ANY)
    state_blk = pl.BlockSpec((None, nseq, CONV_K - 1, CONV_WIDTH), lambda l, j: (l, j, 0, 0))
    in_specs = ([smem, x_blk, cache_blk, cache_blk, state_blk]
                + [_layer_spec(wts[name], lambda l, j: l) for name in SMALL_WEIGHT_NAMES]
                + [hbm] * len(BIG_WEIGHT_NAMES))
    out_shape = (jax.ShapeDtypeStruct((n, d), F32),
                 jax.ShapeDtypeStruct(ck.shape, F32),
                 jax.ShapeDtypeStruct(cv.shape, F32),
                 jax.ShapeDtypeStruct(st.shape, F32),
                 jax.ShapeDtypeStruct((depth, n, MLP_WIDTH), F32)
                 ) + tuple(jax.ShapeDtypeStruct(wts[name].shape, BF16) for name in BIG_WEIGHT_NAMES)
    out_specs = ((y_blk, cache_blk, cache_blk, state_blk, rows_blk(MLP_WIDTH))
                 + (hbm,) * len(BIG_WEIGHT_NAMES))
    scratch = [pltpu.VMEM((n, d), F32),
               pltpu.VMEM((stack, LANES), F32),
               pltpu.VMEM((nseq, stack, LANES), F32),
               pltpu.VMEM((nseq, stack, LANES), F32),
               pltpu.VMEM((nseq, stack, LANES), F32),
               pltpu.VMEM((nseq, stack, LANES), F32),
               pltpu.VMEM((nseq, stack, LANES), F32)]
    scratch += [pltpu.VMEM(wts[name].shape[1:], BF16) for name in BIG_WEIGHT_NAMES]
    scratch += [pltpu.VMEM((STAGE_SLOTS, STAGE_ROWS_W_IN, IN_COLS), F32),
                pltpu.VMEM((STAGE_SLOTS, STAGE_ROWS_SQUARE, d), F32),
                pltpu.SemaphoreType.DMA((STAGE_SLOTS,)), pltpu.SemaphoreType.DMA((STAGE_SLOTS,)),
                pltpu.SemaphoreType.DMA((len(BIG_WEIGHT_NAMES),)),
                pltpu.VMEM((nseq, dec_seq, CONV_WIDTH), F32)]
    return pl.pallas_call(
        functools.partial(_sample_kernel, dec_seq=dec_seq),
        grid=(depth, n // rows),
        in_specs=in_specs, out_specs=out_specs, out_shape=out_shape, scratch_shapes=scratch,
        compiler_params=pltpu.CompilerParams(dimension_semantics=("arbitrary", "arbitrary"),
                                             vmem_limit_bytes=VMEM_LIMIT_BYTES),
        name="sample_layers",
    )(wts["sinks"], x, ck, cv, st, *[wts[name] for name in SMALL_WEIGHT_NAMES + BIG_WEIGHT_NAMES])


def kernel(x_prompt, x_sample, cache_k, cache_v, state_conv, norm_w, w_in, b_gate, q_norm_w, k_norm_w,
           sinks, conv_w, v_norm_w, w_spatial, b_spatial, w_out_a, w_out_b, w_out_c, w_o):
    batch, seq, d = x_prompt.shape
    dec_batch, dec_seq, _ = x_sample.shape
    depth = w_in.shape[0]
    w_buf = cache_k.shape[2]
    assert d == D_MODEL and seq % PROMPT_ROWS == 0 and (dec_batch * dec_seq) % SAMPLE_ROWS == 0
    assert w_buf == WINDOW and dec_seq == SUBLANES and w_in.shape[2] == IN_COLS

    gw = MLP_WIDTH // N_SPATIAL_GROUPS
    reps = CHUNK // dec_seq
    spb_p = jnp.repeat(jnp.swapaxes(b_spatial, 1, 2), gw, axis=2)
    common = {
        "sinks": sinks,
        "norm_w": norm_w[:, None, :], "w_in": w_in, "b_gate": b_gate[:, None, :],
        "qw": jnp.tile(q_norm_w, (1, LANES // HEAD_DIM))[:, None, :],
        "kw": jnp.tile(k_norm_w, (1, LANES // HEAD_DIM))[:, None, :],
        "conv_w": conv_w, "vnw": v_norm_w[:, None, :],
        "woa": w_out_a, "wob": w_out_b, "woc": w_out_c, "wo": w_o,
    }
    wts_s = dict(common,
                 mix=jnp.tile(w_spatial[:, :, :dec_seq, :dec_seq], (1, 1, 1, reps)),
                 spb=spb_p[:, :dec_seq, :])

    ck = jnp.transpose(cache_k, (0, 1, 3, 4, 2))
    cv = jnp.transpose(cache_v, (0, 1, 3, 4, 2))

    ys, sk, sv, sc, scv, *big_bf16 = _sample_layers(x_sample.reshape(dec_batch * dec_seq, d), ck, cv,
                                                    state_conv, wts_s, dec_seq)
    wts_p = dict(common, mix=w_spatial, spb=spb_p, **dict(zip(BIG_WEIGHT_NAMES, big_bf16)))
    sk = jnp.transpose(sk, (0, 1, 4, 2, 3))
    sv = jnp.transpose(sv, (0, 1, 4, 2, 3))
    scv = scv.reshape(depth, dec_batch, dec_seq, MLP_WIDTH)

    yp = x_prompt.reshape(batch * seq, d)
    pk, pv, pc = [], [], []
    for l in range(depth):
        yp, k_l, v_l, c_l = _prompt_layer(yp, wts_p, l, batch)
        pk.append(k_l.reshape(batch, WINDOW, N_KV_HEADS, HEAD_DIM))
        pv.append(v_l.reshape(batch, WINDOW, N_KV_HEADS, HEAD_DIM))
        pc.append(c_l[:, SUBLANES - (CONV_K - 1):, :])

    return (yp.reshape(batch, seq, d), ys.reshape(dec_batch, dec_seq, d), jnp.stack(pk), jnp.stack(pv),
            jnp.stack(pc), sk, sv, sc, scv)
```

```python
import functools

import jax
import jax.numpy as jnp
from jax import lax
from jax.experimental import pallas as pl
from jax.experimental.pallas import tpu as pltpu

F32 = jnp.float32
BF16 = jnp.bfloat16

D_MODEL = 1024
N_HEADS = 8
N_KV_HEADS = 2
HEAD_DIM = 64
Q_GROUP = N_HEADS // N_KV_HEADS
ATTN_WIDTH = N_HEADS * HEAD_DIM
KV_WIDTH = N_KV_HEADS * HEAD_DIM
WINDOW = 128
CONV_WIDTH = 512
CONV_K = 3
CHUNK = 128
MLP_WIDTH = 512
N_SPATIAL_GROUPS = 4
EPS = 1e-6
NEG_INF = -1e30
LOG2E = 1.4426950408889634

COL_A = 0
COL_B = COL_A + 2 * ATTN_WIDTH + 2 * KV_WIDTH
COL_C = COL_B + 4 * CONV_WIDTH
COL_G = COL_C + 3 * MLP_WIDTH
IN_COLS = COL_G + 3 * D_MODEL

LANES = 128
SUBLANES = 8
PROMPT_ROWS = 512
SAMPLE_ROWS = 128
VMEM_LIMIT_BYTES = 56 * 1024 * 1024


def _dot(a, b):
    return jnp.dot(a, b, preferred_element_type=F32)


def _dot_nt(a, b):
    return lax.dot_general(a, b, (((1,), (1,)), ((), ())), preferred_element_type=F32)


def _rms(x, w):
    ms = jnp.mean(x * x, axis=-1, keepdims=True)
    return (x * lax.rsqrt(ms + EPS)) * w


def _twice_sigmoid_of_twice(h):
    return jnp.tanh(h) + 1.0


def _silu_of_twice(h):
    return h * (jnp.tanh(h) + 1.0)


def _halved_columns():
    col = lax.broadcasted_iota(jnp.int32, (1, IN_COLS), 1)
    halved = col >= COL_G
    for z0, width in ((COL_B - ATTN_WIDTH, ATTN_WIDTH), (COL_C - CONV_WIDTH, CONV_WIDTH),
                      (COL_G - MLP_WIDTH, MLP_WIDTH)):
        halved = halved | ((col >= z0) & (col < z0 + width))
    return jnp.where(halved, 0.5, 1.0)


def _lo_lanes(shape):
    return lax.broadcasted_iota(jnp.int32, shape, len(shape) - 1) < HEAD_DIM


def _pair_rms(x, w):
    lo = _lo_lanes(x.shape)
    sq = x * x
    s_lo = jnp.sum(jnp.where(lo, sq, 0.0), axis=-1, keepdims=True)
    s_hi = jnp.sum(jnp.where(lo, 0.0, sq), axis=-1, keepdims=True)
    r = lax.rsqrt(jnp.where(lo, s_lo, s_hi) + HEAD_DIM * EPS)
    return (x * r) * (w * HEAD_DIM ** 0.5)


def _norm_heads(x, w):
    groups = [_pair_rms(x[:, g * LANES:(g + 1) * LANES], w) for g in range(x.shape[1] // LANES)]
    return groups[0] if len(groups) == 1 else jnp.concatenate(groups, axis=1)


def _slope(head):
    return 2.0 ** (-(head + 1))


def _branch_b(hb, conv_shift, conv_w_ref, wob_ref):
    gate_b = hb[:, 0:CONV_WIDTH]
    gate_c = hb[:, CONV_WIDTH:2 * CONV_WIDTH]
    h_b = hb[:, 2 * CONV_WIDTH:3 * CONV_WIDTH]
    z_b = hb[:, 3 * CONV_WIDTH:4 * CONV_WIDTH]
    conv_in = gate_c * h_b
    xm1, xm2 = conv_shift(conv_in)
    cw = conv_w_ref[...]
    conv_out = cw[0:1] * xm2 + cw[1:2] * xm1 + cw[2:3] * conv_in
    y_b = gate_b * conv_out
    return _dot((_silu_of_twice(z_b) * y_b).astype(BF16), wob_ref[...]), conv_in


def _branch_c(hc, mix_mask, mix_rows, spb, vnw_ref, woc_ref):
    rows = hc.shape[0]
    u = hc[:, 0:MLP_WIDTH]
    v_c = _rms(hc[:, MLP_WIDTH:2 * MLP_WIDTH], vnw_ref[...])
    z_c = hc[:, 2 * MLP_WIDTH:3 * MLP_WIDTH]
    vb = v_c.astype(BF16)
    gw = MLP_WIDTH // N_SPATIAL_GROUPS
    mixes = [jnp.where(mix_mask, mix_rows(g), 0.0).astype(BF16) for g in range(N_SPATIAL_GROUPS)]
    sp_chunks = []
    for c in range(rows // CHUNK):
        r0 = c * CHUNK
        parts = [_dot(mixes[g], vb[r0:r0 + CHUNK, g * gw:(g + 1) * gw])
                 for g in range(N_SPATIAL_GROUPS)]
        sp_chunks.append(jnp.concatenate(parts, axis=1) + spb)
    sp = sp_chunks[0] if len(sp_chunks) == 1 else jnp.concatenate(sp_chunks, axis=0)
    y_c = u * sp
    return _dot((_silu_of_twice(z_c) * y_c).astype(BF16), woc_ref[...]), v_c


def _branch_a(y_a, z_a, woa_ref):
    return _dot((_silu_of_twice(z_a) * y_a).astype(BF16), woa_ref[...])


def _merge(x, gated_a, gated_b, gated_c, wo_ref):
    m = gated_a + gated_b + gated_c
    return x + _dot(m.astype(BF16), wo_ref[...])


def _prompt_kernel(sinks_ref, x_ref, norm_w_ref, w_in_ref, b_gate_ref, qw_ref, kw_ref, conv_w_ref,
                   vnw_ref, mix_ref, spb_ref, woa_ref, wob_ref, woc_ref, wo_ref,
                   y_ref, pk_ref, pv_ref, pc_ref,
                   bias_scr, ka_scr, kb_scr, vt_scr, carry_scr, *, rows, nblk, layer):
    i = pl.program_id(0)
    first = lax.rem(i, nblk) == 0
    nsub = rows // WINDOW
    stack = 2 * WINDOW

    @pl.when(i == 0)
    def _():
        key = lax.broadcasted_iota(jnp.int32, (2 * WINDOW, stack), 0)
        c = lax.broadcasted_iota(jnp.int32, (2 * WINDOW, stack), 1)
        left = c < WINDOW
        dist = jnp.where(left, c, c - WINDOW) + WINDOW - key
        band = (dist >= 0) & (dist < WINDOW)
        distf = dist.astype(F32)
        for flag in range(2):
            valid = band & (key >= WINDOW) if flag else band
            for h in range(N_KV_HEADS):
                for half in range(2):
                    slope = jnp.where(left, _slope(h * Q_GROUP + half), _slope(h * Q_GROUP + 2 + half))
                    bias_scr[flag * 4 + h * 2 + half] = jnp.where(valid, -(slope * distf) * LOG2E, NEG_INF)

    @pl.when(first)
    def _():
        zeros = jnp.zeros((WINDOW, KV_WIDTH), BF16)
        ka_scr[0:WINDOW, :] = zeros
        kb_scr[0:WINDOW, :] = zeros
        vt_scr[:, 0:WINDOW] = zeros
        carry_scr[...] = jnp.zeros(carry_scr.shape, F32)

    x = x_ref[...]
    xn = _rms(x, norm_w_ref[...]).astype(BF16)
    def proj(c0, c1):
        return _dot(xn, w_in_ref[:, c0:c1])

    def gate(idx):
        g = proj(COL_G + idx * D_MODEL, COL_G + (idx + 1) * D_MODEL)
        return _twice_sigmoid_of_twice(g + 0.5 * b_gate_ref[:, idx * D_MODEL:(idx + 1) * D_MODEL])

    ha = proj(COL_A, COL_B)
    hb = proj(COL_B, COL_C)
    q = _norm_heads(ha[:, 0:ATTN_WIDTH], qw_ref[...] * (HEAD_DIM ** -0.5 * LOG2E))
    k = _pair_rms(ha[:, ATTN_WIDTH:ATTN_WIDTH + KV_WIDTH], kw_ref[...])
    v = ha[:, ATTN_WIDTH + KV_WIDTH:ATTN_WIDTH + 2 * KV_WIDTH]
    z_a = ha[:, ATTN_WIDTH + 2 * KV_WIDTH:COL_B]

    pk_ref[0] = k[rows - WINDOW:rows, :]
    pv_ref[0] = v[rows - WINDOW:rows, :]
    ka_scr[WINDOW:WINDOW + rows, :] = k.astype(BF16)
    kb_scr[WINDOW:WINDOW + rows, :] = pltpu.roll(k, HEAD_DIM, axis=1).astype(BF16)
    for c in range(nsub):
        vt_scr[:, (c + 1) * WINDOW:(c + 2) * WINDOW] = jnp.transpose(
            v[c * WINDOW:(c + 1) * WINDOW, :]).astype(BF16)

    lo = _lo_lanes((WINDOW, LANES))
    scol = lax.broadcasted_iota(jnp.int32, (1, stack), 1) < WINDOW
    flag = first.astype(jnp.int32)
    carry = carry_scr[...]
    prev1 = carry[SUBLANES - 1:SUBLANES, :]
    prev2 = carry[SUBLANES - 2:SUBLANES - 1, :]

    def conv_shift(ci):
        rid = lax.broadcasted_iota(jnp.int32, ci.shape, 0)
        xm1 = jnp.where(rid == 0, prev1, pltpu.roll(ci, 1, axis=0))
        xm2 = jnp.where(rid == 0, prev2, jnp.where(rid == 1, prev1, pltpu.roll(ci, 2, axis=0)))
        return xm1, xm2

    mr = lax.broadcasted_iota(jnp.int32, (CHUNK, CHUNK), 0)
    mc = lax.broadcasted_iota(jnp.int32, (CHUNK, CHUNK), 1)
    done = {}
    stages = [
        lambda: done.update(hc=proj(COL_C, COL_G)),
        lambda: done.update(b=_branch_b(hb, conv_shift, conv_w_ref, wob_ref)),
        lambda: done.update(gated_b=gate(1) * done["b"][0]),
        lambda: done.update(c=_branch_c(done["hc"], mc <= mr, lambda g: mix_ref[g], spb_ref[...],
                                        vnw_ref, woc_ref)),
        lambda: done.update(gated_c=gate(2) * done["c"][0]),
        lambda: done.update(g0=gate(0)),
    ]
    slots = 2 * nsub
    order = list(range(0, slots, 2)) + list(range(1, slots, 2))
    per_slot = [0] * slots
    for n in range(len(stages)):
        per_slot[order[n % slots]] += 1

    def run_stages(slot):
        for _ in range(per_slot[slot]):
            stages.pop(0)()

    y_rows = []
    for c in range(nsub):
        r0 = c * WINDOW
        keys = slice(r0, r0 + 2 * WINDOW)
        k_nat, k_swp = ka_scr[keys, :], kb_scr[keys, :]
        vt = vt_scr[:, keys]
        scores = []
        for h in range(N_KV_HEADS):
            p0 = q[r0:r0 + WINDOW, (2 * h) * LANES:(2 * h + 1) * LANES]
            p1 = q[r0:r0 + WINDOW, (2 * h + 1) * LANES:(2 * h + 2) * LANES]
            for half in range(2):
                keep = lo if half == 0 else jnp.logical_not(lo)
                qs = jnp.concatenate([jnp.where(keep, p0, 0.0), jnp.where(keep, p1, 0.0)],
                                     axis=0).astype(BF16)
                kk = k_nat if h == half else k_swp
                tbl = h * 2 + half
                bias = bias_scr[flag * 4 + tbl] if c == 0 else bias_scr[tbl]
                scores.append(_dot_nt(kk, qs) + bias)
        run_stages(2 * c)
        pairs = []
        for h in range(N_KV_HEADS):
            outs = []
            for half in range(2):
                s = scores[h * 2 + half]
                sink = jnp.where(scol, sinks_ref[layer, h * Q_GROUP + half],
                                 sinks_ref[layer, h * Q_GROUP + 2 + half]) * LOG2E
                mx = jnp.maximum(jnp.max(s, axis=0, keepdims=True), sink)
                e = jnp.exp2(s - mx)
                den = jnp.sum(e, axis=0, keepdims=True) + jnp.exp2(sink - mx)
                outs.append(_dot(vt[h * HEAD_DIM:(h + 1) * HEAD_DIM, :], e.astype(BF16)) * (1.0 / den))
            for p in range(2):
                cols = slice(p * WINDOW, (p + 1) * WINDOW)
                pairs.append(jnp.transpose(jnp.concatenate([outs[0][:, cols], outs[1][:, cols]], axis=0)))
        y_rows.append(jnp.concatenate(pairs, axis=1))
        run_stages(2 * c + 1)
    y_a = y_rows[0] if nsub == 1 else jnp.concatenate(y_rows, axis=0)

    ka_scr[0:WINDOW, :] = ka_scr[rows:rows + WINDOW, :]
    kb_scr[0:WINDOW, :] = kb_scr[rows:rows + WINDOW, :]
    vt_scr[:, 0:WINDOW] = vt_scr[:, rows:rows + WINDOW]

    gated_a = done["g0"] * _branch_a(y_a, z_a, woa_ref)
    y_ref[...] = _merge(x, gated_a, done["gated_b"], done["gated_c"], wo_ref)
    conv_in = done["b"][1]
    last = conv_in[rows - SUBLANES:rows, :]
    carry_scr[...] = last
    pc_ref[0] = last


def _stage_bf16(streams, layer):
    def copy(s, k):
        jobs, stage, sem = streams[s]
        src, _, r0, _ = jobs[k]
        slot = k % stage.shape[0]
        return pltpu.make_async_copy(src.at[layer, pl.ds(r0, stage.shape[1]), :], stage.at[slot], sem.at[slot])

    for s, (jobs, stage, _) in enumerate(streams):
        for k in range(min(stage.shape[0] - 1, len(jobs))):
            copy(s, k).start()
    for k in range(max(len(jobs) for jobs, _, _ in streams)):
        for s, (jobs, stage, _) in enumerate(streams):
            if k >= len(jobs):
                continue
            nslots, chunk_rows = stage.shape[0], stage.shape[1]
            if k + nslots - 1 < len(jobs):
                copy(s, k + nslots - 1).start()
            copy(s, k).wait()
            _, dst, r0, scale = jobs[k]
            dst[r0:r0 + chunk_rows, :] = (stage[k % nslots] * scale).astype(BF16)


def _sample_kernel(sinks_ref, x_ref, ck_ref, cv_ref, st_ref, norm_w_ref, b_gate_ref, qw_ref,
                   kw_ref, conv_w_ref, vnw_ref, mix_ref, spb_ref,
                   w_in_hbm, woa_hbm, wob_hbm, woc_hbm, wo_hbm,
                   y_ref, sk_ref, sv_ref, ci_ref, scv_ref,
                   w_in_out, woa_out, wob_out, woc_out, wo_out,
                   x_scr, bold_scr, bnew_scr, q_scr, en_scr, oo_scr, inv_scr,
                   w_in_ref, woa_ref, wob_ref, woc_ref, wo_ref, stage_in, stage_sq,
                   sem_in, sem_sq, sem_out, st_scr, *, dec_seq):
    layer = pl.program_id(0)
    j = pl.program_id(1)
    rows = SAMPLE_ROWS
    nseq = rows // dec_seq
    stack = N_HEADS * dec_seq
    w = WINDOW

    @pl.when((layer == 0) & (j == 0))
    def _():
        st_scr[...] = jnp.zeros(st_scr.shape, F32)
        r = lax.broadcasted_iota(jnp.int32, (stack, LANES), 0)
        col = lax.broadcasted_iota(jnp.int32, (stack, LANES), 1)
        tok = lax.rem(r, dec_seq)
        head = r // dec_seq
        slope = jnp.zeros((stack, LANES), F32)
        for n in range(N_HEADS):
            slope = jnp.where(head == n, _slope(n), slope)
        dist = tok + w - col
        bold_scr[...] = jnp.where(dist < WINDOW, -(slope * dist.astype(F32)), NEG_INF)
        kseq = col // dec_seq
        dist = tok - lax.rem(col, dec_seq)
        pen = -(slope * dist.astype(F32))
        for b in range(nseq):
            bnew_scr[b] = jnp.where((kseq == b) & (dist >= 0), pen, NEG_INF)

    r0 = pl.multiple_of(j * rows, rows)

    resident = ((w_in_hbm, w_in_ref, w_in_out), (woa_hbm, woa_ref, woa_out), (wob_hbm, wob_ref, wob_out),
                (woc_hbm, woc_ref, woc_out), (wo_hbm, wo_ref, wo_out))

    def publish(n, lyr):
        return pltpu.make_async_copy(resident[n][1], resident[n][2].at[lyr], sem_out.at[n])

    @pl.when(j == 0)
    def _():
        @pl.when(layer > 0)
        def _():
            for n in range(len(resident)):
                publish(n, layer - 1).wait()

        halved = _halved_columns()
        wide = [(w_in_hbm, w_in_ref, r, halved) for r in range(0, w_in_ref.shape[0], stage_in.shape[1])]
        square = [(src, dst, r, 0.5 if dst is wo_ref else 1.0) for src, dst, _ in resident[1:]
                  for r in range(0, dst.shape[0], stage_sq.shape[1])]
        _stage_bf16([(wide, stage_in, sem_in), (square, stage_sq, sem_sq)], layer)
        for n in range(len(resident)):
            publish(n, layer).start()

    @pl.when(layer == 0)
    def _():
        x_scr[pl.ds(r0, rows), :] = x_ref[...]

    x = x_scr[pl.ds(r0, rows), :]
    xn = _rms(x, norm_w_ref[...]).astype(BF16)
    def proj(c0, c1):
        return _dot(xn, w_in_ref[:, c0:c1])

    def gate(idx):
        g = proj(COL_G + idx * D_MODEL, COL_G + (idx + 1) * D_MODEL)
        return _twice_sigmoid_of_twice(g + 0.5 * b_gate_ref[:, idx * D_MODEL:(idx + 1) * D_MODEL])

    ha = proj(COL_A, COL_B)
    hb = proj(COL_B, COL_C)
    q = _norm_heads(ha[:, 0:ATTN_WIDTH], qw_ref[...] * (HEAD_DIM ** -0.5))
    k = _pair_rms(ha[:, ATTN_WIDTH:ATTN_WIDTH + KV_WIDTH], kw_ref[...])
    v = ha[:, ATTN_WIDTH + KV_WIDTH:ATTN_WIDTH + 2 * KV_WIDTH]
    z_a = ha[:, ATTN_WIDTH + 2 * KV_WIDTH:COL_B]

    lo = _lo_lanes((rows, LANES))
    for h in range(N_KV_HEADS):
        keep = lo if h == 0 else jnp.logical_not(lo)
        for g in range(Q_GROUP):
            pair = q[:, (h * 2 + g // 2) * LANES:(h * 2 + g // 2 + 1) * LANES]
            if g % 2 != h:
                pair = pltpu.roll(pair, HEAD_DIM, axis=1)
            piece = jnp.where(keep, pair, 0.0).reshape(nseq, dec_seq, LANES)
            s0 = (h * Q_GROUP + g) * dec_seq
            q_scr[:, s0:s0 + dec_seq, :] = piece

    kt_new = jnp.transpose(k)
    vt_new = jnp.transpose(v)
    qall = q_scr[...].reshape(nseq * stack, LANES).astype(BF16)
    s_new_all = _dot(qall, kt_new.astype(BF16)).reshape(nseq, stack, LANES)

    srow = lax.broadcasted_iota(jnp.int32, (stack, 1), 0) // dec_seq
    sink = jnp.zeros((stack, 1), F32)
    for n in range(N_HEADS):
        sink = jnp.where(srow == n, sinks_ref[layer, n], sink)
    bias_old = bold_scr[...]
    lane = lax.broadcasted_iota(jnp.int32, (KV_WIDTH, w), 1)
    s_olds = [_dot(q_scr[b].astype(BF16), ck_ref[b].reshape(KV_WIDTH, w).astype(BF16)) + bias_old
              for b in range(nseq)]
    hc = proj(COL_C, COL_G)
    gates = []
    for b in range(nseq):
        if b % (nseq // 4) == 0 and 0 < b:
            gates.append(gate(len(gates)))
        kt = ck_ref[b].reshape(KV_WIDTH, w)
        vt = cv_ref[b].reshape(KV_WIDTH, w)
        s_old = s_olds[b]
        s_new = s_new_all[b] + bnew_scr[b]
        mx = jnp.maximum(jnp.maximum(jnp.max(s_old, axis=-1, keepdims=True),
                                     jnp.max(s_new, axis=-1, keepdims=True)), sink)
        e_old = jnp.exp(s_old - mx)
        e_new = jnp.exp(s_new - mx)
        den = (jnp.sum(e_old, axis=-1, keepdims=True) + jnp.sum(e_new, axis=-1, keepdims=True)
               + jnp.exp(sink - mx))
        en_scr[b] = e_new
        oo_scr[b] = _dot_nt(e_old.astype(BF16), vt.astype(BF16))
        inv_scr[b] = jnp.broadcast_to(1.0 / den, (stack, LANES))
        shift = w - dec_seq - b * dec_seq
        newk = pltpu.roll(kt_new, shift, axis=1) if shift else kt_new
        newv = pltpu.roll(vt_new, shift, axis=1) if shift else vt_new
        keep_old = lane < w - dec_seq
        sk_ref[b] = jnp.where(keep_old, pltpu.roll(kt, w - dec_seq, axis=1), newk).reshape(
            N_KV_HEADS, HEAD_DIM, w)
        sv_ref[b] = jnp.where(keep_old, pltpu.roll(vt, w - dec_seq, axis=1), newv).reshape(
            N_KV_HEADS, HEAD_DIM, w)

    o_new = _dot(en_scr[...].reshape(nseq * stack, LANES).astype(BF16), v.astype(BF16))
    o = (oo_scr[...] + o_new.reshape(nseq, stack, LANES)) * inv_scr[...]
    pairs = []
    for h in range(N_KV_HEADS):
        heads = []
        for g in range(Q_GROUP):
            s0 = (h * Q_GROUP + g) * dec_seq
            og = o[:, s0:s0 + dec_seq, :].reshape(rows, LANES)
            if g % 2 != h:
                og = pltpu.roll(og, HEAD_DIM, axis=1)
            heads.append(og)
        pairs.append(jnp.where(lo, heads[0], heads[1]))
        pairs.append(jnp.where(lo, heads[2], heads[3]))
    y_a = jnp.concatenate(pairs, axis=1)

    st_scr[:, 0:CONV_K - 1, :] = st_ref[...]
    st = st_scr[...].reshape(rows, CONV_WIDTH)

    def conv_shift(ci):
        t = lax.rem(lax.broadcasted_iota(jnp.int32, ci.shape, 0), dec_seq)
        xm1 = jnp.where(t == 0, pltpu.roll(st, rows - 1, axis=0), pltpu.roll(ci, 1, axis=0))
        xm2 = jnp.where(t < CONV_K - 1, st, pltpu.roll(ci, 2, axis=0))
        return xm1, xm2

    mr = lax.broadcasted_iota(jnp.int32, (CHUNK, CHUNK), 0)
    mc = lax.broadcasted_iota(jnp.int32, (CHUNK, CHUNK), 1)
    mix_mask = (mr // dec_seq == mc // dec_seq) & (mc <= mr)
    b_out, conv_in = _branch_b(hb, conv_shift, conv_w_ref, wob_ref)
    reps = CHUNK // dec_seq

    def mix_rows(g):
        return jnp.broadcast_to(mix_ref[g][None], (reps, dec_seq, CHUNK)).reshape(CHUNK, CHUNK)

    spb = jnp.broadcast_to(spb_ref[...][None], (reps, dec_seq, MLP_WIDTH)).reshape(CHUNK, MLP_WIDTH)
    c_out, v_c = _branch_c(hc, mix_mask, mix_rows, spb, vnw_ref, woc_ref)
    y = _merge(x, gates[0] * _branch_a(y_a, z_a, woa_ref), gates[1] * b_out, gates[2] * c_out, wo_ref)
    x_scr[pl.ds(r0, rows), :] = y

    @pl.when(layer == pl.num_programs(0) - 1)
    def _():
        y_ref[...] = y
    ci_ref[...] = conv_in.reshape(nseq, dec_seq, CONV_WIDTH)[:, dec_seq - (CONV_K - 1):, :]
    scv_ref[...] = v_c

    @pl.when((layer == pl.num_programs(0) - 1) & (j == pl.num_programs(1) - 1))
    def _():
        for n in range(len(resident)):
            publish(n, layer).wait()


SMALL_WEIGHT_NAMES = ("norm_w", "b_gate", "qw", "kw", "conv_w", "vnw", "mix", "spb")
BIG_WEIGHT_NAMES = ("w_in", "woa", "wob", "woc", "wo")
STAGE_SLOTS = 8
STAGE_ROWS_W_IN = 32
STAGE_ROWS_SQUARE = 128
WEIGHT_NAMES = ("norm_w", "w_in", "b_gate", "qw", "kw", "conv_w", "vnw", "mix", "spb",
                "woa", "wob", "woc", "wo")


def _layer_spec(arr, layer_of):
    nd = arr.ndim - 1
    return pl.BlockSpec((None,) + arr.shape[1:], lambda *g, _nd=nd: (layer_of(*g),) + (0,) * _nd,
                        pipeline_mode=pl.Buffered(1))


def _prompt_layer(x, wts, layer, batch):
    n, d = x.shape
    rows = PROMPT_ROWS
    nblk = n // batch // rows
    smem = pl.BlockSpec(memory_space=pltpu.SMEM)
    in_specs = ([smem, pl.BlockSpec((rows, d), lambda i: (i, 0))]
                + [_layer_spec(wts[name], lambda i: layer) for name in WEIGHT_NAMES])
    out_shape = (jax.ShapeDtypeStruct((n, d), F32),
                 jax.ShapeDtypeStruct((batch, WINDOW, KV_WIDTH), F32),
                 jax.ShapeDtypeStruct((batch, WINDOW, KV_WIDTH), F32),
                 jax.ShapeDtypeStruct((batch, SUBLANES, CONV_WIDTH), F32))
    out_specs = (pl.BlockSpec((rows, d), lambda i: (i, 0)),
                 pl.BlockSpec((1, WINDOW, KV_WIDTH), lambda i: (i // nblk, 0, 0)),
                 pl.BlockSpec((1, WINDOW, KV_WIDTH), lambda i: (i // nblk, 0, 0)),
                 pl.BlockSpec((1, SUBLANES, CONV_WIDTH), lambda i: (i // nblk, 0, 0)))
    n_tables = 2 * N_KV_HEADS * 2
    scratch = [pltpu.VMEM((n_tables, 2 * WINDOW, 2 * WINDOW), F32)]
    scratch += [pltpu.VMEM((WINDOW + rows, KV_WIDTH), BF16) for _ in range(2)]
    scratch += [pltpu.VMEM((KV_WIDTH, WINDOW + rows), BF16)]
    scratch += [pltpu.VMEM((SUBLANES, CONV_WIDTH), F32)]
    return pl.pallas_call(
        functools.partial(_prompt_kernel, rows=rows, nblk=nblk, layer=layer),
        grid=(n // rows,),
        in_specs=in_specs, out_specs=out_specs, out_shape=out_shape, scratch_shapes=scratch,
        compiler_params=pltpu.CompilerParams(dimension_semantics=("arbitrary",),
                                             vmem_limit_bytes=VMEM_LIMIT_BYTES),
        name="prompt_layer",
    )(wts["sinks"], x, *[wts[name] for name in WEIGHT_NAMES])


def _sample_layers(x, ck, cv, st, wts, dec_seq):
    n, d = x.shape
    depth = ck.shape[0]
    rows = SAMPLE_ROWS
    nseq = rows // dec_seq
    stack = N_HEADS * dec_seq
    w = ck.shape[-1]
    smem = pl.BlockSpec(memory_space=pltpu.SMEM)
    cache_blk = pl.BlockSpec((None, nseq, N_KV_HEADS, HEAD_DIM, w), lambda l, j: (l, j, 0, 0, 0))
    rows_blk = lambda width: pl.BlockSpec((None, rows, width), lambda l, j: (l, j, 0))
    nblk = n // rows
    x_blk = pl.BlockSpec((rows, d), lambda l, j: (jnp.where(l == 0, j, nblk - 1), 0))
    y_blk = pl.BlockSpec((rows, d), lambda l, j: (jnp.where(l == depth - 1, j, 0), 0))
    hbm = pl.BlockSpec(memory_space=pl.ANY)
    state_blk = pl.BlockSpec((None, nseq, CONV_K - 1, CONV_WIDTH), lambda l, j: (l, j, 0, 0))
    in_specs = ([smem, x_blk, cache_blk, cache_blk, state_blk]
                + [_layer_spec(wts[name], lambda l, j: l) for name in SMALL_WEIGHT_NAMES]
                + [hbm] * len(BIG_WEIGHT_NAMES))
    out_shape = (jax.ShapeDtypeStruct((n, d), F32),
                 jax.ShapeDtypeStruct(ck.shape, F32),
                 jax.ShapeDtypeStruct(cv.shape, F32),
                 jax.ShapeDtypeStruct(st.shape, F32),
                 jax.ShapeDtypeStruct((depth, n, MLP_WIDTH), F32)
                 ) + tuple(jax.ShapeDtypeStruct(wts[name].shape, BF16) for name in BIG_WEIGHT_NAMES)
    out_specs = ((y_blk, cache_blk, cache_blk, state_blk, rows_blk(MLP_WIDTH))
                 + (hbm,) * len(BIG_WEIGHT_NAMES))
    scratch = [pltpu.VMEM((n, d), F32),
               pltpu.VMEM((stack, LANES), F32),
               pltpu.VMEM((nseq, stack, LANES), F32),
               pltpu.VMEM((nseq, stack, LANES), F32),
               pltpu.VMEM((nseq, stack, LANES), F32),
               pltpu.VMEM((nseq, stack, LANES), F32),
               pltpu.VMEM((nseq, stack, LANES), F32)]
    scratch += [pltpu.VMEM(wts[name].shape[1:], BF16) for name in BIG_WEIGHT_NAMES]
    scratch += [pltpu.VMEM((STAGE_SLOTS, STAGE_ROWS_W_IN, IN_COLS), F32),
                pltpu.VMEM((STAGE_SLOTS, STAGE_ROWS_SQUARE, d), F32),
                pltpu.SemaphoreType.DMA((STAGE_SLOTS,)), pltpu.SemaphoreType.DMA((STAGE_SLOTS,)),
                pltpu.SemaphoreType.DMA((len(BIG_WEIGHT_NAMES),)),
                pltpu.VMEM((nseq, dec_seq, CONV_WIDTH), F32)]
    return pl.pallas_call(
        functools.partial(_sample_kernel, dec_seq=dec_seq),
        grid=(depth, n // rows),
        in_specs=in_specs, out_specs=out_specs, out_shape=out_shape, scratch_shapes=scratch,
        compiler_params=pltpu.CompilerParams(dimension_semantics=("arbitrary", "arbitrary"),
                                             vmem_limit_bytes=VMEM_LIMIT_BYTES),
        name="sample_layers",
    )(wts["sinks"], x, ck, cv, st, *[wts[name] for name in SMALL_WEIGHT_NAMES + BIG_WEIGHT_NAMES])


def kernel(x_prompt, x_sample, cache_k, cache_v, state_conv, norm_w, w_in, b_gate, q_norm_w, k_norm_w,
           sinks, conv_w, v_norm_w, w_spatial, b_spatial, w_out_a, w_out_b, w_out_c, w_o):
    batch, seq, d = x_prompt.shape
    dec_batch, dec_seq, _ = x_sample.shape
    depth = w_in.shape[0]
    w_buf = cache_k.shape[2]
    assert d == D_MODEL and seq % PROMPT_ROWS == 0 and (dec_batch * dec_seq) % SAMPLE_ROWS == 0
    assert w_buf == WINDOW and dec_seq == SUBLANES and w_in.shape[2] == IN_COLS

    gw = MLP_WIDTH // N_SPATIAL_GROUPS
    reps = CHUNK // dec_seq
    spb_p = jnp.repeat(jnp.swapaxes(b_spatial, 1, 2), gw, axis=2)
    common = {
        "sinks": sinks,
        "norm_w": norm_w[:, None, :], "w_in": w_in, "b_gate": b_gate[:, None, :],
        "qw": jnp.tile(q_norm_w, (1, LANES // HEAD_DIM))[:, None, :],
        "kw": jnp.tile(k_norm_w, (1, LANES // HEAD_DIM))[:, None, :],
        "conv_w": conv_w, "vnw": v_norm_w[:, None, :],
        "woa": w_out_a, "wob": w_out_b, "woc": w_out_c, "wo": w_o,
    }
    wts_s = dict(common,
                 mix=jnp.tile(w_spatial[:, :, :dec_seq, :dec_seq], (1, 1, 1, reps)),
                 spb=spb_p[:, :dec_seq, :])

    ck = jnp.transpose(cache_k, (0, 1, 3, 4, 2))
    cv = jnp.transpose(cache_v, (0, 1, 3, 4, 2))

    ys, sk, sv, sc, scv, *big_bf16 = _sample_layers(x_sample.reshape(dec_batch * dec_seq, d), ck, cv,
                                                    state_conv, wts_s, dec_seq)
    wts_p = dict(common, mix=w_spatial, spb=spb_p, **dict(zip(BIG_WEIGHT_NAMES, big_bf16)))
    sk = jnp.transpose(sk, (0, 1, 4, 2, 3))
    sv = jnp.transpose(sv, (0, 1, 4, 2, 3))
    scv = scv.reshape(depth, dec_batch, dec_seq, MLP_WIDTH)

    yp = x_prompt.reshape(batch * seq, d)
    pk, pv, pc = [], [], []
    for l in range(depth):
        yp, k_l, v_l, c_l = _prompt_layer(yp, wts_p, l, batch)
        pk.append(k_l.reshape(batch, WINDOW, N_KV_HEADS, HEAD_DIM))
        pv.append(v_l.reshape(batch, WINDOW, N_KV_HEADS, HEAD_DIM))
        pc.append(c_l[:, SUBLANES - (CONV_K - 1):, :])

    return (yp.reshape(batch, seq, d), ys.reshape(dec_batch, dec_seq, d), jnp.stack(pk), jnp.stack(pv),
            jnp.stack(pc), sk, sv, sc, scv)
```

```python
import functools

import jax
import jax.numpy as jnp
from jax import lax
from jax.experimental import pallas as pl
from jax.experimental.pallas import tpu as pltpu

F32 = jnp.float32
BF16 = jnp.bfloat16

D_MODEL = 1024
N_HEADS = 8
N_KV_HEADS = 2
HEAD_DIM = 64
Q_GROUP = N_HEADS // N_KV_HEADS
ATTN_WIDTH = N_HEADS * HEAD_DIM
KV_WIDTH = N_KV_HEADS * HEAD_DIM
WINDOW = 128
CONV_WIDTH = 512
CONV_K = 3
CHUNK = 128
MLP_WIDTH = 512
N_SPATIAL_GROUPS = 4
EPS = 1e-6
NEG_INF = -1e30
LOG2E = 1.4426950408889634

COL_A = 0
COL_B = COL_A + 2 * ATTN_WIDTH + 2 * KV_WIDTH
COL_C = COL_B + 4 * CONV_WIDTH
COL_G = COL_C + 3 * MLP_WIDTH
IN_COLS = COL_G + 3 * D_MODEL

LANES = 128
SUBLANES = 8
PROMPT_ROWS = 512
SAMPLE_ROWS = 128
VMEM_LIMIT_BYTES = 56 * 1024 * 1024


def _dot(a, b):
    return jnp.dot(a, b, preferred_element_type=F32)


def _dot_nt(a, b):
    return lax.dot_general(a, b, (((1,), (1,)), ((), ())), preferred_element_type=F32)


def _rms(x, w):
    ms = jnp.mean(x * x, axis=-1, keepdims=True)
    return (x * lax.rsqrt(ms + EPS)) * w


def _twice_sigmoid_of_twice(h):
    return jnp.tanh(h) + 1.0


def _silu_of_twice(h):
    return h * (jnp.tanh(h) + 1.0)


def _halved_columns():
    col = lax.broadcasted_iota(jnp.int32, (1, IN_COLS), 1)
    halved = col >= COL_G
    for z0, width in ((COL_B - ATTN_WIDTH, ATTN_WIDTH), (COL_C - CONV_WIDTH, CONV_WIDTH),
                      (COL_G - MLP_WIDTH, MLP_WIDTH)):
        halved = halved | ((col >= z0) & (col < z0 + width))
    return jnp.where(halved, 0.5, 1.0)


def _lo_lanes(shape):
    return lax.broadcasted_iota(jnp.int32, shape, len(shape) - 1) < HEAD_DIM


def _pair_rms(x, w):
    lo = _lo_lanes(x.shape)
    sq = x * x
    s_lo = jnp.sum(jnp.where(lo, sq, 0.0), axis=-1, keepdims=True)
    s_hi = jnp.sum(jnp.where(lo, 0.0, sq), axis=-1, keepdims=True)
    r = lax.rsqrt(jnp.where(lo, s_lo, s_hi) + HEAD_DIM * EPS)
    return (x * r) * (w * HEAD_DIM ** 0.5)


def _norm_heads(x, w):
    groups = [_pair_rms(x[:, g * LANES:(g + 1) * LANES], w) for g in range(x.shape[1] // LANES)]
    return groups[0] if len(groups) == 1 else jnp.concatenate(groups, axis=1)


def _slope(head):
    return 2.0 ** (-(head + 1))


def _branch_b(hb, conv_shift, conv_w_ref, wob_ref):
    gate_b = hb[:, 0:CONV_WIDTH]
    gate_c = hb[:, CONV_WIDTH:2 * CONV_WIDTH]
    h_b = hb[:, 2 * CONV_WIDTH:3 * CONV_WIDTH]
    z_b = hb[:, 3 * CONV_WIDTH:4 * CONV_WIDTH]
    conv_in = gate_c * h_b
    xm1, xm2 = conv_shift(conv_in)
    cw = conv_w_ref[...]
    conv_out = cw[0:1] * xm2 + cw[1:2] * xm1 + cw[2:3] * conv_in
    y_b = gate_b * conv_out
    return _dot((_silu_of_twice(z_b) * y_b).astype(BF16), wob_ref[...]), conv_in


def _branch_c(hc, mix_mask, mix_rows, spb, vnw_ref, woc_ref):
    rows = hc.shape[0]
    u = hc[:, 0:MLP_WIDTH]
    v_c = _rms(hc[:, MLP_WIDTH:2 * MLP_WIDTH], vnw_ref[...])
    z_c = hc[:, 2 * MLP_WIDTH:3 * MLP_WIDTH]
    vb = v_c.astype(BF16)
    gw = MLP_WIDTH // N_SPATIAL_GROUPS
    mixes = [jnp.where(mix_mask, mix_rows(g), 0.0).astype(BF16) for g in range(N_SPATIAL_GROUPS)]
    sp_chunks = []
    for c in range(rows // CHUNK):
        r0 = c * CHUNK
        parts = [_dot(mixes[g], vb[r0:r0 + CHUNK, g * gw:(g + 1) * gw])
                 for g in range(N_SPATIAL_GROUPS)]
        sp_chunks.append(jnp.concatenate(parts, axis=1) + spb)
    sp = sp_chunks[0] if len(sp_chunks) == 1 else jnp.concatenate(sp_chunks, axis=0)
    y_c = u * sp
    return _dot((_silu_of_twice(z_c) * y_c).astype(BF16), woc_ref[...]), v_c


def _branch_a(y_a, z_a, woa_ref):
    return _dot((_silu_of_twice(z_a) * y_a).astype(BF16), woa_ref[...])


def _merge(x, gated_a, gated_b, gated_c, wo_ref):
    m = gated_a + gated_b + gated_c
    return x + _dot(m.astype(BF16), wo_ref[...])


def _prompt_kernel(sinks_ref, x_ref, norm_w_ref, b_gate_ref, qw_ref, kw_ref, conv_w_ref,
                   vnw_ref, mix_ref, spb_ref, w_in_hbm, woa_hbm, wob_hbm, woc_hbm, wo_hbm,
                   y_ref, pk_ref, pv_ref, pc_ref,
                   bias_scr, ka_scr, kb_scr, vt_scr, carry_scr,
                   w_in_ref, woa_ref, wob_ref, woc_ref, wo_ref, sem, *, rows, nblk, layer):
    i = pl.program_id(0)
    copies = {}
    for name, c0, c1 in (("A", COL_A, COL_B), ("B", COL_B, COL_C), ("C", COL_C, COL_G),
                         ("G0", COL_G, COL_G + D_MODEL), ("G1", COL_G + D_MODEL, COL_G + 2 * D_MODEL),
                         ("G2", COL_G + 2 * D_MODEL, IN_COLS)):
        copies[name] = pltpu.make_async_copy(w_in_hbm.at[layer, :, c0:c1], w_in_ref.at[:, c0:c1],
                                             sem.at[len(copies)])
    for name, src, dst in (("wob", wob_hbm, wob_ref), ("woc", woc_hbm, woc_ref),
                           ("woa", woa_hbm, woa_ref), ("wo", wo_hbm, wo_ref)):
        copies[name] = pltpu.make_async_copy(src.at[layer], dst, sem.at[len(copies)])

    def step(wait):
        _prompt_step(sinks_ref, x_ref, norm_w_ref, w_in_ref, b_gate_ref, qw_ref, kw_ref, conv_w_ref,
                     vnw_ref, mix_ref, spb_ref, woa_ref, wob_ref, woc_ref, wo_ref,
                     y_ref, pk_ref, pv_ref, pc_ref, bias_scr, ka_scr, kb_scr, vt_scr, carry_scr,
                     wait, rows=rows, nblk=nblk, layer=layer)

    @pl.when(i == 0)
    def _():
        for copy in copies.values():
            copy.start()
        step(lambda name: copies[name].wait())

    @pl.when(i > 0)
    def _():
        step(lambda name: None)


def _prompt_step(sinks_ref, x_ref, norm_w_ref, w_in_ref, b_gate_ref, qw_ref, kw_ref, conv_w_ref,
                 vnw_ref, mix_ref, spb_ref, woa_ref, wob_ref, woc_ref, wo_ref,
                 y_ref, pk_ref, pv_ref, pc_ref,
                 bias_scr, ka_scr, kb_scr, vt_scr, carry_scr, wait, *, rows, nblk, layer):
    i = pl.program_id(0)
    first = lax.rem(i, nblk) == 0
    nsub = rows // WINDOW
    stack = 2 * WINDOW

    @pl.when(i == 0)
    def _():
        key = lax.broadcasted_iota(jnp.int32, (2 * WINDOW, stack), 0)
        c = lax.broadcasted_iota(jnp.int32, (2 * WINDOW, stack), 1)
        left = c < WINDOW
        dist = jnp.where(left, c, c - WINDOW) + WINDOW - key
        band = (dist >= 0) & (dist < WINDOW)
        distf = dist.astype(F32)
        for flag in range(2):
            valid = band & (key >= WINDOW) if flag else band
            for h in range(N_KV_HEADS):
                for half in range(2):
                    slope = jnp.where(left, _slope(h * Q_GROUP + half), _slope(h * Q_GROUP + 2 + half))
                    bias_scr[flag * 4 + h * 2 + half] = jnp.where(valid, -(slope * distf) * LOG2E, NEG_INF)

    @pl.when(first)
    def _():
        zeros = jnp.zeros((WINDOW, KV_WIDTH), BF16)
        ka_scr[0:WINDOW, :] = zeros
        kb_scr[0:WINDOW, :] = zeros
        vt_scr[:, 0:WINDOW] = zeros
        carry_scr[...] = jnp.zeros(carry_scr.shape, F32)

    x = x_ref[...]
    xn = _rms(x, norm_w_ref[...]).astype(BF16)
    def proj(c0, c1):
        return _dot(xn, w_in_ref[:, c0:c1])

    def gate(idx):
        wait("G%d" % idx)
        g = proj(COL_G + idx * D_MODEL, COL_G + (idx + 1) * D_MODEL)
        return _twice_sigmoid_of_twice(g + 0.5 * b_gate_ref[:, idx * D_MODEL:(idx + 1) * D_MODEL])

    wait("A")
    ha = proj(COL_A, COL_B)
    wait("B")
    hb = proj(COL_B, COL_C)
    q = _norm_heads(ha[:, 0:ATTN_WIDTH], qw_ref[...] * (HEAD_DIM ** -0.5 * LOG2E))
    k = _pair_rms(ha[:, ATTN_WIDTH:ATTN_WIDTH + KV_WIDTH], kw_ref[...])
    v = ha[:, ATTN_WIDTH + KV_WIDTH:ATTN_WIDTH + 2 * KV_WIDTH]
    z_a = ha[:, ATTN_WIDTH + 2 * KV_WIDTH:COL_B]

    pk_ref[0] = k[rows - WINDOW:rows, :]
    pv_ref[0] = v[rows - WINDOW:rows, :]
    ka_scr[WINDOW:WINDOW + rows, :] = k.astype(BF16)
    kb_scr[WINDOW:WINDOW + rows, :] = pltpu.roll(k, HEAD_DIM, axis=1).astype(BF16)
    for c in range(nsub):
        vt_scr[:, (c + 1) * WINDOW:(c + 2) * WINDOW] = jnp.transpose(
            v[c * WINDOW:(c + 1) * WINDOW, :]).astype(BF16)

    lo = _lo_lanes((WINDOW, LANES))
    scol = lax.broadcasted_iota(jnp.int32, (1, stack), 1) < WINDOW
    flag = first.astype(jnp.int32)
    carry = carry_scr[...]
    prev1 = carry[SUBLANES - 1:SUBLANES, :]
    prev2 = carry[SUBLANES - 2:SUBLANES - 1, :]

    def conv_shift(ci):
        rid = lax.broadcasted_iota(jnp.int32, ci.shape, 0)
        xm1 = jnp.where(rid == 0, prev1, pltpu.roll(ci, 1, axis=0))
        xm2 = jnp.where(rid == 0, prev2, jnp.where(rid == 1, prev1, pltpu.roll(ci, 2, axis=0)))
        return xm1, xm2

    mr = lax.broadcasted_iota(jnp.int32, (CHUNK, CHUNK), 0)
    mc = lax.broadcasted_iota(jnp.int32, (CHUNK, CHUNK), 1)
    done = {}
    stages = [
        lambda: (wait("C"), done.update(hc=proj(COL_C, COL_G))),
        lambda: (wait("wob"), done.update(b=_branch_b(hb, conv_shift, conv_w_ref, wob_ref))),
        lambda: done.update(gated_b=gate(1) * done["b"][0]),
        lambda: (wait("woc"), done.update(c=_branch_c(done["hc"], mc <= mr, lambda g: mix_ref[g],
                                                      spb_ref[...], vnw_ref, woc_ref))),
        lambda: done.update(gated_c=gate(2) * done["c"][0]),
        lambda: done.update(g0=gate(0)),
    ]
    slots = 2 * nsub
    order = list(range(0, slots, 2)) + list(range(1, slots, 2))
    per_slot = [0] * slots
    for n in range(len(stages)):
        per_slot[order[n % slots]] += 1

    def run_stages(slot):
        for _ in range(per_slot[slot]):
            stages.pop(0)()

    y_rows = []
    for c in range(nsub):
        r0 = c * WINDOW
        keys = slice(r0, r0 + 2 * WINDOW)
        k_nat, k_swp = ka_scr[keys, :], kb_scr[keys, :]
        vt = vt_scr[:, keys]
        scores = []
        for h in range(N_KV_HEADS):
            p0 = q[r0:r0 + WINDOW, (2 * h) * LANES:(2 * h + 1) * LANES]
            p1 = q[r0:r0 + WINDOW, (2 * h + 1) * LANES:(2 * h + 2) * LANES]
            for half in range(2):
                keep = lo if half == 0 else jnp.logical_not(lo)
                qs = jnp.concatenate([jnp.where(keep, p0, 0.0), jnp.where(keep, p1, 0.0)],
                                     axis=0).astype(BF16)
                kk = k_nat if h == half else k_swp
                tbl = h * 2 + half
                bias = bias_scr[flag * 4 + tbl] if c == 0 else bias_scr[tbl]
                scores.append(_dot_nt(kk, qs) + bias)
        run_stages(2 * c)
        pairs = []
        for h in range(N_KV_HEADS):
            outs = []
            for half in range(2):
                s = scores[h * 2 + half]
                sink = jnp.where(scol, sinks_ref[layer, h * Q_GROUP + half],
                                 sinks_ref[layer, h * Q_GROUP + 2 + half]) * LOG2E
                mx = jnp.maximum(jnp.max(s, axis=0, keepdims=True), sink)
                e = jnp.exp2(s - mx)
                den = jnp.sum(e, axis=0, keepdims=True) + jnp.exp2(sink - mx)
                outs.append(_dot(vt[h * HEAD_DIM:(h + 1) * HEAD_DIM, :], e.astype(BF16)) * (1.0 / den))
            for p in range(2):
                cols = slice(p * WINDOW, (p + 1) * WINDOW)
                pairs.append(jnp.transpose(jnp.concatenate([outs[0][:, cols], outs[1][:, cols]], axis=0)))
        y_rows.append(jnp.concatenate(pairs, axis=1))
        run_stages(2 * c + 1)
    y_a = y_rows[0] if nsub == 1 else jnp.concatenate(y_rows, axis=0)

    ka_scr[0:WINDOW, :] = ka_scr[rows:rows + WINDOW, :]
    kb_scr[0:WINDOW, :] = kb_scr[rows:rows + WINDOW, :]
    vt_scr[:, 0:WINDOW] = vt_scr[:, rows:rows + WINDOW]

    wait("woa")
    gated_a = done["g0"] * _branch_a(y_a, z_a, woa_ref)
    wait("wo")
    y_ref[...] = _merge(x, gated_a, done["gated_b"], done["gated_c"], wo_ref)
    conv_in = done["b"][1]
    last = conv_in[rows - SUBLANES:rows, :]
    carry_scr[...] = last
    pc_ref[0] = last


def _stage_bf16(streams, layer):
    def copy(s, k):
        jobs, stage, sem = streams[s]
        src, _, r0, _ = jobs[k]
        slot = k % stage.shape[0]
        return pltpu.make_async_copy(src.at[layer, pl.ds(r0, stage.shape[1]), :], stage.at[slot], sem.at[slot])

    for s, (jobs, stage, _) in enumerate(streams):
        for k in range(min(stage.shape[0] - 1, len(jobs))):
            copy(s, k).start()
    for k in range(max(len(jobs) for jobs, _, _ in streams)):
        for s, (jobs, stage, _) in enumerate(streams):
            if k >= len(jobs):
                continue
            nslots, chunk_rows = stage.shape[0], stage.shape[1]
            if k + nslots - 1 < len(jobs):
                copy(s, k + nslots - 1).start()
            copy(s, k).wait()
            _, dst, r0, scale = jobs[k]
            dst[r0:r0 + chunk_rows, :] = (stage[k % nslots] * scale).astype(BF16)


def _sample_kernel(sinks_ref, x_ref, ck_ref, cv_ref, st_ref, norm_w_ref, b_gate_ref, qw_ref,
                   kw_ref, conv_w_ref, vnw_ref, mix_ref, spb_ref,
                   w_in_hbm, woa_hbm, wob_hbm, woc_hbm, wo_hbm,
                   y_ref, sk_ref, sv_ref, ci_ref, scv_ref,
                   w_in_out, woa_out, wob_out, woc_out, wo_out,
                   x_scr, bold_scr, bnew_scr, q_scr, en_scr, oo_scr, inv_scr,
                   w_in_ref, woa_ref, wob_ref, woc_ref, wo_ref, stage_in, stage_sq,
                   sem_in, sem_sq, sem_out, st_scr, *, dec_seq):
    layer = pl.program_id(0)
    j = pl.program_id(1)
    rows = SAMPLE_ROWS
    nseq = rows // dec_seq
    stack = N_HEADS * dec_seq
    w = WINDOW

    @pl.when((layer == 0) & (j == 0))
    def _():
        st_scr[...] = jnp.zeros(st_scr.shape, F32)
        r = lax.broadcasted_iota(jnp.int32, (stack, LANES), 0)
        col = lax.broadcasted_iota(jnp.int32, (stack, LANES), 1)
        tok = lax.rem(r, dec_seq)
        head = r // dec_seq
        slope = jnp.zeros((stack, LANES), F32)
        for n in range(N_HEADS):
            slope = jnp.where(head == n, _slope(n), slope)
        dist = tok + w - col
        bold_scr[...] = jnp.where(dist < WINDOW, -(slope * dist.astype(F32)), NEG_INF)
        kseq = col // dec_seq
        dist = tok - lax.rem(col, dec_seq)
        pen = -(slope * dist.astype(F32))
        for b in range(nseq):
            bnew_scr[b] = jnp.where((kseq == b) & (dist >= 0), pen, NEG_INF)

    r0 = pl.multiple_of(j * rows, rows)

    resident = ((w_in_hbm, w_in_ref, w_in_out), (woa_hbm, woa_ref, woa_out), (wob_hbm, wob_ref, wob_out),
                (woc_hbm, woc_ref, woc_out), (wo_hbm, wo_ref, wo_out))

    def publish(n, lyr):
        return pltpu.make_async_copy(resident[n][1], resident[n][2].at[lyr], sem_out.at[n])

    @pl.when(j == 0)
    def _():
        @pl.when(layer > 0)
        def _():
            for n in range(len(resident)):
                publish(n, layer - 1).wait()

        halved = _halved_columns()
        wide = [(w_in_hbm, w_in_ref, r, halved) for r in range(0, w_in_ref.shape[0], stage_in.shape[1])]
        square = [(src, dst, r, 0.5 if dst is wo_ref else 1.0) for src, dst, _ in resident[1:]
                  for r in range(0, dst.shape[0], stage_sq.shape[1])]
        _stage_bf16([(wide, stage_in, sem_in), (square, stage_sq, sem_sq)], layer)
        for n in range(len(resident)):
            publish(n, layer).start()

    @pl.when(layer == 0)
    def _():
        x_scr[pl.ds(r0, rows), :] = x_ref[...]

    x = x_scr[pl.ds(r0, rows), :]
    xn = _rms(x, norm_w_ref[...]).astype(BF16)
    def proj(c0, c1):
        return _dot(xn, w_in_ref[:, c0:c1])

    def gate(idx):
        g = proj(COL_G + idx * D_MODEL, COL_G + (idx + 1) * D_MODEL)
        return _twice_sigmoid_of_twice(g + 0.5 * b_gate_ref[:, idx * D_MODEL:(idx + 1) * D_MODEL])

    ha = proj(COL_A, COL_B)
    hb = proj(COL_B, COL_C)
    q = _norm_heads(ha[:, 0:ATTN_WIDTH], qw_ref[...] * (HEAD_DIM ** -0.5))
    k = _pair_rms(ha[:, ATTN_WIDTH:ATTN_WIDTH + KV_WIDTH], kw_ref[...])
    v = ha[:, ATTN_WIDTH + KV_WIDTH:ATTN_WIDTH + 2 * KV_WIDTH]
    z_a = ha[:, ATTN_WIDTH + 2 * KV_WIDTH:COL_B]

    lo = _lo_lanes((rows, LANES))
    for h in range(N_KV_HEADS):
        keep = lo if h == 0 else jnp.logical_not(lo)
        for g in range(Q_GROUP):
            pair = q[:, (h * 2 + g // 2) * LANES:(h * 2 + g // 2 + 1) * LANES]
            if g % 2 != h:
                pair = pltpu.roll(pair, HEAD_DIM, axis=1)
            piece = jnp.where(keep, pair, 0.0).reshape(nseq, dec_seq, LANES)
            s0 = (h * Q_GROUP + g) * dec_seq
            q_scr[:, s0:s0 + dec_seq, :] = piece

    kt_new = jnp.transpose(k)
    vt_new = jnp.transpose(v)
    qall = q_scr[...].reshape(nseq * stack, LANES).astype(BF16)
    s_new_all = _dot(qall, kt_new.astype(BF16)).reshape(nseq, stack, LANES)

    srow = lax.broadcasted_iota(jnp.int32, (stack, 1), 0) // dec_seq
    sink = jnp.zeros((stack, 1), F32)
    for n in range(N_HEADS):
        sink = jnp.where(srow == n, sinks_ref[layer, n], sink)
    bias_old = bold_scr[...]
    lane = lax.broadcasted_iota(jnp.int32, (KV_WIDTH, w), 1)
    s_olds = [_dot(q_scr[b].astype(BF16), ck_ref[b].reshape(KV_WIDTH, w).astype(BF16)) + bias_old
              for b in range(nseq)]
    hc = proj(COL_C, COL_G)
    gates = []
    for b in range(nseq):
        if b % (nseq // 4) == 0 and 0 < b:
            gates.append(gate(len(gates)))
        kt = ck_ref[b].reshape(KV_WIDTH, w)
        vt = cv_ref[b].reshape(KV_WIDTH, w)
        s_old = s_olds[b]
        s_new = s_new_all[b] + bnew_scr[b]
        mx = jnp.maximum(jnp.maximum(jnp.max(s_old, axis=-1, keepdims=True),
                                     jnp.max(s_new, axis=-1, keepdims=True)), sink)
        e_old = jnp.exp(s_old - mx)
        e_new = jnp.exp(s_new - mx)
        den = (jnp.sum(e_old, axis=-1, keepdims=True) + jnp.sum(e_new, axis=-1, keepdims=True)
               + jnp.exp(sink - mx))
        en_scr[b] = e_new
        oo_scr[b] = _dot_nt(e_old.astype(BF16), vt.astype(BF16))
        inv_scr[b] = jnp.broadcast_to(1.0 / den, (stack, LANES))
        shift = w - dec_seq - b * dec_seq
        newk = pltpu.roll(kt_new, shift, axis=1) if shift else kt_new
        newv = pltpu.roll(vt_new, shift, axis=1) if shift else vt_new
        keep_old = lane < w - dec_seq
        sk_ref[b] = jnp.where(keep_old, pltpu.roll(kt, w - dec_seq, axis=1), newk).reshape(
            N_KV_HEADS, HEAD_DIM, w)
        sv_ref[b] = jnp.where(keep_old, pltpu.roll(vt, w - dec_seq, axis=1), newv).reshape(
            N_KV_HEADS, HEAD_DIM, w)

    o_new = _dot(en_scr[...].reshape(nseq * stack, LANES).astype(BF16), v.astype(BF16))
    o = (oo_scr[...] + o_new.reshape(nseq, stack, LANES)) * inv_scr[...]
    pairs = []
    for h in range(N_KV_HEADS):
        heads = []
        for g in range(Q_GROUP):
            s0 = (h * Q_GROUP + g) * dec_seq
            og = o[:, s0:s0 + dec_seq, :].reshape(rows, LANES)
            if g % 2 != h:
                og = pltpu.roll(og, HEAD_DIM, axis=1)
            heads.append(og)
        pairs.append(jnp.where(lo, heads[0], heads[1]))
        pairs.append(jnp.where(lo, heads[2], heads[3]))
    y_a = jnp.concatenate(pairs, axis=1)

    st_scr[:, 0:CONV_K - 1, :] = st_ref[...]
    st = st_scr[...].reshape(rows, CONV_WIDTH)

    def conv_shift(ci):
        t = lax.rem(lax.broadcasted_iota(jnp.int32, ci.shape, 0), dec_seq)
        xm1 = jnp.where(t == 0, pltpu.roll(st, rows - 1, axis=0), pltpu.roll(ci, 1, axis=0))
        xm2 = jnp.where(t < CONV_K - 1, st, pltpu.roll(ci, 2, axis=0))
        return xm1, xm2

    mr = lax.broadcasted_iota(jnp.int32, (CHUNK, CHUNK), 0)
    mc = lax.broadcasted_iota(jnp.int32, (CHUNK, CHUNK), 1)
    mix_mask = (mr // dec_seq == mc // dec_seq) & (mc <= mr)
    b_out, conv_in = _branch_b(hb, conv_shift, conv_w_ref, wob_ref)
    reps = CHUNK // dec_seq

    def mix_rows(g):
        return jnp.broadcast_to(mix_ref[g][None], (reps, dec_seq, CHUNK)).reshape(CHUNK, CHUNK)

    spb = jnp.broadcast_to(spb_ref[...][None], (reps, dec_seq, MLP_WIDTH)).reshape(CHUNK, MLP_WIDTH)
    c_out, v_c = _branch_c(hc, mix_mask, mix_rows, spb, vnw_ref, woc_ref)
    y = _merge(x, gates[0] * _branch_a(y_a, z_a, woa_ref), gates[1] * b_out, gates[2] * c_out, wo_ref)
    x_scr[pl.ds(r0, rows), :] = y

    @pl.when(layer == pl.num_programs(0) - 1)
    def _():
        y_ref[...] = y
    ci_ref[...] = conv_in.reshape(nseq, dec_seq, CONV_WIDTH)[:, dec_seq - (CONV_K - 1):, :]
    scv_ref[...] = v_c

    @pl.when((layer == pl.num_programs(0) - 1) & (j == pl.num_programs(1) - 1))
    def _():
        for n in range(len(resident)):
            publish(n, layer).wait()


SMALL_WEIGHT_NAMES = ("norm_w", "b_gate", "qw", "kw", "conv_w", "vnw", "mix", "spb")
BIG_WEIGHT_NAMES = ("w_in", "woa", "wob", "woc", "wo")
STAGE_SLOTS = 8
STAGE_ROWS_W_IN = 32
STAGE_ROWS_SQUARE = 128
N_WEIGHT_COPIES = 10
WEIGHT_NAMES = ("norm_w", "w_in", "b_gate", "qw", "kw", "conv_w", "vnw", "mix", "spb",
                "woa", "wob", "woc", "wo")


def _layer_spec(arr, layer_of):
    nd = arr.ndim - 1
    return pl.BlockSpec((None,) + arr.shape[1:], lambda *g, _nd=nd: (layer_of(*g),) + (0,) * _nd,
                        pipeline_mode=pl.Buffered(1))


def _prompt_layer(x, wts, layer, batch):
    n, d = x.shape
    rows = PROMPT_ROWS
    nblk = n // batch // rows
    smem = pl.BlockSpec(memory_space=pltpu.SMEM)
    in_specs = ([smem, pl.BlockSpec((rows, d), lambda i: (i, 0))]
                + [_layer_spec(wts[name], lambda i: layer) for name in SMALL_WEIGHT_NAMES]
                + [pl.BlockSpec(memory_space=pl.ANY)] * len(BIG_WEIGHT_NAMES))
    out_shape = (jax.ShapeDtypeStruct((n, d), F32),
                 jax.ShapeDtypeStruct((batch, WINDOW, KV_WIDTH), F32),
                 jax.ShapeDtypeStruct((batch, WINDOW, KV_WIDTH), F32),
                 jax.ShapeDtypeStruct((batch, SUBLANES, CONV_WIDTH), F32))
    out_specs = (pl.BlockSpec((rows, d), lambda i: (i, 0)),
                 pl.BlockSpec((1, WINDOW, KV_WIDTH), lambda i: (i // nblk, 0, 0)),
                 pl.BlockSpec((1, WINDOW, KV_WIDTH), lambda i: (i // nblk, 0, 0)),
                 pl.BlockSpec((1, SUBLANES, CONV_WIDTH), lambda i: (i // nblk, 0, 0)))
    n_tables = 2 * N_KV_HEADS * 2
    scratch = [pltpu.VMEM((n_tables, 2 * WINDOW, 2 * WINDOW), F32)]
    scratch += [pltpu.VMEM((WINDOW + rows, KV_WIDTH), BF16) for _ in range(2)]
    scratch += [pltpu.VMEM((KV_WIDTH, WINDOW + rows), BF16)]
    scratch += [pltpu.VMEM((SUBLANES, CONV_WIDTH), F32)]
    scratch += [pltpu.VMEM(wts[name].shape[1:], BF16) for name in BIG_WEIGHT_NAMES]
    scratch += [pltpu.SemaphoreType.DMA((N_WEIGHT_COPIES,))]
    return pl.pallas_call(
        functools.partial(_prompt_kernel, rows=rows, nblk=nblk, layer=layer),
        grid=(n // rows,),
        in_specs=in_specs, out_specs=out_specs, out_shape=out_shape, scratch_shapes=scratch,
        compiler_params=pltpu.CompilerParams(dimension_semantics=("arbitrary",),
                                             vmem_limit_bytes=VMEM_LIMIT_BYTES),
        name="prompt_layer",
    )(wts["sinks"], x, *[wts[name] for name in SMALL_WEIGHT_NAMES + BIG_WEIGHT_NAMES])


def _sample_layers(x, ck, cv, st, wts, dec_seq):
    n, d = x.shape
    depth = ck.shape[0]
    rows = SAMPLE_ROWS
    nseq = rows // dec_seq
    stack = N_HEADS * dec_seq
    w = ck.shape[-1]
    smem = pl.BlockSpec(memory_space=pltpu.SMEM)
    cache_blk = pl.BlockSpec((None, nseq, N_KV_HEADS, HEAD_DIM, w), lambda l, j: (l, j, 0, 0, 0))
    rows_blk = lambda width: pl.BlockSpec((None, rows, width), lambda l, j: (l, j, 0))
    nblk = n // rows
    x_blk = pl.BlockSpec((rows, d), lambda l, j: (jnp.where(l == 0, j, nblk - 1), 0))
    y_blk = pl.BlockSpec((rows, d), lambda l, j: (jnp.where(l == depth - 1, j, 0), 0))
    hbm = pl.BlockSpec(memory_space=pl.ANY)
    state_blk = pl.BlockSpec((None, nseq, CONV_K - 1, CONV_WIDTH), lambda l, j: (l, j, 0, 0))
    in_specs = ([smem, x_blk, cache_blk, cache_blk, state_blk]
                + [_layer_spec(wts[name], lambda l, j: l) for name in SMALL_WEIGHT_NAMES]
                + [hbm] * len(BIG_WEIGHT_NAMES))
    out_shape = (jax.ShapeDtypeStruct((n, d), F32),
                 jax.ShapeDtypeStruct(ck.shape, F32),
                 jax.ShapeDtypeStruct(cv.shape, F32),
                 jax.ShapeDtypeStruct(st.shape, F32),
                 jax.ShapeDtypeStruct((depth, n, MLP_WIDTH), F32)
                 ) + tuple(jax.ShapeDtypeStruct(wts[name].shape, BF16) for name in BIG_WEIGHT_NAMES)
    out_specs = ((y_blk, cache_blk, cache_blk, state_blk, rows_blk(MLP_WIDTH))
                 + (hbm,) * len(BIG_WEIGHT_NAMES))
    scratch = [pltpu.VMEM((n, d), F32),
               pltpu.VMEM((stack, LANES), F32),
               pltpu.VMEM((nseq, stack, LANES), F32),
               pltpu.VMEM((nseq, stack, LANES), F32),
               pltpu.VMEM((nseq, stack, LANES), F32),
               pltpu.VMEM((nseq, stack, LANES), F32),
               pltpu.VMEM((nseq, stack, LANES), F32)]
    scratch += [pltpu.VMEM(wts[name].shape[1:], BF16) for name in BIG_WEIGHT_NAMES]
    scratch += [pltpu.VMEM((STAGE_SLOTS, STAGE_ROWS_W_IN, IN_COLS), F32),
                pltpu.VMEM((STAGE_SLOTS, STAGE_ROWS_SQUARE, d), F32),
                pltpu.SemaphoreType.DMA((STAGE_SLOTS,)), pltpu.SemaphoreType.DMA((STAGE_SLOTS,)),
                pltpu.SemaphoreType.DMA((len(BIG_WEIGHT_NAMES),)),
                pltpu.VMEM((nseq, dec_seq, CONV_WIDTH), F32)]
    return pl.pallas_call(
        functools.partial(_sample_kernel, dec_seq=dec_seq),
        grid=(depth, n // rows),
        in_specs=in_specs, out_specs=out_specs, out_shape=out_shape, scratch_shapes=scratch,
        compiler_params=pltpu.CompilerParams(dimension_semantics=("arbitrary", "arbitrary"),
                                             vmem_limit_bytes=VMEM_LIMIT_BYTES),
        name="sample_layers",
    )(wts["sinks"], x, ck, cv, st, *[wts[name] for name in SMALL_WEIGHT_NAMES + BIG_WEIGHT_NAMES])


def kernel(x_prompt, x_sample, cache_k, cache_v, state_conv, norm_w, w_in, b_gate, q_norm_w, k_norm_w,
           sinks, conv_w, v_norm_w, w_spatial, b_spatial, w_out_a, w_out_b, w_out_c, w_o):
    batch, seq, d = x_prompt.shape
    dec_batch, dec_seq, _ = x_sample.shape
    depth = w_in.shape[0]
    w_buf = cache_k.shape[2]
    assert d == D_MODEL and seq % PROMPT_ROWS == 0 and (dec_batch * dec_seq) % SAMPLE_ROWS == 0
    assert w_buf == WINDOW and dec_seq == SUBLANES and w_in.shape[2] == IN_COLS

    gw = MLP_WIDTH // N_SPATIAL_GROUPS
    reps = CHUNK // dec_seq
    spb_p = jnp.repeat(jnp.swapaxes(b_spatial, 1, 2), gw, axis=2)
    common = {
        "sinks": sinks,
        "norm_w": norm_w[:, None, :], "w_in": w_in, "b_gate": b_gate[:, None, :],
        "qw": jnp.tile(q_norm_w, (1, LANES // HEAD_DIM))[:, None, :],
        "kw": jnp.tile(k_norm_w, (1, LANES // HEAD_DIM))[:, None, :],
        "conv_w": conv_w, "vnw": v_norm_w[:, None, :],
        "woa": w_out_a, "wob": w_out_b, "woc": w_out_c, "wo": w_o,
    }
    wts_s = dict(common,
                 mix=jnp.tile(w_spatial[:, :, :dec_seq, :dec_seq], (1, 1, 1, reps)),
                 spb=spb_p[:, :dec_seq, :])

    ck = jnp.transpose(cache_k, (0, 1, 3, 4, 2))
    cv = jnp.transpose(cache_v, (0, 1, 3, 4, 2))

    ys, sk, sv, sc, scv, *big_bf16 = _sample_layers(x_sample.reshape(dec_batch * dec_seq, d), ck, cv,
                                                    state_conv, wts_s, dec_seq)
    wts_p = dict(common, mix=w_spatial, spb=spb_p, **dict(zip(BIG_WEIGHT_NAMES, big_bf16)))
    sk = jnp.transpose(sk, (0, 1, 4, 2, 3))
    sv = jnp.transpose(sv, (0, 1, 4, 2, 3))
    scv = scv.reshape(depth, dec_batch, dec_seq, MLP_WIDTH)

    yp = x_prompt.reshape(batch * seq, d)
    pk, pv, pc = [], [], []
    for l in range(depth):
        yp, k_l, v_l, c_l = _prompt_layer(yp, wts_p, l, batch)
        pk.append(k_l.reshape(batch, WINDOW, N_KV_HEADS, HEAD_DIM))
        pv.append(v_l.reshape(batch, WINDOW, N_KV_HEADS, HEAD_DIM))
        pc.append(c_l[:, SUBLANES - (CONV_K - 1):, :])

    return (yp.reshape(batch, seq, d), ys.reshape(dec_batch, dec_seq, d), jnp.stack(pk), jnp.stack(pv),
            jnp.stack(pc), sk, sv, sc, scv)
```

```python
import functools

import jax
import jax.numpy as jnp
from jax import lax
from jax.experimental import pallas as pl
from jax.experimental.pallas import tpu as pltpu

F32 = jnp.float32
BF16 = jnp.bfloat16

D_MODEL = 1024
N_HEADS = 8
N_KV_HEADS = 2
HEAD_DIM = 64
Q_GROUP = N_HEADS // N_KV_HEADS
ATTN_WIDTH = N_HEADS * HEAD_DIM
KV_WIDTH = N_KV_HEADS * HEAD_DIM
WINDOW = 128
CONV_WIDTH = 512
CONV_K = 3
CHUNK = 128
MLP_WIDTH = 512
N_SPATIAL_GROUPS = 4
EPS = 1e-6
NEG_INF = -1e30
LOG2E = 1.4426950408889634

COL_A = 0
COL_B = COL_A + 2 * ATTN_WIDTH + 2 * KV_WIDTH
COL_C = COL_B + 4 * CONV_WIDTH
COL_G = COL_C + 3 * MLP_WIDTH
IN_COLS = COL_G + 3 * D_MODEL

LANES = 128
SUBLANES = 8
PROMPT_ROWS = 512
SAMPLE_ROWS = 128
VMEM_LIMIT_BYTES = 56 * 1024 * 1024


def _dot(a, b):
    return jnp.dot(a, b, preferred_element_type=F32)


def _dot_nt(a, b):
    return lax.dot_general(a, b, (((1,), (1,)), ((), ())), preferred_element_type=F32)


def _rms(x, w):
    ms = jnp.mean(x * x, axis=-1, keepdims=True)
    return (x * lax.rsqrt(ms + EPS)) * w


def _twice_sigmoid_of_twice(h):
    return jnp.tanh(h) + 1.0


def _silu_of_twice(h):
    return h * (jnp.tanh(h) + 1.0)


def _halved_columns():
    col = lax.broadcasted_iota(jnp.int32, (1, IN_COLS), 1)
    halved = col >= COL_G
    for z0, width in ((COL_B - ATTN_WIDTH, ATTN_WIDTH), (COL_C - CONV_WIDTH, CONV_WIDTH),
                      (COL_G - MLP_WIDTH, MLP_WIDTH)):
        halved = halved | ((col >= z0) & (col < z0 + width))
    return jnp.where(halved, 0.5, 1.0)


def _lo_lanes(shape):
    return lax.broadcasted_iota(jnp.int32, shape, len(shape) - 1) < HEAD_DIM


def _pair_rms(x, w):
    lo = _lo_lanes(x.shape)
    sq = x * x
    s_lo = jnp.sum(jnp.where(lo, sq, 0.0), axis=-1, keepdims=True)
    s_hi = jnp.sum(jnp.where(lo, 0.0, sq), axis=-1, keepdims=True)
    r = lax.rsqrt(jnp.where(lo, s_lo, s_hi) + HEAD_DIM * EPS)
    return (x * r) * (w * HEAD_DIM ** 0.5)


def _norm_heads(x, w):
    groups = [_pair_rms(x[:, g * LANES:(g + 1) * LANES], w) for g in range(x.shape[1] // LANES)]
    return groups[0] if len(groups) == 1 else jnp.concatenate(groups, axis=1)


def _slope(head):
    return 2.0 ** (-(head + 1))


def _branch_b(hb, conv_shift, conv_w_ref, wob_ref):
    gate_b = hb[:, 0:CONV_WIDTH]
    gate_c = hb[:, CONV_WIDTH:2 * CONV_WIDTH]
    h_b = hb[:, 2 * CONV_WIDTH:3 * CONV_WIDTH]
    z_b = hb[:, 3 * CONV_WIDTH:4 * CONV_WIDTH]
    conv_in = gate_c * h_b
    xm1, xm2 = conv_shift(conv_in)
    cw = conv_w_ref[...]
    conv_out = cw[0:1] * xm2 + cw[1:2] * xm1 + cw[2:3] * conv_in
    y_b = gate_b * conv_out
    return _dot((_silu_of_twice(z_b) * y_b).astype(BF16), wob_ref[...]), conv_in


def _branch_c(hc, mix_mask, mix_rows, spb, vnw_ref, woc_ref):
    rows = hc.shape[0]
    u = hc[:, 0:MLP_WIDTH]
    v_c = _rms(hc[:, MLP_WIDTH:2 * MLP_WIDTH], vnw_ref[...])
    z_c = hc[:, 2 * MLP_WIDTH:3 * MLP_WIDTH]
    vb = v_c.astype(BF16)
    gw = MLP_WIDTH // N_SPATIAL_GROUPS
    mixes = [jnp.where(mix_mask, mix_rows(g), 0.0).astype(BF16) for g in range(N_SPATIAL_GROUPS)]
    sp_chunks = []
    for c in range(rows // CHUNK):
        r0 = c * CHUNK
        parts = [_dot(mixes[g], vb[r0:r0 + CHUNK, g * gw:(g + 1) * gw])
                 for g in range(N_SPATIAL_GROUPS)]
        sp_chunks.append(jnp.concatenate(parts, axis=1) + spb)
    sp = sp_chunks[0] if len(sp_chunks) == 1 else jnp.concatenate(sp_chunks, axis=0)
    y_c = u * sp
    return _dot((_silu_of_twice(z_c) * y_c).astype(BF16), woc_ref[...]), v_c


def _branch_a(y_a, z_a, woa_ref):
    return _dot((_silu_of_twice(z_a) * y_a).astype(BF16), woa_ref[...])


def _merge(x, gated_a, gated_b, gated_c, wo_ref):
    m = gated_a + gated_b + gated_c
    return x + _dot(m.astype(BF16), wo_ref[...])


def _prompt_kernel(sinks_ref, x_ref, norm_w_ref, w_in_ref, b_gate_ref, qw_ref, kw_ref, conv_w_ref,
                   vnw_ref, mix_ref, spb_ref, woa_ref, wob_ref, woc_ref, wo_ref,
                   y_ref, pk_ref, pv_ref, pc_ref,
                   bias_scr, ka_scr, kb_scr, vt_scr, carry_scr, *, rows, nblk, layer):
    i = pl.program_id(0)
    first = lax.rem(i, nblk) == 0
    nsub = rows // WINDOW
    stack = 2 * WINDOW

    @pl.when(i == 0)
    def _():
        key = lax.broadcasted_iota(jnp.int32, (2 * WINDOW, stack), 0)
        c = lax.broadcasted_iota(jnp.int32, (2 * WINDOW, stack), 1)
        left = c < WINDOW
        dist = jnp.where(left, c, c - WINDOW) + WINDOW - key
        band = (dist >= 0) & (dist < WINDOW)
        distf = dist.astype(F32)
        for flag in range(2):
            valid = band & (key >= WINDOW) if flag else band
            for h in range(N_KV_HEADS):
                for half in range(2):
                    slope = jnp.where(left, _slope(h * Q_GROUP + half), _slope(h * Q_GROUP + 2 + half))
                    bias_scr[flag * 4 + h * 2 + half] = jnp.where(valid, -(slope * distf) * LOG2E, NEG_INF)

    @pl.when(first)
    def _():
        zeros = jnp.zeros((WINDOW, KV_WIDTH), BF16)
        ka_scr[0:WINDOW, :] = zeros
        kb_scr[0:WINDOW, :] = zeros
        vt_scr[:, 0:WINDOW] = zeros
        carry_scr[...] = jnp.zeros(carry_scr.shape, F32)

    x = x_ref[...]
    xn = _rms(x, norm_w_ref[...]).astype(BF16)
    def proj(c0, c1):
        return _dot(xn, w_in_ref[:, c0:c1])

    def gate(idx):
        g = proj(COL_G + idx * D_MODEL, COL_G + (idx + 1) * D_MODEL)
        return _twice_sigmoid_of_twice(g + 0.5 * b_gate_ref[:, idx * D_MODEL:(idx + 1) * D_MODEL])

    ha = proj(COL_A, COL_B)
    hb = proj(COL_B, COL_C)
    q = _norm_heads(ha[:, 0:ATTN_WIDTH], qw_ref[...] * (HEAD_DIM ** -0.5 * LOG2E))
    k = _pair_rms(ha[:, ATTN_WIDTH:ATTN_WIDTH + KV_WIDTH], kw_ref[...])
    v = ha[:, ATTN_WIDTH + KV_WIDTH:ATTN_WIDTH + 2 * KV_WIDTH]
    z_a = ha[:, ATTN_WIDTH + 2 * KV_WIDTH:COL_B]

    pk_ref[0] = k[rows - WINDOW:rows, :]
    pv_ref[0] = v[rows - WINDOW:rows, :]
    ka_scr[WINDOW:WINDOW + rows, :] = k.astype(BF16)
    kb_scr[WINDOW:WINDOW + rows, :] = pltpu.roll(k, HEAD_DIM, axis=1).astype(BF16)
    for c in range(nsub):
        vt_scr[:, (c + 1) * WINDOW:(c + 2) * WINDOW] = jnp.transpose(
            v[c * WINDOW:(c + 1) * WINDOW, :]).astype(BF16)

    lo = _lo_lanes((WINDOW, LANES))
    scol = lax.broadcasted_iota(jnp.int32, (1, stack), 1) < WINDOW
    flag = first.astype(jnp.int32)
    carry = carry_scr[...]
    prev1 = carry[SUBLANES - 1:SUBLANES, :]
    prev2 = carry[SUBLANES - 2:SUBLANES - 1, :]

    def conv_shift(ci):
        rid = lax.broadcasted_iota(jnp.int32, ci.shape, 0)
        xm1 = jnp.where(rid == 0, prev1, pltpu.roll(ci, 1, axis=0))
        xm2 = jnp.where(rid == 0, prev2, jnp.where(rid == 1, prev1, pltpu.roll(ci, 2, axis=0)))
        return xm1, xm2

    mr = lax.broadcasted_iota(jnp.int32, (CHUNK, CHUNK), 0)
    mc = lax.broadcasted_iota(jnp.int32, (CHUNK, CHUNK), 1)
    done = {}
    stages = [
        lambda: done.update(hc=proj(COL_C, COL_G)),
        lambda: done.update(b=_branch_b(hb, conv_shift, conv_w_ref, wob_ref)),
        lambda: done.update(gated_b=gate(1) * done["b"][0]),
        lambda: done.update(c=_branch_c(done["hc"], mc <= mr, lambda g: mix_ref[g], spb_ref[...],
                                        vnw_ref, woc_ref)),
        lambda: done.update(gated_c=gate(2) * done["c"][0]),
        lambda: done.update(g0=gate(0)),
    ]
    slots = 2 * nsub
    order = list(range(0, slots, 2)) + list(range(1, slots, 2))
    per_slot = [0] * slots
    for n in range(len(stages)):
        per_slot[order[n % slots]] += 1

    def run_stages(slot):
        for _ in range(per_slot[slot]):
            stages.pop(0)()

    def score(c, h, half):
        r0 = c * WINDOW
        p0 = q[r0:r0 + WINDOW, (2 * h) * LANES:(2 * h + 1) * LANES]
        p1 = q[r0:r0 + WINDOW, (2 * h + 1) * LANES:(2 * h + 2) * LANES]
        keep = lo if half == 0 else jnp.logical_not(lo)
        qs = jnp.concatenate([jnp.where(keep, p0, 0.0), jnp.where(keep, p1, 0.0)], axis=0).astype(BF16)
        kk = (ka_scr if h == half else kb_scr)[r0:r0 + 2 * WINDOW, :]
        tbl = h * 2 + half
        bias = bias_scr[flag * 4 + tbl] if c == 0 else bias_scr[tbl]
        return _dot_nt(kk, qs) + bias

    def attend(s, c, h, half):
        vt = vt_scr[h * HEAD_DIM:(h + 1) * HEAD_DIM, c * WINDOW:(c + 2) * WINDOW]
        sink = jnp.where(scol, sinks_ref[layer, h * Q_GROUP + half],
                         sinks_ref[layer, h * Q_GROUP + 2 + half]) * LOG2E
        mx = jnp.maximum(jnp.max(s, axis=0, keepdims=True), sink)
        e = jnp.exp2(s - mx)
        den = jnp.sum(e, axis=0, keepdims=True) + jnp.exp2(sink - mx)
        return _dot(vt, e.astype(BF16)) * (1.0 / den)

    units = [(c, h, half) for c in range(nsub) for h in range(N_KV_HEADS) for half in range(2)]
    pending = score(*units[0])
    outs = {}
    y_rows = []
    for n, (c, h, half) in enumerate(units):
        upcoming = score(*units[n + 1]) if n + 1 < len(units) else None
        if n % 2 == 1:
            run_stages(n // 2)
        outs[h, half] = attend(pending, c, h, half)
        pending = upcoming
        if (h, half) == (N_KV_HEADS - 1, 1):
            pairs = []
            for hh in range(N_KV_HEADS):
                for p in range(2):
                    cols = slice(p * WINDOW, (p + 1) * WINDOW)
                    pairs.append(jnp.transpose(
                        jnp.concatenate([outs[hh, 0][:, cols], outs[hh, 1][:, cols]], axis=0)))
            y_rows.append(jnp.concatenate(pairs, axis=1))
    y_a = y_rows[0] if nsub == 1 else jnp.concatenate(y_rows, axis=0)

    ka_scr[0:WINDOW, :] = ka_scr[rows:rows + WINDOW, :]
    kb_scr[0:WINDOW, :] = kb_scr[rows:rows + WINDOW, :]
    vt_scr[:, 0:WINDOW] = vt_scr[:, rows:rows + WINDOW]

    gated_a = done["g0"] * _branch_a(y_a, z_a, woa_ref)
    y_ref[...] = _merge(x, gated_a, done["gated_b"], done["gated_c"], wo_ref)
    conv_in = done["b"][1]
    last = conv_in[rows - SUBLANES:rows, :]
    carry_scr[...] = last
    pc_ref[0] = last


def _stage_bf16(streams, layer):
    def copy(s, k):
        jobs, stage, sem = streams[s]
        src, _, r0, _ = jobs[k]
        slot = k % stage.shape[0]
        return pltpu.make_async_copy(src.at[layer, pl.ds(r0, stage.shape[1]), :], stage.at[slot], sem.at[slot])

    for s, (jobs, stage, _) in enumerate(streams):
        for k in range(min(stage.shape[0] - 1, len(jobs))):
            copy(s, k).start()
    for k in range(max(len(jobs) for jobs, _, _ in streams)):
        for s, (jobs, stage, _) in enumerate(streams):
            if k >= len(jobs):
                continue
            nslots, chunk_rows = stage.shape[0], stage.shape[1]
            if k + nslots - 1 < len(jobs):
                copy(s, k + nslots - 1).start()
            copy(s, k).wait()
            _, dst, r0, scale = jobs[k]
            dst[r0:r0 + chunk_rows, :] = (stage[k % nslots] * scale).astype(BF16)


def _sample_kernel(sinks_ref, x_ref, ck_ref, cv_ref, st_ref, norm_w_ref, b_gate_ref, qw_ref,
                   kw_ref, conv_w_ref, vnw_ref, mix_ref, spb_ref,
                   w_in_hbm, woa_hbm, wob_hbm, woc_hbm, wo_hbm,
                   y_ref, sk_ref, sv_ref, ci_ref, scv_ref,
                   w_in_out, woa_out, wob_out, woc_out, wo_out,
                   x_scr, bold_scr, bnew_scr, q_scr, en_scr, oo_scr, inv_scr,
                   w_in_ref, woa_ref, wob_ref, woc_ref, wo_ref, stage_in, stage_sq,
                   sem_in, sem_sq, sem_out, st_scr, *, dec_seq):
    layer = pl.program_id(0)
    j = pl.program_id(1)
    rows = SAMPLE_ROWS
    nseq = rows // dec_seq
    stack = N_HEADS * dec_seq
    w = WINDOW

    @pl.when((layer == 0) & (j == 0))
    def _():
        st_scr[...] = jnp.zeros(st_scr.shape, F32)
        r = lax.broadcasted_iota(jnp.int32, (stack, LANES), 0)
        col = lax.broadcasted_iota(jnp.int32, (stack, LANES), 1)
        tok = lax.rem(r, dec_seq)
        head = r // dec_seq
        slope = jnp.zeros((stack, LANES), F32)
        for n in range(N_HEADS):
            slope = jnp.where(head == n, _slope(n), slope)
        dist = tok + w - col
        bold_scr[...] = jnp.where(dist < WINDOW, -(slope * dist.astype(F32)), NEG_INF)
        kseq = col // dec_seq
        dist = tok - lax.rem(col, dec_seq)
        pen = -(slope * dist.astype(F32))
        for b in range(nseq):
            bnew_scr[b] = jnp.where((kseq == b) & (dist >= 0), pen, NEG_INF)

    r0 = pl.multiple_of(j * rows, rows)

    resident = ((w_in_hbm, w_in_ref, w_in_out), (woa_hbm, woa_ref, woa_out), (wob_hbm, wob_ref, wob_out),
                (woc_hbm, woc_ref, woc_out), (wo_hbm, wo_ref, wo_out))

    def publish(n, lyr):
        return pltpu.make_async_copy(resident[n][1], resident[n][2].at[lyr], sem_out.at[n])

    @pl.when(j == 0)
    def _():
        @pl.when(layer > 0)
        def _():
            for n in range(len(resident)):
                publish(n, layer - 1).wait()

        halved = _halved_columns()
        wide = [(w_in_hbm, w_in_ref, r, halved) for r in range(0, w_in_ref.shape[0], stage_in.shape[1])]
        square = [(src, dst, r, 0.5 if dst is wo_ref else 1.0) for src, dst, _ in resident[1:]
                  for r in range(0, dst.shape[0], stage_sq.shape[1])]
        _stage_bf16([(wide, stage_in, sem_in), (square, stage_sq, sem_sq)], layer)
        for n in range(len(resident)):
            publish(n, layer).start()

    @pl.when(layer == 0)
    def _():
        x_scr[pl.ds(r0, rows), :] = x_ref[...]

    x = x_scr[pl.ds(r0, rows), :]
    xn = _rms(x, norm_w_ref[...]).astype(BF16)
    def proj(c0, c1):
        return _dot(xn, w_in_ref[:, c0:c1])

    def gate(idx):
        g = proj(COL_G + idx * D_MODEL, COL_G + (idx + 1) * D_MODEL)
        return _twice_sigmoid_of_twice(g + 0.5 * b_gate_ref[:, idx * D_MODEL:(idx + 1) * D_MODEL])

    ha = proj(COL_A, COL_B)
    hb = proj(COL_B, COL_C)
    q = _norm_heads(ha[:, 0:ATTN_WIDTH], qw_ref[...] * (HEAD_DIM ** -0.5))
    k = _pair_rms(ha[:, ATTN_WIDTH:ATTN_WIDTH + KV_WIDTH], kw_ref[...])
    v = ha[:, ATTN_WIDTH + KV_WIDTH:ATTN_WIDTH + 2 * KV_WIDTH]
    z_a = ha[:, ATTN_WIDTH + 2 * KV_WIDTH:COL_B]

    lo = _lo_lanes((rows, LANES))
    for h in range(N_KV_HEADS):
        keep = lo if h == 0 else jnp.logical_not(lo)
        for g in range(Q_GROUP):
            pair = q[:, (h * 2 + g // 2) * LANES:(h * 2 + g // 2 + 1) * LANES]
            if g % 2 != h:
                pair = pltpu.roll(pair, HEAD_DIM, axis=1)
            piece = jnp.where(keep, pair, 0.0).reshape(nseq, dec_seq, LANES)
            s0 = (h * Q_GROUP + g) * dec_seq
            q_scr[:, s0:s0 + dec_seq, :] = piece

    kt_new = jnp.transpose(k)
    vt_new = jnp.transpose(v)
    qall = q_scr[...].reshape(nseq * stack, LANES).astype(BF16)
    s_new_all = _dot(qall, kt_new.astype(BF16)).reshape(nseq, stack, LANES)

    srow = lax.broadcasted_iota(jnp.int32, (stack, 1), 0) // dec_seq
    sink = jnp.zeros((stack, 1), F32)
    for n in range(N_HEADS):
        sink = jnp.where(srow == n, sinks_ref[layer, n], sink)
    bias_old = bold_scr[...]
    lane = lax.broadcasted_iota(jnp.int32, (KV_WIDTH, w), 1)
    s_olds = [_dot(q_scr[b].astype(BF16), ck_ref[b].reshape(KV_WIDTH, w).astype(BF16)) + bias_old
              for b in range(nseq)]
    hc = proj(COL_C, COL_G)
    gates = []
    for b in range(nseq):
        if b % (nseq // 4) == 0 and 0 < b:
            gates.append(gate(len(gates)))
        kt = ck_ref[b].reshape(KV_WIDTH, w)
        vt = cv_ref[b].reshape(KV_WIDTH, w)
        s_old = s_olds[b]
        s_new = s_new_all[b] + bnew_scr[b]
        mx = jnp.maximum(jnp.maximum(jnp.max(s_old, axis=-1, keepdims=True),
                                     jnp.max(s_new, axis=-1, keepdims=True)), sink)
        e_old = jnp.exp(s_old - mx)
        e_new = jnp.exp(s_new - mx)
        den = (jnp.sum(e_old, axis=-1, keepdims=True) + jnp.sum(e_new, axis=-1, keepdims=True)
               + jnp.exp(sink - mx))
        en_scr[b] = e_new
        oo_scr[b] = _dot_nt(e_old.astype(BF16), vt.astype(BF16))
        inv_scr[b] = jnp.broadcast_to(1.0 / den, (stack, LANES))
        shift = w - dec_seq - b * dec_seq
        newk = pltpu.roll(kt_new, shift, axis=1) if shift else kt_new
        newv = pltpu.roll(vt_new, shift, axis=1) if shift else vt_new
        keep_old = lane < w - dec_seq
        sk_ref[b] = jnp.where(keep_old, pltpu.roll(kt, w - dec_seq, axis=1), newk).reshape(
            N_KV_HEADS, HEAD_DIM, w)
        sv_ref[b] = jnp.where(keep_old, pltpu.roll(vt, w - dec_seq, axis=1), newv).reshape(
            N_KV_HEADS, HEAD_DIM, w)

    o_new = _dot(en_scr[...].reshape(nseq * stack, LANES).astype(BF16), v.astype(BF16))
    o = (oo_scr[...] + o_new.reshape(nseq, stack, LANES)) * inv_scr[...]
    pairs = []
    for h in range(N_KV_HEADS):
        heads = []
        for g in range(Q_GROUP):
            s0 = (h * Q_GROUP + g) * dec_seq
            og = o[:, s0:s0 + dec_seq, :].reshape(rows, LANES)
            if g % 2 != h:
                og = pltpu.roll(og, HEAD_DIM, axis=1)
            heads.append(og)
        pairs.append(jnp.where(lo, heads[0], heads[1]))
        pairs.append(jnp.where(lo, heads[2], heads[3]))
    y_a = jnp.concatenate(pairs, axis=1)

    st_scr[:, 0:CONV_K - 1, :] = st_ref[...]
    st = st_scr[...].reshape(rows, CONV_WIDTH)

    def conv_shift(ci):
        t = lax.rem(lax.broadcasted_iota(jnp.int32, ci.shape, 0), dec_seq)
        xm1 = jnp.where(t == 0, pltpu.roll(st, rows - 1, axis=0), pltpu.roll(ci, 1, axis=0))
        xm2 = jnp.where(t < CONV_K - 1, st, pltpu.roll(ci, 2, axis=0))
        return xm1, xm2

    mr = lax.broadcasted_iota(jnp.int32, (CHUNK, CHUNK), 0)
    mc = lax.broadcasted_iota(jnp.int32, (CHUNK, CHUNK), 1)
    mix_mask = (mr // dec_seq == mc // dec_seq) & (mc <= mr)
    b_out, conv_in = _branch_b(hb, conv_shift, conv_w_ref, wob_ref)
    reps = CHUNK // dec_seq

    def mix_rows(g):
        return jnp.broadcast_to(mix_ref[g][None], (reps, dec_seq, CHUNK)).reshape(CHUNK, CHUNK)

    spb = jnp.broadcast_to(spb_ref[...][None], (reps, dec_seq, MLP_WIDTH)).reshape(CHUNK, MLP_WIDTH)
    c_out, v_c = _branch_c(hc, mix_mask, mix_rows, spb, vnw_ref, woc_ref)
    y = _merge(x, gates[0] * _branch_a(y_a, z_a, woa_ref), gates[1] * b_out, gates[2] * c_out, wo_ref)
    x_scr[pl.ds(r0, rows), :] = y

    @pl.when(layer == pl.num_programs(0) - 1)
    def _():
        y_ref[...] = y
    ci_ref[...] = conv_in.reshape(nseq, dec_seq, CONV_WIDTH)[:, dec_seq - (CONV_K - 1):, :]
    scv_ref[...] = v_c

    @pl.when((layer == pl.num_programs(0) - 1) & (j == pl.num_programs(1) - 1))
    def _():
        for n in range(len(resident)):
            publish(n, layer).wait()


SMALL_WEIGHT_NAMES = ("norm_w", "b_gate", "qw", "kw", "conv_w", "vnw", "mix", "spb")
BIG_WEIGHT_NAMES = ("w_in", "woa", "wob", "woc", "wo")
STAGE_SLOTS = 8
STAGE_ROWS_W_IN = 32
STAGE_ROWS_SQUARE = 128
WEIGHT_NAMES = ("norm_w", "w_in", "b_gate", "qw", "kw", "conv_w", "vnw", "mix", "spb",
                "woa", "wob", "woc", "wo")


def _layer_spec(arr, layer_of):
    nd = arr.ndim - 1
    return pl.BlockSpec((None,) + arr.shape[1:], lambda *g, _nd=nd: (layer_of(*g),) + (0,) * _nd,
                        pipeline_mode=pl.Buffered(1))


def _prompt_layer(x, wts, layer, batch):
    n, d = x.shape
    rows = PROMPT_ROWS
    nblk = n // batch // rows
    smem = pl.BlockSpec(memory_space=pltpu.SMEM)
    in_specs = ([smem, pl.BlockSpec((rows, d), lambda i: (i, 0))]
                + [_layer_spec(wts[name], lambda i: layer) for name in WEIGHT_NAMES])
    out_shape = (jax.ShapeDtypeStruct((n, d), F32),
                 jax.ShapeDtypeStruct((batch, WINDOW, KV_WIDTH), F32),
                 jax.ShapeDtypeStruct((batch, WINDOW, KV_WIDTH), F32),
                 jax.ShapeDtypeStruct((batch, SUBLANES, CONV_WIDTH), F32))
    out_specs = (pl.BlockSpec((rows, d), lambda i: (i, 0)),
                 pl.BlockSpec((1, WINDOW, KV_WIDTH), lambda i: (i // nblk, 0, 0)),
                 pl.BlockSpec((1, WINDOW, KV_WIDTH), lambda i: (i // nblk, 0, 0)),
                 pl.BlockSpec((1, SUBLANES, CONV_WIDTH), lambda i: (i // nblk, 0, 0)))
    n_tables = 2 * N_KV_HEADS * 2
    scratch = [pltpu.VMEM((n_tables, 2 * WINDOW, 2 * WINDOW), F32)]
    scratch += [pltpu.VMEM((WINDOW + rows, KV_WIDTH), BF16) for _ in range(2)]
    scratch += [pltpu.VMEM((KV_WIDTH, WINDOW + rows), BF16)]
    scratch += [pltpu.VMEM((SUBLANES, CONV_WIDTH), F32)]
    return pl.pallas_call(
        functools.partial(_prompt_kernel, rows=rows, nblk=nblk, layer=layer),
        grid=(n // rows,),
        in_specs=in_specs, out_specs=out_specs, out_shape=out_shape, scratch_shapes=scratch,
        compiler_params=pltpu.CompilerParams(dimension_semantics=("arbitrary",),
                                             vmem_limit_bytes=VMEM_LIMIT_BYTES),
        name="prompt_layer",
    )(wts["sinks"], x, *[wts[name] for name in WEIGHT_NAMES])


def _sample_layers(x, ck, cv, st, wts, dec_seq):
    n, d = x.shape
    depth = ck.shape[0]
    rows = SAMPLE_ROWS
    nseq = rows // dec_seq
    stack = N_HEADS * dec_seq
    w = ck.shape[-1]
    smem = pl.BlockSpec(memory_space=pltpu.SMEM)
    cache_blk = pl.BlockSpec((None, nseq, N_KV_HEADS, HEAD_DIM, w), lambda l, j: (l, j, 0, 0, 0))
    rows_blk = lambda width: pl.BlockSpec((None, rows, width), lambda l, j: (l, j, 0))
    nblk = n // rows
    x_blk = pl.BlockSpec((rows, d), lambda l, j: (jnp.where(l == 0, j, nblk - 1), 0))
    y_blk = pl.BlockSpec((rows, d), lambda l, j: (jnp.where(l == depth - 1, j, 0), 0))
    hbm = pl.BlockSpec(memory_space=pl.ANY)
    state_blk = pl.BlockSpec((None, nseq, CONV_K - 1, CONV_WIDTH), lambda l, j: (l, j, 0, 0))
    in_specs = ([smem, x_blk, cache_blk, cache_blk, state_blk]
                + [_layer_spec(wts[name], lambda l, j: l) for name in SMALL_WEIGHT_NAMES]
                + [hbm] * len(BIG_WEIGHT_NAMES))
    out_shape = (jax.ShapeDtypeStruct((n, d), F32),
                 jax.ShapeDtypeStruct(ck.shape, F32),
                 jax.ShapeDtypeStruct(cv.shape, F32),
                 jax.ShapeDtypeStruct(st.shape, F32),
                 jax.ShapeDtypeStruct((depth, n, MLP_WIDTH), F32)
                 ) + tuple(jax.ShapeDtypeStruct(wts[name].shape, BF16) for name in BIG_WEIGHT_NAMES)
    out_specs = ((y_blk, cache_blk, cache_blk, state_blk, rows_blk(MLP_WIDTH))
                 + (hbm,) * len(BIG_WEIGHT_NAMES))
    scratch = [pltpu.VMEM((n, d), F32),
               pltpu.VMEM((stack, LANES), F32),
               pltpu.VMEM((nseq, stack, LANES), F32),
               pltpu.VMEM((nseq, stack, LANES), F32),
               pltpu.VMEM((nseq, stack, LANES), F32),
               pltpu.VMEM((nseq, stack, LANES), F32),
               pltpu.VMEM((nseq, stack, LANES), F32)]
    scratch += [pltpu.VMEM(wts[name].shape[1:], BF16) for name in BIG_WEIGHT_NAMES]
    scratch += [pltpu.VMEM((STAGE_SLOTS, STAGE_ROWS_W_IN, IN_COLS), F32),
                pltpu.VMEM((STAGE_SLOTS, STAGE_ROWS_SQUARE, d), F32),
                pltpu.SemaphoreType.DMA((STAGE_SLOTS,)), pltpu.SemaphoreType.DMA((STAGE_SLOTS,)),
                pltpu.SemaphoreType.DMA((len(BIG_WEIGHT_NAMES),)),
                pltpu.VMEM((nseq, dec_seq, CONV_WIDTH), F32)]
    return pl.pallas_call(
        functools.partial(_sample_kernel, dec_seq=dec_seq),
        grid=(depth, n // rows),
        in_specs=in_specs, out_specs=out_specs, out_shape=out_shape, scratch_shapes=scratch,
        compiler_params=pltpu.CompilerParams(dimension_semantics=("arbitrary", "arbitrary"),
                                             vmem_limit_bytes=VMEM_LIMIT_BYTES),
        name="sample_layers",
    )(wts["sinks"], x, ck, cv, st, *[wts[name] for name in SMALL_WEIGHT_NAMES + BIG_WEIGHT_NAMES])


def kernel(x_prompt, x_sample, cache_k, cache_v, state_conv, norm_w, w_in, b_gate, q_norm_w, k_norm_w,
           sinks, conv_w, v_norm_w, w_spatial, b_spatial, w_out_a, w_out_b, w_out_c, w_o):
    batch, seq, d = x_prompt.shape
    dec_batch, dec_seq, _ = x_sample.shape
    depth = w_in.shape[0]
    w_buf = cache_k.shape[2]
    assert d == D_MODEL and seq % PROMPT_ROWS == 0 and (dec_batch * dec_seq) % SAMPLE_ROWS == 0
    assert w_buf == WINDOW and dec_seq == SUBLANES and w_in.shape[2] == IN_COLS

    gw = MLP_WIDTH // N_SPATIAL_GROUPS
    reps = CHUNK // dec_seq
    spb_p = jnp.repeat(jnp.swapaxes(b_spatial, 1, 2), gw, axis=2)
    common = {
        "sinks": sinks,
        "norm_w": norm_w[:, None, :], "w_in": w_in, "b_gate": b_gate[:, None, :],
        "qw": jnp.tile(q_norm_w, (1, LANES // HEAD_DIM))[:, None, :],
        "kw": jnp.tile(k_norm_w, (1, LANES // HEAD_DIM))[:, None, :],
        "conv_w": conv_w, "vnw": v_norm_w[:, None, :],
        "woa": w_out_a, "wob": w_out_b, "woc": w_out_c, "wo": w_o,
    }
    wts_s = dict(common,
                 mix=jnp.tile(w_spatial[:, :, :dec_seq, :dec_seq], (1, 1, 1, reps)),
                 spb=spb_p[:, :dec_seq, :])

    ck = jnp.transpose(cache_k, (0, 1, 3, 4, 2))
    cv = jnp.transpose(cache_v, (0, 1, 3, 4, 2))

    ys, sk, sv, sc, scv, *big_bf16 = _sample_layers(x_sample.reshape(dec_batch * dec_seq, d), ck, cv,
                                                    state_conv, wts_s, dec_seq)
    wts_p = dict(common, mix=w_spatial, spb=spb_p, **dict(zip(BIG_WEIGHT_NAMES, big_bf16)))
    sk = jnp.transpose(sk, (0, 1, 4, 2, 3))
    sv = jnp.transpose(sv, (0, 1, 4, 2, 3))
    scv = scv.reshape(depth, dec_batch, dec_seq, MLP_WIDTH)

    yp = x_prompt.reshape(batch * seq, d)
    pk, pv, pc = [], [], []
    for l in range(depth):
        yp, k_l, v_l, c_l = _prompt_layer(yp, wts_p, l, batch)
        pk.append(k_l.reshape(batch, WINDOW, N_KV_HEADS, HEAD_DIM))
        pv.append(v_l.reshape(batch, WINDOW, N_KV_HEADS, HEAD_DIM))
        pc.append(c_l[:, SUBLANES - (CONV_K - 1):, :])

    return (yp.reshape(batch, seq, d), ys.reshape(dec_batch, dec_seq, d), jnp.stack(pk), jnp.stack(pv),
            jnp.stack(pc), sk, sv, sc, scv)
```
